```python
import numpy as np
import jax, jax.numpy as jnp
from jax import lax

D_MODEL = 4096
BATCH = 4
SEQ = 2048
DEPTH = 2

HEAD_DIM = 128
H_GDN = 16
GDN_CONV = 4
GDN_CHUNK = 64
H_NSA = 16
G_NSA = 4
HPG_NSA = H_NSA // G_NSA
L_CMP = 32
S_CMP = 16
L_SLC = 64
N_SEL = 8
WINDOW = 512
SLC_QBLOCK = 64
FORCE_SCORE = 1e4
H_HGRN = 16
HGRN_CHUNK = 64
H_FOX = 16
ATT_QBLOCK = 128
D_FF = 11008
FFN_CONV = 3
D_PLE = 256
EPS = 1e-6

N_AB = (DEPTH + 1) // 2
N_CD = DEPTH // 2
AB_SPLITS = [3 * H_GDN * HEAD_DIM, H_GDN, H_GDN, H_GDN * HEAD_DIM,
             H_NSA * HEAD_DIM, 6 * G_NSA * HEAD_DIM, 3 * H_NSA]
CD_SPLITS = [H_HGRN * HEAD_DIM] * 4 + [3 * H_FOX * HEAD_DIM, H_FOX]
AB_IN = sum(AB_SPLITS)
CD_IN = sum(CD_SPLITS)
AB_OUT = (H_GDN + H_NSA) * HEAD_DIM
CD_OUT = (H_HGRN + H_FOX) * HEAD_DIM

kernel_name = 'hybrid_gdn_nsa_hgrn2_fox_block'

F32 = jnp.float32


def rmsnorm(z, w):
    zf = z.astype(F32)
    y = zf * lax.rsqrt(jnp.mean(zf * zf, -1, keepdims=True) + EPS)
    return (y * w.astype(F32)).astype(z.dtype)


def l2norm(z):
    return z * lax.rsqrt(jnp.sum(z * z, -1, keepdims=True) + EPS)


def masked_softmax(s, mask):
    s = jnp.where(mask, s, -jnp.inf)
    m = jnp.max(s, -1, keepdims=True)
    e = jnp.exp(s - jnp.where(jnp.isfinite(m), m, 0.0))
    d = jnp.sum(e, -1, keepdims=True)
    return e / jnp.where(d > 0, d, 1.0)


def causal_dwconv(z, w):
    k, t = w.shape[0], z.shape[1]
    zp = jnp.pad(z, ((0, 0), (k - 1, 0), (0, 0)))
    y = zp[:, :t] * w[0]
    for j in range(1, k):
        y = y + zp[:, j:j + t] * w[j]
    return y


def split_cols(z, sizes):
    return jnp.split(z, np.cumsum(sizes)[:-1].tolist(), axis=-1)


def split_blocks(z, axis, size):
    shp = z.shape
    z = z.reshape(shp[:axis] + (shp[axis] // size, size) + shp[axis + 1:])
    return jnp.moveaxis(z, axis, 0)


def merge_blocks(z, axis):
    z = jnp.moveaxis(z, 0, axis)
    shp = z.shape
    return z.reshape(shp[:axis] + (shp[axis] * shp[axis + 1],) + shp[axis + 2:])


def gated_deltanet(qkv, a, b, gate, conv_w, a_log, dt_bias, norm_w):
    bsz, t, _ = qkv.shape
    c = GDN_CHUNK
    z = jax.nn.silu(causal_dwconv(qkv, conv_w)).astype(F32)
    z = z.reshape(bsz, t, 3, H_GDN, HEAD_DIM).transpose(2, 0, 3, 1, 4)
    q = l2norm(z[0]) * HEAD_DIM ** -0.5
    k = l2norm(z[1])
    v = z[2]
    beta = jax.nn.sigmoid(b.astype(F32)).transpose(0, 2, 1)
    g = (-jnp.exp(a_log.astype(F32)) * jax.nn.softplus(a.astype(F32) + dt_bias.astype(F32))).transpose(0, 2, 1)
    qc, kc, vc = (split_blocks(u, 2, c) for u in (q, k, v))
    bc = split_blocks(beta, 2, c)
    gam = jnp.cumsum(split_blocks(g, 2, c), -1)
    incl = jnp.tril(jnp.ones((c, c), bool))
    strict = jnp.tril(jnp.ones((c, c), bool), -1)
    dec = jnp.exp(jnp.where(incl, gam[..., :, None] - gam[..., None, :], -jnp.inf))
    a_mat = jnp.where(strict, bc[..., :, None] * dec * jnp.einsum('nbhrd,nbhjd->nbhrj', kc, kc), 0.0)
    rhs = jnp.concatenate([vc * bc[..., None], kc * (bc * jnp.exp(gam))[..., None]], -1)
    sol = lax.linalg.triangular_solve(a_mat + jnp.eye(c, dtype=F32), rhs, left_side=True,
                                      lower=True, unit_diagonal=True)
    u0, w = sol[..., :HEAD_DIM], sol[..., HEAD_DIM:]
    qk = dec * jnp.einsum('nbhrd,nbhjd->nbhrj', qc, kc)
    q_dec = qc * jnp.exp(gam)[..., None]
    k_dec = kc * jnp.exp(gam[..., -1:] - gam)[..., None]
    g_last = jnp.exp(gam[..., -1])

    def step(s, xs):
        u0_n, w_n, qk_n, qd_n, kd_n, gl_n = xs
        u = u0_n - jnp.einsum('bhcd,bhde->bhce', w_n, s)
        o = jnp.einsum('bhcd,bhde->bhce', qd_n, s) + jnp.einsum('bhcj,bhje->bhce', qk_n, u)
        s = s * gl_n[..., None, None] + jnp.einsum('bhcd,bhce->bhde', kd_n, u)
        return s, o

    s0 = jnp.zeros((bsz, H_GDN, HEAD_DIM, HEAD_DIM), F32)
    _, o = lax.scan(step, s0, (u0, w, qk, q_dec, k_dec, g_last))
    o = merge_blocks(o, 2).transpose(0, 2, 1, 3)
    o = rmsnorm(o, norm_w) * jax.nn.silu(gate.astype(F32).reshape(bsz, t, H_GDN, HEAD_DIM))
    return o.reshape(bsz, t, H_GDN * HEAD_DIM)


def nsa_attention(q, kv, gates, pe_k, pe_v, wk1, wk2, wv1, wv2):
    bsz, t, _ = q.shape
    q = q.astype(F32).reshape(bsz, t, G_NSA, HPG_NSA, HEAD_DIM).transpose(0, 2, 3, 1, 4) * HEAD_DIM ** -0.5
    k_c, v_c, k_s, v_s, k_w, v_w = kv.astype(F32).reshape(bsz, t, 6, G_NSA, HEAD_DIM).transpose(2, 0, 3, 1, 4)
    pos = jnp.arange(t, dtype=jnp.int32)

    n_cmp = (t - L_CMP) // S_CMP + 1
    cmp_idx = np.arange(n_cmp)[:, None] * S_CMP + np.arange(L_CMP)[None, :]

    def compress(z, pe, w1, w2):
        zb = (z[:, :, cmp_idx] + pe.astype(F32)).reshape(bsz, G_NSA, n_cmp, L_CMP * HEAD_DIM)
        return jax.nn.silu(zb @ w1.astype(F32)) @ w2.astype(F32)

    kc = compress(k_c, pe_k, wk1, wk2)
    vc = compress(v_c, pe_v, wv1, wv2)
    cmp_mask = jnp.asarray(cmp_idx[:, -1], jnp.int32)[None, :] <= pos[:, None]
    p_cmp = masked_softmax(jnp.einsum('bgptd,bgnd->bgptn', q, kc), cmp_mask)
    o_cmp = jnp.einsum('bgptn,bgnd->bgptd', p_cmp, vc)

    n_slc = t // L_SLC
    cs = np.arange(n_cmp) * S_CMP
    ss = np.arange(n_slc) * L_SLC
    overlap = ((cs[:, None] < ss[None, :] + L_SLC) & (cs[:, None] + L_CMP > ss[None, :])).astype(np.float32)
    imp = jnp.einsum('bgptn,nm->bgtm', p_cmp, jnp.asarray(overlap))
    blk = jnp.arange(n_slc, dtype=jnp.int32)[None, :]
    cur = (pos // L_SLC)[:, None]
    valid = blk <= cur
    forced = (blk == 0) | (blk == cur) | (blk == cur - 1)
    score = jnp.where(valid, jnp.where(forced, FORCE_SCORE, imp), -jnp.inf)
    n_top = min(N_SEL, n_slc)
    top_val, top_idx = lax.top_k(score, n_top)
    top_ok = jnp.isfinite(top_val)
    kb = k_s.reshape(bsz, G_NSA, n_slc, L_SLC, HEAD_DIM)
    vb = v_s.reshape(bsz, G_NSA, n_slc, L_SLC, HEAD_DIM)
    gather = jax.vmap(jax.vmap(lambda z, i: z[i]))
    offs = jnp.arange(L_SLC, dtype=jnp.int32)
    n_keys = n_top * L_SLC

    def slc_block(xs):
        qb, ib, okb, tb = xs
        ks = gather(kb, ib).reshape(bsz, G_NSA, SLC_QBLOCK, n_keys, HEAD_DIM)
        vs = gather(vb, ib).reshape(bsz, G_NSA, SLC_QBLOCK, n_keys, HEAD_DIM)
        tok = (ib[..., None] * L_SLC + offs).reshape(bsz, G_NSA, SLC_QBLOCK, n_keys)
        ok = jnp.repeat(okb, L_SLC, axis=-1) & (tok <= tb[:, None])
        pr = masked_softmax(jnp.einsum('bgpqd,bgqmd->bgpqm', qb, ks), ok[:, :, None])
        return jnp.einsum('bgpqm,bgqmd->bgpqd', pr, vs)

    o_slc = merge_blocks(lax.map(slc_block, (split_blocks(q, 3, SLC_QBLOCK),
                                             split_blocks(top_idx, 2, SLC_QBLOCK),
                                             split_blocks(top_ok, 2, SLC_QBLOCK),
                                             pos.reshape(-1, SLC_QBLOCK))), 3)

    kwp = jnp.pad(k_w, ((0, 0), (0, 0), (WINDOW, 0), (0, 0)))
    vwp = jnp.pad(v_w, ((0, 0), (0, 0), (WINDOW, 0), (0, 0)))
    span = WINDOW + ATT_QBLOCK

    def win_block(xs):
        qb, q0 = xs
        kk = lax.dynamic_slice_in_dim(kwp, q0, span, axis=2)
        vv = lax.dynamic_slice_in_dim(vwp, q0, span, axis=2)
        tq = q0 + jnp.arange(ATT_QBLOCK, dtype=jnp.int32)
        ts = q0 - WINDOW + jnp.arange(span, dtype=jnp.int32)
        d = tq[:, None] - ts[None, :]
        m = (d >= 0) & (d < WINDOW) & (ts[None, :] >= 0)
        pr = masked_softmax(jnp.einsum('bgpqd,bgkd->bgpqk', qb, kk), m)
        return jnp.einsum('bgpqk,bgkd->bgpqd', pr, vv)

    q0s = jnp.arange(t // ATT_QBLOCK, dtype=jnp.int32) * ATT_QBLOCK
    o_win = merge_blocks(lax.map(win_block, (split_blocks(q, 3, ATT_QBLOCK), q0s)), 3)

    gt = jax.nn.sigmoid(gates.astype(F32)).reshape(bsz, t, 3, G_NSA, HPG_NSA).transpose(2, 0, 3, 4, 1)[..., None]
    o = gt[0] * o_cmp + gt[1] * o_slc + gt[2] * o_win
    return o.transpose(0, 3, 1, 2, 4).reshape(bsz, t, H_NSA * HEAD_DIM)


def hgrn2(q, f, i, g, lb, norm_w):
    bsz, t, _ = q.shape
    c = HGRN_CHUNK
    shp = (bsz, t, H_HGRN, HEAD_DIM)
    lb = lb.astype(F32).reshape(H_HGRN, HEAD_DIM)
    fx = f.astype(F32).reshape(shp)
    log_f = jnp.logaddexp(jnp.log(lb), jnp.log1p(-lb) + jax.nn.log_sigmoid(fx))
    k = (1.0 - lb) * jax.nn.sigmoid(-fx)
    q = jax.nn.silu(q.astype(F32)).reshape(shp)
    v = i.astype(F32).reshape(shp)
    qc, kc, vc, lfc = (split_blocks(u.transpose(0, 2, 1, 3), 2, c) for u in (q, k, v, log_f))
    bcum = jnp.cumsum(lfc, axis=3)
    incl = jnp.tril(jnp.ones((c, c), bool))[:, :, None]

    def step(s, xs):
        q_n, k_n, v_n, b_n = xs
        dec = jnp.exp(jnp.where(incl, b_n[:, :, :, None, :] - b_n[:, :, None, :, :], -jnp.inf))
        att = jnp.einsum('bhrd,bhjd,bhrjd->bhrj', q_n, k_n, dec)
        b_last = b_n[:, :, -1]
        o = jnp.einsum('bhrd,bhde->bhre', q_n * jnp.exp(b_n), s) + jnp.einsum('bhrj,bhje->bhre', att, v_n)
        s = s * jnp.exp(b_last)[..., None] + jnp.einsum('bhjd,bhje->bhde', k_n * jnp.exp(b_last[:, :, None] - b_n), v_n)
        return s, o

    s0 = jnp.zeros((bsz, H_HGRN, HEAD_DIM, HEAD_DIM), F32)
    _, o = lax.scan(step, s0, (qc, kc, vc, bcum))
    o = merge_blocks(o, 2).transpose(0, 2, 1, 3)
    o = rmsnorm(o, norm_w) * jax.nn.silu(g.astype(F32).reshape(shp))
    return o.reshape(bsz, t, H_HGRN * HEAD_DIM)


def forgetting_attention(qkv, fgate, f_bias):
    bsz, t, _ = qkv.shape
    z = qkv.astype(F32).reshape(bsz, t, 3, H_FOX, HEAD_DIM).transpose(2, 0, 3, 1, 4)
    q, k, v = z[0] * HEAD_DIM ** -0.5, z[1], z[2]
    cum = jnp.cumsum(jax.nn.log_sigmoid(fgate.astype(F32) + f_bias.astype(F32)), axis=1).transpose(0, 2, 1)
    pos = jnp.arange(t, dtype=jnp.int32)

    def blk(xs):
        qb, cb, tq = xs
        s = jnp.einsum('bhqd,bhkd->bhqk', qb, k) + cb[..., None] - cum[:, :, None, :]
        pr = masked_softmax(s, pos[None, :] <= tq[:, None])
        return jnp.einsum('bhqk,bhkd->bhqd', pr, v)

    o = merge_blocks(lax.map(blk, (split_blocks(q, 2, ATT_QBLOCK), split_blocks(cum, 2, ATT_QBLOCK),
                                   pos.reshape(-1, ATT_QBLOCK))), 2)
    return o.transpose(0, 2, 1, 3).reshape(bsz, t, H_FOX * HEAD_DIM)


def conv_ffn(h, w_up, conv_w, conv_b, w_down):
    u = causal_dwconv(h @ w_up, conv_w) + conv_b
    gt, up = jnp.split(u, 2, axis=-1)
    return (jax.nn.silu(gt) * up) @ w_down


def setup_inputs(seed: int = 0) -> dict:
    key = jax.random.key(seed)
    ks = iter(jax.random.split(key, 40))

    def nrm(shape, scale):
        return jax.random.normal(next(ks), shape, F32) * scale

    def gain(shape):
        return 1.0 + 0.05 * jax.random.normal(next(ks), shape, F32)

    dt = jnp.exp(jax.random.uniform(next(ks), (N_AB, H_GDN), F32, np.log(1e-3), np.log(1e-1)))
    return {
        'x': nrm((BATCH, SEQ, D_MODEL), 1.0),
        'p': nrm((DEPTH, BATCH, SEQ, D_PLE), 1.0),
        'ab_norm_pre': gain((N_AB, D_MODEL)),
        'ab_norm_post': gain((N_AB, D_MODEL)),
        'ab_w_in': nrm((N_AB, D_MODEL, AB_IN), D_MODEL ** -0.5),
        'gdn_conv_w': nrm((N_AB, GDN_CONV, 3 * H_GDN * HEAD_DIM), GDN_CONV ** -0.5),
        'gdn_a_log': jnp.log(jax.random.uniform(next(ks), (N_AB, H_GDN), F32, 1.0, 16.0)),
        'gdn_dt_bias': dt + jnp.log(-jnp.expm1(-dt)),
        'gdn_norm': gain((N_AB, HEAD_DIM)),
        'nsa_pe_k': nrm((N_AB, L_CMP, HEAD_DIM), 0.1),
        'nsa_pe_v': nrm((N_AB, L_CMP, HEAD_DIM), 0.1),
        'nsa_cmp_k1': nrm((N_AB, L_CMP * HEAD_DIM, HEAD_DIM), (L_CMP * HEAD_DIM) ** -0.5),
        'nsa_cmp_k2': nrm((N_AB, HEAD_DIM, HEAD_DIM), HEAD_DIM ** -0.5),
        'nsa_cmp_v1': nrm((N_AB, L_CMP * HEAD_DIM, HEAD_DIM), (L_CMP * HEAD_DIM) ** -0.5),
        'nsa_cmp_v2': nrm((N_AB, HEAD_DIM, HEAD_DIM), HEAD_DIM ** -0.5),
        'ab_w_out': nrm((N_AB, AB_OUT, D_MODEL), AB_OUT ** -0.5),
        'cd_norm_pre': gain((N_CD, D_MODEL)),
        'cd_norm_post': gain((N_CD, D_MODEL)),
        'cd_w_in': nrm((N_CD, D_MODEL, CD_IN), D_MODEL ** -0.5),
        'hgrn_lb_logits': nrm((DEPTH, H_HGRN * HEAD_DIM), 1.0),
        'hgrn_norm': gain((N_CD, HEAD_DIM)),
        'fox_f_bias': 2.0 + nrm((N_CD, H_FOX), 0.5),
        'cd_w_out': nrm((N_CD, CD_OUT, D_MODEL), CD_OUT ** -0.5),
        'ffn_norm_pre': gain((DEPTH, D_MODEL)),
        'ffn_norm_post': gain((DEPTH, D_MODEL)),
        'ffn_w_up': nrm((DEPTH, D_MODEL, 2 * D_FF), D_MODEL ** -0.5),
        'ffn_conv_w': nrm((DEPTH, FFN_CONV, 2 * D_FF), FFN_CONV ** -0.5),
        'ffn_conv_b': nrm((DEPTH, 2 * D_FF), 0.02),
        'ffn_w_down': nrm((DEPTH, D_FF, D_MODEL), D_FF ** -0.5),
        'ple_w_proj': nrm((DEPTH, D_PLE, D_MODEL), D_PLE ** -0.5),
        'ple_gate_norm': gain((DEPTH, D_MODEL)),
        'ple_w_gate': nrm((DEPTH, D_MODEL, D_MODEL), D_MODEL ** -0.5),
        'ple_norm_post': gain((DEPTH, D_MODEL)),
    }


def reference(x, p, ab_norm_pre, ab_norm_post, ab_w_in, gdn_conv_w, gdn_a_log, gdn_dt_bias,
              gdn_norm, nsa_pe_k, nsa_pe_v, nsa_cmp_k1, nsa_cmp_k2, nsa_cmp_v1, nsa_cmp_v2,
              ab_w_out, cd_norm_pre, cd_norm_post, cd_w_in, hgrn_lb_logits, hgrn_norm,
              fox_f_bias, cd_w_out, ffn_norm_pre, ffn_norm_post, ffn_w_up, ffn_conv_w,
              ffn_conv_b, ffn_w_down, ple_w_proj, ple_gate_norm, ple_w_gate, ple_norm_post):
    sm = jax.nn.softmax(hgrn_lb_logits.astype(F32), axis=0)
    lb_table = jnp.cumsum(sm, axis=0) - sm[0]
    for li in range(DEPTH):
        j = li // 2
        if li % 2 == 0:
            h = rmsnorm(x, ab_norm_pre[j])
            g_qkv, g_a, g_b, g_gate, n_q, n_kv, n_gate = split_cols(h @ ab_w_in[j], AB_SPLITS)
            o_a = gated_deltanet(g_qkv, g_a, g_b, g_gate, gdn_conv_w[j], gdn_a_log[j],
                                 gdn_dt_bias[j], gdn_norm[j])
            o_b = nsa_attention(n_q, n_kv, n_gate, nsa_pe_k[j], nsa_pe_v[j], nsa_cmp_k1[j],
                                nsa_cmp_k2[j], nsa_cmp_v1[j], nsa_cmp_v2[j])
            y = jnp.concatenate([o_a, o_b], -1).astype(x.dtype) @ ab_w_out[j]
            x = x + rmsnorm(y, ab_norm_post[j])
        else:
            h = rmsnorm(x, cd_norm_pre[j])
            h_q, h_f, h_i, h_g, f_qkv, f_f = split_cols(h @ cd_w_in[j], CD_SPLITS)
            o_c = hgrn2(h_q, h_f, h_i, h_g, lb_table[li], hgrn_norm[j])
            o_d = forgetting_attention(f_qkv, f_f, fox_f_bias[j])
            y = jnp.concatenate([o_c, o_d], -1).astype(x.dtype) @ cd_w_out[j]
            x = x + rmsnorm(y, cd_norm_post[j])
        h = rmsnorm(x, ffn_norm_pre[li])
        x = x + rmsnorm(conv_ffn(h, ffn_w_up[li], ffn_conv_w[li], ffn_conv_b[li], ffn_w_down[li]),
                        ffn_norm_post[li])
        gate = jax.nn.sigmoid(rmsnorm(x, ple_gate_norm[li]) @ ple_w_gate[li])
        x = x + rmsnorm(gate * (p[li].astype(x.dtype) @ ple_w_proj[li]), ple_norm_post[li])
    return x
```

```python
import functools

import numpy as np
import jax
import jax.numpy as jnp
from jax import lax
from jax.experimental import pallas as pl
from jax.experimental.pallas import tpu as pltpu

F32 = jnp.float32
BF16 = jnp.bfloat16
HI = lax.Precision.HIGHEST
NT = (((1,), (1,)), ((), ()))
TN = (((0,), (0,)), ((), ()))

D_MODEL = 4096
HEAD_DIM = 128
N_HEADS = 16
G_NSA = 4
HPG_NSA = 4
L_CMP = 32
S_CMP = 16
L_SLC = 64
N_SEL = 8
WINDOW = 512
FORCE_SCORE = 1e4
GDN_CHUNK = 64
GDN_GROUP = 256
HGRN_CHUNK = 256
D_FF = 11008
EPS = 1e-6
NEG = -1e30
LANES = 128
SMALL_W = 512

AB_QKV, AB_GATE, AB_NQ, AB_NKV, AB_SMALL = 0, 48, 64, 80, 104
AB_N = 108 * LANES
CD_Q, CD_F, CD_I, CD_G, CD_FQKV, CD_SMALL = 0, 16, 32, 48, 64, 112
CD_N = 116 * LANES

VMEM_LIMIT = 56 * 1024 * 1024


def _cparams(sem):
    return pltpu.CompilerParams(dimension_semantics=sem, vmem_limit_bytes=VMEM_LIMIT)


def _rms(x, w):
    return x * lax.rsqrt(jnp.mean(x * x, axis=-1, keepdims=True) + EPS) * w


def _silu(x):
    return x * jax.nn.sigmoid(x)


def _lane_col(x, lane_idx, lane):
    return jnp.sum(jnp.where(lane_idx == lane, x, 0.0), axis=-1, keepdims=True)


def _norm_matmul_body(x_ref, nw_ref, w_ref, o_ref, h_ref):
    @pl.when(pl.program_id(1) == 0)
    def _():
        h_ref[...] = _rms(x_ref[...], nw_ref[...]).astype(BF16)

    o_ref[...] = jnp.dot(h_ref[...], w_ref[...], preferred_element_type=F32)


def norm_matmul(x, nw, w, tm=512, tn=512):
    m, k = x.shape
    n = w.shape[1]
    return pl.pallas_call(
        _norm_matmul_body,
        grid=(m // tm, n // tn),
        in_specs=[pl.BlockSpec((tm, k), lambda i, j: (i, 0)),
                  pl.BlockSpec((1, k), lambda i, j: (0, 0)),
                  pl.BlockSpec((k, tn), lambda i, j: (0, j))],
        out_specs=pl.BlockSpec((tm, tn), lambda i, j: (i, j)),
        out_shape=jax.ShapeDtypeStruct((m, n), F32),
        scratch_shapes=[pltpu.VMEM((tm, k), BF16)],
        compiler_params=_cparams(("parallel", "arbitrary")),
        name="norm_matmul",
    )(x, nw.reshape(1, k), w)


def _outproj_body(oa_ref, ob_ref, wa_ref, wb_ref, x_ref, nw_ref, o_ref, *, tn):
    j = pl.program_id(1)
    y = jnp.dot(oa_ref[...], wa_ref[...], preferred_element_type=F32)
    y = y + jnp.dot(ob_ref[...], wb_ref[...], preferred_element_type=F32)
    o_ref[:, pl.ds(pl.multiple_of(j * tn, tn), tn)] = y

    @pl.when(j == pl.num_programs(1) - 1)
    def _():
        o_ref[...] = x_ref[...] + _rms(o_ref[...], nw_ref[...])


def outproj(oa, ob, wa, wb, x, nw, tm=256, tn=512):
    m, ka = oa.shape
    n = wa.shape[1]
    return pl.pallas_call(
        functools.partial(_outproj_body, tn=tn),
        grid=(m // tm, n // tn),
        in_specs=[pl.BlockSpec((tm, ka), lambda i, j: (i, 0)),
                  pl.BlockSpec((tm, ka), lambda i, j: (i, 0)),
                  pl.BlockSpec((ka, tn), lambda i, j: (0, j)),
                  pl.BlockSpec((ka, tn), lambda i, j: (0, j)),
                  pl.BlockSpec((tm, n), lambda i, j: (i, 0)),
                  pl.BlockSpec((1, n), lambda i, j: (0, 0))],
        out_specs=pl.BlockSpec((tm, n), lambda i, j: (i, 0)),
        out_shape=jax.ShapeDtypeStruct((m, n), F32),
        compiler_params=_cparams(("parallel", "arbitrary")),
        name="outproj",
    )(oa, ob, wa, wb, x, nw.reshape(1, n))


FFN_HALO = 16


def _ffn_body(x_ref, xh_ref, nw_ref, wg_ref, wu_ref, cwg_ref, cwu_ref, cbg_ref, cbu_ref,
              wd_ref, pw_ref, o_ref, h_s, *, tm, seq):
    i = pl.program_id(0)
    j = pl.program_id(1)

    @pl.when(j == 0)
    def _():
        keep = jnp.where((i * tm) % seq == 0, 0.0, 1.0)
        h_s[0:FFN_HALO, :] = (_rms(xh_ref[...], nw_ref[...]) * keep).astype(BF16)
        h_s[FFN_HALO:, :] = _rms(x_ref[...], nw_ref[...]).astype(BF16)
        o_ref[...] = jnp.zeros_like(o_ref)

    h = h_s[...]

    def branch(w_ref, cw_ref, cb_ref):
        u = jnp.dot(h, w_ref[...], preferred_element_type=F32)
        cw = cw_ref[...]
        y = u * cw[2:3] + pltpu.roll(u, 1, 0) * cw[1:2] + pltpu.roll(u, 2, 0) * cw[0:1]
        return y[FFN_HALO:] + cb_ref[...]

    g = branch(wg_ref, cwg_ref, cbg_ref)
    u = branch(wu_ref, cwu_ref, cbu_ref)
    act = (_silu(g) * u).astype(BF16)
    o_ref[...] += jnp.dot(act, wd_ref[...], preferred_element_type=F32)

    @pl.when(j == pl.num_programs(1) - 1)
    def _():
        o_ref[...] = x_ref[...] + _rms(o_ref[...], pw_ref[...])


def conv_ffn(x, nw, w_up, conv_w, conv_b, w_down, pw, seq, tm=512, tf=256):
    m, k = x.shape
    f = w_down.shape[0]
    nf = f // tf
    hb = tm // FFN_HALO
    cb = conv_b.reshape(1, 2 * f)
    return pl.pallas_call(
        functools.partial(_ffn_body, tm=tm, seq=seq),
        grid=(m // tm, nf),
        in_specs=[pl.BlockSpec((tm, k), lambda i, j: (i, 0), pipeline_mode=pl.Buffered(1)),
                  pl.BlockSpec((FFN_HALO, k), lambda i, j: (jnp.maximum(i * hb - 1, 0), 0)),
                  pl.BlockSpec((1, k), lambda i, j: (0, 0)),
                  pl.BlockSpec((k, tf), lambda i, j: (0, j)),
                  pl.BlockSpec((k, tf), lambda i, j: (0, nf + j)),
                  pl.BlockSpec((3, tf), lambda i, j: (0, j)),
                  pl.BlockSpec((3, tf), lambda i, j: (0, nf + j)),
                  pl.BlockSpec((1, tf), lambda i, j: (0, j)),
                  pl.BlockSpec((1, tf), lambda i, j: (0, nf + j)),
                  pl.BlockSpec((tf, k), lambda i, j: (j, 0)),
                  pl.BlockSpec((1, k), lambda i, j: (0, 0))],
        out_specs=pl.BlockSpec((tm, k), lambda i, j: (i, 0)),
        out_shape=jax.ShapeDtypeStruct((m, k), F32),
        scratch_shapes=[pltpu.VMEM((tm + FFN_HALO, k), BF16)],
        compiler_params=_cparams(("parallel", "arbitrary")),
        name="conv_ffn",
    )(x, x, nw.reshape(1, k), w_up, w_up, conv_w, conv_w, cb, cb, w_down, pw.reshape(1, k))


def _ple_body(x_ref, p_ref, gnw_ref, wg_ref, wp_ref, pnw_ref, o_ref, h_s, *, tn):
    j = pl.program_id(1)

    @pl.when(j == 0)
    def _():
        h_s[...] = _rms(x_ref[...], gnw_ref[...]).astype(BF16)

    gate = jax.nn.sigmoid(jnp.dot(h_s[...], wg_ref[...], preferred_element_type=F32))
    proj = jnp.dot(p_ref[...].astype(BF16), wp_ref[...], preferred_element_type=F32)
    o_ref[:, pl.ds(pl.multiple_of(j * tn, tn), tn)] = gate * proj

    @pl.when(j == pl.num_programs(1) - 1)
    def _():
        o_ref[...] = x_ref[...] + _rms(o_ref[...], pnw_ref[...])


def ple(x, p, gnw, wg, wp, pnw, tm=256, tn=512):
    m, k = x.shape
    dp = p.shape[1]
    return pl.pallas_call(
        functools.partial(_ple_body, tn=tn),
        grid=(m // tm, k // tn),
        in_specs=[pl.BlockSpec((tm, k), lambda i, j: (i, 0)),
                  pl.BlockSpec((tm, dp), lambda i, j: (i, 0)),
                  pl.BlockSpec((1, k), lambda i, j: (0, 0)),
                  pl.BlockSpec((k, tn), lambda i, j: (0, j)),
                  pl.BlockSpec((dp, tn), lambda i, j: (0, j)),
                  pl.BlockSpec((1, k), lambda i, j: (0, 0))],
        out_specs=pl.BlockSpec((tm, k), lambda i, j: (i, 0)),
        out_shape=jax.ShapeDtypeStruct((m, k), F32),
        scratch_shapes=[pltpu.VMEM((tm, k), BF16)],
        compiler_params=_cparams(("parallel", "arbitrary")),
        name="ple",
    )(x, p, gnw.reshape(1, k), wg, wp, pnw.reshape(1, k))


def _gdn_body(zq_ref, zk_ref, zv_ref, zg_ref, sm_ref, cwq_ref, cwk_ref, cwv_ref, prm_ref, nw_ref,
              o_ref, q_s, k_s, v_s, g_s, b_s, qe_s, o0_s, m_s, n0_s, *, seq):
    hd = pl.program_id(1)
    c = GDN_CHUNK
    gs = GDN_GROUP
    row = lax.broadcasted_iota(jnp.int32, (seq, LANES), 0)
    lane = lax.broadcasted_iota(jnp.int32, (seq, LANES), 1)

    def conv_silu(z_ref, w_ref):
        z = z_ref[0]
        w = w_ref[...]
        y = z * w[3:4]
        for s in (1, 2, 3):
            y = y + jnp.where(row >= s, pltpu.roll(z, s, 0), 0.0) * w[3 - s:4 - s]
        return _silu(y)

    q = conv_silu(zq_ref, cwq_ref)
    k = conv_silu(zk_ref, cwk_ref)
    q_s[...] = q * lax.rsqrt(jnp.sum(q * q, -1, keepdims=True) + EPS) * (HEAD_DIM ** -0.5)
    k_s[...] = k * lax.rsqrt(jnp.sum(k * k, -1, keepdims=True) + EPS)
    v_s[...] = conv_silu(zv_ref, cwv_ref)

    sm = sm_ref[0]
    glog = -jnp.exp(prm_ref[0:1, :]) * jax.nn.softplus(sm + prm_ref[1:2, :])
    g_s[...] = jnp.broadcast_to(_lane_col(glog, lane, hd), (seq, LANES))
    b_s[...] = jnp.broadcast_to(_lane_col(jax.nn.sigmoid(sm), lane, N_HEADS + hd), (seq, LANES))

    r = lax.broadcasted_iota(jnp.int32, (gs, gs), 0)
    cc = lax.broadcasted_iota(jnp.int32, (gs, gs), 1)
    same = (r // c) == (cc // c)
    incl = same & (r >= cc)
    strict = same & (r > cc)
    ltri = incl.astype(F32)
    eye = (r == cc).astype(F32)
    r1 = lax.broadcasted_iota(jnp.int32, (LANES, LANES), 0)
    c1 = lax.broadcasted_iota(jnp.int32, (LANES, LANES), 1)
    eye_d = (r1 == c1).astype(F32)
    pick0 = (lax.broadcasted_iota(jnp.int32, (gs, LANES), 1) == 0).astype(F32)

    def group(gi, carry):
        off = pl.multiple_of(gi * gs, gs)
        qg = q_s[pl.ds(off, gs), :]
        kg = k_s[pl.ds(off, gs), :]
        vg = v_s[pl.ds(off, gs), :]
        bg = b_s[pl.ds(off, gs), :]
        gam = jnp.dot(ltri, g_s[pl.ds(off, gs), :], precision=HI, preferred_element_type=F32)
        gam_row = lax.dot_general(pick0, gam, NT, precision=HI, preferred_element_type=F32)
        gam_col = jnp.concatenate([gam, gam], axis=1)
        dec = jnp.where(incl, jnp.exp(jnp.where(incl, gam_col - gam_row, 0.0)), 0.0)
        kk = lax.dot_general(kg, kg, NT, precision=HI, preferred_element_type=F32)
        qk = dec * lax.dot_general(qg, kg, NT, precision=HI, preferred_element_type=F32)
        a = jnp.where(strict, jnp.concatenate([bg, bg], axis=1) * dec * kk, 0.0)
        tinv = eye - a
        x = a
        for _ in range(5):
            x = jnp.dot(x, x, precision=HI, preferred_element_type=F32)
            tinv = tinv + jnp.dot(tinv, x, precision=HI, preferred_element_type=F32)
        eg = jnp.exp(gam)
        rhs = jnp.concatenate([vg * bg, kg * (bg * eg)], axis=1)
        sol = jnp.dot(tinv, rhs, precision=HI, preferred_element_type=F32)
        qkuw = jnp.dot(qk, sol, precision=HI, preferred_element_type=F32)
        o0_s[pl.ds(off, gs), :] = qkuw[:, :HEAD_DIM]
        qe_s[pl.ds(off, gs), :] = qg * eg - qkuw[:, HEAD_DIM:]
        for ci in range(gs // c):
            lo = ci * c
            gl = gam[lo + c - 1:lo + c, :]
            kd = kg[lo:lo + c] * jnp.exp(gl - gam[lo:lo + c])
            kds = lax.dot_general(kd, sol[lo:lo + c], TN, precision=HI, preferred_element_type=F32)
            so = pl.multiple_of((gi * (gs // c) + ci) * HEAD_DIM, HEAD_DIM)
            n0_s[pl.ds(so, HEAD_DIM), :] = kds[:, :HEAD_DIM]
            m_s[pl.ds(so, HEAD_DIM), :] = eye_d * jnp.exp(gl) - kds[:, HEAD_DIM:]
        return carry

    lax.fori_loop(0, seq // gs, group, 0)

    def step(n, s):
        off = pl.multiple_of(n * c, c)
        so = pl.multiple_of(n * HEAD_DIM, HEAD_DIM)
        o0_s[pl.ds(off, c), :] = o0_s[pl.ds(off, c), :] + jnp.dot(
            qe_s[pl.ds(off, c), :], s, precision=HI, preferred_element_type=F32)
        return n0_s[pl.ds(so, HEAD_DIM), :] + jnp.dot(
            m_s[pl.ds(so, HEAD_DIM), :], s, precision=HI, preferred_element_type=F32)

    lax.fori_loop(0, seq // c, step, jnp.zeros((HEAD_DIM, HEAD_DIM), F32))
    o_ref[0] = (_rms(o0_s[...], nw_ref[...]) * _silu(zg_ref[0])).astype(BF16)


def gated_deltanet(z3, conv_w, prm, norm_w):
    bsz, seq, _ = z3.shape
    zspec = lambda cb: pl.BlockSpec((1, seq, LANES), lambda b, h: (b, 0, cb + h))
    wspec = lambda cb: pl.BlockSpec((4, LANES), lambda b, h: (0, cb + h))
    nchunk = seq // GDN_CHUNK
    return pl.pallas_call(
        functools.partial(_gdn_body, seq=seq),
        grid=(bsz, N_HEADS),
        in_specs=[zspec(AB_QKV), zspec(AB_QKV + 16), zspec(AB_QKV + 32), zspec(AB_GATE),
                  pl.BlockSpec((1, seq, LANES), lambda b, h: (b, 0, AB_SMALL)),
                  wspec(0), wspec(16), wspec(32),
                  pl.BlockSpec((8, LANES), lambda b, h: (0, 0)),
                  pl.BlockSpec((1, LANES), lambda b, h: (0, 0))],
        out_specs=pl.BlockSpec((1, seq, LANES), lambda b, h: (b, 0, h)),
        out_shape=jax.ShapeDtypeStruct((bsz, seq, N_HEADS * HEAD_DIM), BF16),
        scratch_shapes=[pltpu.VMEM((seq, LANES), F32) for _ in range(7)]
        + [pltpu.VMEM((nchunk * HEAD_DIM, HEAD_DIM), F32) for _ in range(2)],
        compiler_params=_cparams(("parallel", "parallel")),
        name="gated_deltanet",
    )(z3, z3, z3, z3, z3, conv_w, conv_w, conv_w, prm, norm_w.reshape(1, HEAD_DIM))


def _hgrn_body(zq_ref, zf_ref, zi_ref, zg_ref, lb_ref, nw_ref, o_ref, *, seq):
    c = HGRN_CHUNK
    r = lax.broadcasted_iota(jnp.int32, (c, c), 0)
    cc = lax.broadcasted_iota(jnp.int32, (c, c), 1)
    ltri = (r >= cc).astype(F32)
    diag = r == cc
    rr = lax.broadcasted_iota(jnp.int32, (c, LANES), 0)
    lb = lb_ref[...]

    def chunk(n, st):
        off = pl.multiple_of(n * c, c)
        fx = zf_ref[0, pl.ds(off, c), :]
        q = _silu(zq_ref[0, pl.ds(off, c), :])
        v = zi_ref[0, pl.ds(off, c), :]
        lf = jnp.log(lb + (1.0 - lb) * jax.nn.sigmoid(fx))
        k = (1.0 - lb) * jax.nn.sigmoid(-fx)
        b = jnp.dot(ltri, lf, precision=HI, preferred_element_type=F32)
        att = jnp.where(diag, jnp.sum(q * k, -1, keepdims=True), 0.0)
        s = c // 2
        while s >= 1:
            isq = ((r // s) % 2) == 1
            wsel = ((r // s) == (cc // s)) & ((isq & (cc <= r)) | (~isq & (cc > r)))
            e = jnp.exp(jnp.dot(wsel.astype(F32), lf, precision=HI, preferred_element_type=F32))
            isq_r = ((rr // s) % 2) == 1
            qs = jnp.where(isq_r, q * e, 0.0)
            ks = jnp.where(isq_r, 0.0, k * e)
            part = lax.dot_general(qs, ks, NT, precision=HI, preferred_element_type=F32)
            att = att + jnp.where((r // (2 * s)) == (cc // (2 * s)), part, 0.0)
            s //= 2
        bl = b[c - 1:c, :]
        o = lax.dot_general(q * jnp.exp(b), st, NT, precision=HI, preferred_element_type=F32)
        o = o + jnp.dot(att, v, precision=HI, preferred_element_type=F32)
        o = _rms(o, nw_ref[...]) * _silu(zg_ref[0, pl.ds(off, c), :])
        o_ref[0, pl.ds(off, c), :] = o.astype(BF16)
        kd = k * jnp.exp(bl - b)
        return st * jnp.exp(bl) + lax.dot_general(v, kd, TN, precision=HI, preferred_element_type=F32)

    lax.fori_loop(0, seq // c, chunk, jnp.zeros((HEAD_DIM, HEAD_DIM), F32))


def hgrn2(z3, lb, norm_w):
    bsz, seq, _ = z3.shape
    zspec = lambda cb: pl.BlockSpec((1, seq, LANES), lambda b, h: (b, 0, cb + h))
    return pl.pallas_call(
        functools.partial(_hgrn_body, seq=seq),
        grid=(bsz, N_HEADS),
        in_specs=[zspec(CD_Q), zspec(CD_F), zspec(CD_I), zspec(CD_G),
                  pl.BlockSpec((1, LANES), lambda b, h: (0, h)),
                  pl.BlockSpec((1, LANES), lambda b, h: (0, 0))],
        out_specs=pl.BlockSpec((1, seq, LANES), lambda b, h: (b, 0, h)),
        out_shape=jax.ShapeDtypeStruct((bsz, seq, N_HEADS * HEAD_DIM), BF16),
        compiler_params=_cparams(("parallel", "parallel")),
        name="hgrn2",
    )(z3, z3, z3, z3, lb.reshape(1, N_HEADS * HEAD_DIM), norm_w.reshape(1, HEAD_DIM))


FOX_BLK = 256


def _fox_gate_body(sm_ref, bias_ref, cum_ref, cumt_ref, *, seq):
    c = FOX_BLK
    r = lax.broadcasted_iota(jnp.int32, (c, c), 0)
    cc = lax.broadcasted_iota(jnp.int32, (c, c), 1)
    ltri = (r >= cc).astype(F32)
    r1 = lax.broadcasted_iota(jnp.int32, (LANES, LANES), 0)
    c1 = lax.broadcasted_iota(jnp.int32, (LANES, LANES), 1)
    eye = (r1 == c1).astype(F32)

    def blk(n, carry):
        off = pl.multiple_of(n * c, c)
        ls = jax.nn.log_sigmoid(sm_ref[0, pl.ds(off, c), :] + bias_ref[...])
        cum = carry + jnp.dot(ltri, ls, precision=HI, preferred_element_type=F32)
        cum_ref[0, pl.ds(off, c), :] = cum
        cumt_ref[0, :, pl.ds(off, c)] = lax.dot_general(eye, cum, NT, precision=HI, preferred_element_type=F32)
        return cum[c - 1:c, :]

    lax.fori_loop(0, seq // c, blk, jnp.zeros((1, LANES), F32))


def fox_gates(z3, bias_row):
    bsz, seq, _ = z3.shape
    return pl.pallas_call(
        functools.partial(_fox_gate_body, seq=seq),
        grid=(bsz,),
        in_specs=[pl.BlockSpec((1, seq, LANES), lambda b: (b, 0, CD_SMALL)),
                  pl.BlockSpec((1, LANES), lambda b: (0, 0))],
        out_specs=[pl.BlockSpec((1, seq, LANES), lambda b: (b, 0, 0)),
                   pl.BlockSpec((1, LANES, seq), lambda b: (b, 0, 0))],
        out_shape=[jax.ShapeDtypeStruct((bsz, seq, LANES), F32),
                   jax.ShapeDtypeStruct((bsz, LANES, seq), F32)],
        compiler_params=_cparams(("parallel",)),
        name="fox_gates",
    )(z3, bias_row)


def _fox_attn_body(q_ref, k_ref, v_ref, cum_ref, cumt_ref, o_ref, *, tq):
    hd = pl.program_id(1)
    qi = pl.program_id(2)
    tk = tq
    q = (q_ref[0] * (HEAD_DIM ** -0.5)).astype(BF16)
    lane = lax.broadcasted_iota(jnp.int32, (tq, LANES), 1)
    cq = _lane_col(cum_ref[0], lane, hd)
    qpos = qi * tq + lax.broadcasted_iota(jnp.int32, (tq, tk), 0)
    kofs = lax.broadcasted_iota(jnp.int32, (tq, tk), 1)

    def kv(j, carry):
        m, l, acc = carry
        off = pl.multiple_of(j * tk, tk)
        kj = k_ref[0, pl.ds(off, tk), :].astype(BF16)
        vj = v_ref[0, pl.ds(off, tk), :].astype(BF16)
        ck = cumt_ref[0, pl.ds(hd % 8, 1), pl.ds(off, tk)]
        s = lax.dot_general(q, kj, NT, preferred_element_type=F32) + (cq - ck)
        ok = (off + kofs) <= qpos
        s = jnp.where(ok, s, NEG)
        m_new = jnp.maximum(m, jnp.max(s, -1, keepdims=True))
        p = jnp.where(ok, jnp.exp(s - m_new), 0.0)
        alpha = jnp.exp(m - m_new)
        l = alpha * l + jnp.sum(p, -1, keepdims=True)
        acc = alpha * acc + jnp.dot(p.astype(BF16), vj, preferred_element_type=F32)
        return m_new, l, acc

    init = (jnp.full((tq, 1), NEG, F32), jnp.zeros((tq, 1), F32), jnp.zeros((tq, HEAD_DIM), F32))
    _, l, acc = lax.fori_loop(0, qi + 1, kv, init)
    o_ref[0] = (acc / jnp.where(l > 0, l, 1.0)).astype(BF16)


def fox_attention(z3, cum, cumt, tq=FOX_BLK):
    bsz, seq, _ = z3.shape
    full = lambda cb: pl.BlockSpec((1, seq, LANES), lambda b, h, i: (b, 0, cb + h))
    return pl.pallas_call(
        functools.partial(_fox_attn_body, tq=tq),
        grid=(bsz, N_HEADS, seq // tq),
        in_specs=[pl.BlockSpec((1, tq, LANES), lambda b, h, i: (b, i, CD_FQKV + h)),
                  full(CD_FQKV + 16), full(CD_FQKV + 32),
                  pl.BlockSpec((1, tq, LANES), lambda b, h, i: (b, i, 0)),
                  pl.BlockSpec((1, 8, seq), lambda b, h, i: (b, h // 8, 0))],
        out_specs=pl.BlockSpec((1, tq, LANES), lambda b, h, i: (b, i, h)),
        out_shape=jax.ShapeDtypeStruct((bsz, seq, N_HEADS * HEAD_DIM), BF16),
        compiler_params=_cparams(("parallel", "parallel", "arbitrary")),
        name="fox_attention",
    )(z3, z3, z3, cum, cumt)


def _nsa_cmp_body(zk_ref, zv_ref, pek_ref, pev_ref, wk1_ref, wk2_ref, wv1_ref, wv2_ref, kc_ref, vc_ref, *, nblk):
    def compress(z_ref, pe_ref, w1_ref, w2_ref):
        u1 = jnp.zeros((nblk, HEAD_DIM), F32)
        u2 = jnp.zeros((nblk, HEAD_DIM), F32)
        for l in range(S_CMP):
            zl = z_ref[0, :, l, :]
            u1 = u1 + jnp.dot(zl, w1_ref[l * HEAD_DIM:(l + 1) * HEAD_DIM, :], precision=HI,
                              preferred_element_type=F32)
            u2 = u2 + jnp.dot(zl, w1_ref[(S_CMP + l) * HEAD_DIM:(S_CMP + l + 1) * HEAD_DIM, :], precision=HI,
                              preferred_element_type=F32)
        pe = jnp.dot(jnp.broadcast_to(pe_ref[...], (8, L_CMP * HEAD_DIM)), w1_ref[...], precision=HI,
                     preferred_element_type=F32)[0:1]
        hmid = u1 + pltpu.roll(u2, nblk - 1, 0) + pe
        return jnp.dot(_silu(hmid), w2_ref[...], precision=HI, preferred_element_type=F32)

    kc_ref[0, 0] = compress(zk_ref, pek_ref, wk1_ref, wk2_ref)
    vc_ref[0, 0] = compress(zv_ref, pev_ref, wv1_ref, wv2_ref)


def nsa_compress(z3, pe_k, pe_v, wk1, wk2, wv1, wv2):
    bsz, seq, n = z3.shape
    nblk = seq // S_CMP
    z4 = z3.reshape(bsz, nblk, S_CMP, n)
    zspec = lambda cb: pl.BlockSpec((1, nblk, S_CMP, LANES), lambda b, g: (b, 0, 0, cb + g))
    wfull = lambda a: pl.BlockSpec(a.shape, lambda b, g: (0, 0))
    pek = pe_k.reshape(1, L_CMP * HEAD_DIM)
    pev = pe_v.reshape(1, L_CMP * HEAD_DIM)
    out = jax.ShapeDtypeStruct((bsz, G_NSA, nblk, HEAD_DIM), F32)
    ospec = pl.BlockSpec((1, 1, nblk, HEAD_DIM), lambda b, g: (b, g, 0, 0))
    return pl.pallas_call(
        functools.partial(_nsa_cmp_body, nblk=nblk),
        grid=(bsz, G_NSA),
        in_specs=[zspec(AB_NKV), zspec(AB_NKV + G_NSA), wfull(pek), wfull(pev),
                  wfull(wk1), wfull(wk2), wfull(wv1), wfull(wv2)],
        out_specs=[ospec, ospec],
        out_shape=[out, out],
        compiler_params=_cparams(("parallel", "parallel")),
        name="nsa_compress",
    )(z4, z4, pek, pev, wk1, wk2, wv1, wv2)


NSA_TQ = 128


def _nsa_attn_body(q_ref, kc_ref, vc_ref, ks_ref, vs_ref, kw_ref, vw_ref, sm_ref, o_ref, *, seq):
    g = pl.program_id(1)
    qi = pl.program_id(2)
    tq = NSA_TQ
    tk = NSA_TQ
    rows = HPG_NSA * tq
    nblk = seq // S_CMP
    n_slc = seq // L_SLC
    n_top = min(N_SEL, n_slc)
    q0 = qi * tq

    qall = q_ref[0]
    qs = jnp.concatenate([qall[:, p * HEAD_DIM:(p + 1) * HEAD_DIM] for p in range(HPG_NSA)], axis=0)
    qs = qs * (HEAD_DIM ** -0.5)
    qb = qs.astype(BF16)
    tpos = q0 + lax.broadcasted_iota(jnp.int32, (rows, 1), 0) % tq

    kc = kc_ref[0, 0]
    vc = vc_ref[0, 0]
    sc = lax.dot_general(qs, kc, NT, precision=HI, preferred_element_type=F32)
    nidx = lax.broadcasted_iota(jnp.int32, (rows, nblk), 1)
    cmask = (nidx * S_CMP + (L_CMP - 1) <= tpos) & (nidx <= nblk - 2)
    scm = jnp.where(cmask, sc, NEG)
    mc = jnp.max(scm, -1, keepdims=True)
    ec = jnp.where(cmask, jnp.exp(scm - mc), 0.0)
    dc = jnp.sum(ec, -1, keepdims=True)
    p_cmp = ec / jnp.where(dc > 0, dc, 1.0)
    o_cmp = jnp.dot(p_cmp, vc, precision=HI, preferred_element_type=F32)

    psum = p_cmp[0:tq]
    for p in range(1, HPG_NSA):
        psum = psum + p_cmp[p * tq:(p + 1) * tq]
    cs = lax.broadcasted_iota(jnp.int32, (nblk, LANES), 0) * S_CMP
    ss = lax.broadcasted_iota(jnp.int32, (nblk, LANES), 1) * L_SLC
    overlap = ((cs < ss + L_SLC) & (cs + L_CMP > ss) & (cs <= seq - L_CMP) & (ss < seq)).astype(F32)
    imp = jnp.dot(psum, overlap, precision=HI, preferred_element_type=F32)
    blk = lax.broadcasted_iota(jnp.int32, (tq, LANES), 1)
    cur = (q0 + lax.broadcasted_iota(jnp.int32, (tq, LANES), 0)) // L_SLC
    valid = blk <= cur
    forced = (blk == 0) | (blk == cur) | (blk == cur - 1)
    score = jnp.where(valid, jnp.where(forced, FORCE_SCORE, imp), NEG)
    rank = jnp.zeros((tq, LANES), jnp.int32)
    for mp in range(n_slc):
        colv = score[:, mp:mp + 1]
        rank = rank + ((colv > score) | ((colv == score) & (mp < blk))).astype(jnp.int32)
    self = (valid & (rank < n_top)).astype(F32)

    lane_k = lax.broadcasted_iota(jnp.int32, (rows, tk), 1)
    lane_q = lax.broadcasted_iota(jnp.int32, (tq, tk), 1)
    per_blk = tk // L_SLC

    def online(s, ok, vj, carry):
        m, l, acc = carry
        s = jnp.where(ok, s, NEG)
        m_new = jnp.maximum(m, jnp.max(s, -1, keepdims=True))
        p = jnp.where(ok, jnp.exp(s - m_new), 0.0)
        alpha = jnp.exp(m - m_new)
        l = alpha * l + jnp.sum(p, -1, keepdims=True)
        acc = alpha * acc + jnp.dot(p.astype(BF16), vj, preferred_element_type=F32)
        return m_new, l, acc

    def finish(carry):
        _, l, acc = carry
        return acc / jnp.where(l > 0, l, 1.0)

    init = (jnp.full((rows, 1), NEG, F32), jnp.zeros((rows, 1), F32), jnp.zeros((rows, HEAD_DIM), F32))

    def slc(j, carry):
        off = pl.multiple_of(j * tk, tk)
        kj = ks_ref[0, pl.ds(off, tk), :].astype(BF16)
        vj = vs_ref[0, pl.ds(off, tk), :].astype(BF16)
        s = lax.dot_general(qb, kj, NT, preferred_element_type=F32)
        selq = jnp.zeros((tq, tk), F32)
        for u in range(per_blk):
            colu = jnp.max(jnp.where(blk == j * per_blk + u, self, 0.0), -1, keepdims=True)
            selq = jnp.where((lane_q // L_SLC) == u, colu, selq)
        sel = jnp.concatenate([selq] * HPG_NSA, axis=0)
        ok = (sel > 0.0) & ((off + lane_k) <= tpos)
        return online(s, ok, vj, carry)

    o_slc = finish(lax.fori_loop(0, qi + 1, slc, init))

    def win(j, carry):
        off = pl.multiple_of(j * tk, tk)
        kj = kw_ref[0, pl.ds(off, tk), :].astype(BF16)
        vj = vw_ref[0, pl.ds(off, tk), :].astype(BF16)
        s = lax.dot_general(qb, kj, NT, preferred_element_type=F32)
        d = tpos - (off + lane_k)
        ok = (d >= 0) & (d < WINDOW)
        return online(s, ok, vj, carry)

    o_win = finish(lax.fori_loop(jnp.maximum(qi - WINDOW // tk, 0), qi + 1, win, init))

    sg = jax.nn.sigmoid(sm_ref[0])
    for p in range(HPG_NSA):
        sl = slice(p * tq, (p + 1) * tq)
        base = 2 * N_HEADS + g * HPG_NSA + p
        o = (_lane_col(sg, blk, base) * o_cmp[sl] + _lane_col(sg, blk, base + N_HEADS) * o_slc[sl]
             + _lane_col(sg, blk, base + 2 * N_HEADS) * o_win[sl])
        o_ref[0, :, p * HEAD_DIM:(p + 1) * HEAD_DIM] = o.astype(BF16)


def nsa_attention(z3, kc, vc):
    bsz, seq, _ = z3.shape
    tq = NSA_TQ
    nblk = seq // S_CMP
    full = lambda cb: pl.BlockSpec((1, seq, LANES), lambda b, g, i: (b, 0, cb + g))
    cspec = pl.BlockSpec((1, 1, nblk, HEAD_DIM), lambda b, g, i: (b, g, 0, 0))
    qw = HPG_NSA * HEAD_DIM
    return pl.pallas_call(
        functools.partial(_nsa_attn_body, seq=seq),
        grid=(bsz, G_NSA, seq // tq),
        in_specs=[pl.BlockSpec((1, tq, qw), lambda b, g, i: (b, i, AB_NQ * LANES // qw + g)),
                  cspec, cspec,
                  full(AB_NKV + 2 * G_NSA), full(AB_NKV + 3 * G_NSA),
                  full(AB_NKV + 4 * G_NSA), full(AB_NKV + 5 * G_NSA),
                  pl.BlockSpec((1, tq, LANES), lambda b, g, i: (b, i, AB_SMALL))],
        out_specs=pl.BlockSpec((1, tq, qw), lambda b, g, i: (b, i, g)),
        out_shape=jax.ShapeDtypeStruct((bsz, seq, N_HEADS * HEAD_DIM), BF16),
        compiler_params=_cparams(("parallel", "parallel", "arbitrary")),
        name="nsa_attention",
    )(z3, kc, vc, z3, z3, z3, z3, z3)


def _pad_cols(w, width):
    return jnp.pad(w, ((0, 0), (0, width - w.shape[1])))


def _ab_weight(w):
    small = _pad_cols(jnp.concatenate([w[:, 6144:6176], w[:, 13344:13392]], axis=1), SMALL_W)
    return jnp.concatenate([w[:, :6144], w[:, 6176:8224], w[:, 8224:10272], w[:, 10272:13344], small],
                           axis=1).astype(BF16)


def _cd_weight(w):
    return jnp.concatenate([w[:, :14336], _pad_cols(w[:, 14336:14352], SMALL_W)], axis=1).astype(BF16)


def _row128(v):
    return jnp.pad(v.astype(F32), (0, LANES - v.shape[0])).reshape(1, LANES)


def kernel(x, p, ab_norm_pre, ab_norm_post, ab_w_in, gdn_conv_w, gdn_a_log, gdn_dt_bias, gdn_norm, nsa_pe_k, nsa_pe_v, nsa_cmp_k1, nsa_cmp_k2, nsa_cmp_v1, nsa_cmp_v2, ab_w_out, cd_norm_pre, cd_norm_post, cd_w_in, hgrn_lb_logits, hgrn_norm, fox_f_bias, cd_w_out, ffn_norm_pre, ffn_norm_post, ffn_w_up, ffn_conv_w, ffn_conv_b, ffn_w_down, ple_w_proj, ple_gate_norm, ple_w_gate, ple_norm_post):
    bsz, seq, dm = x.shape
    depth = p.shape[0]
    m = bsz * seq
    half = N_HEADS * HEAD_DIM
    sm_ = jax.nn.softmax(hgrn_lb_logits.astype(F32), axis=0)
    lb_table = jnp.cumsum(sm_, axis=0) - sm_[0]
    xf = x.reshape(m, dm)
    for li in range(depth):
        j = li // 2
        if li % 2 == 0:
            z3 = norm_matmul(xf, ab_norm_pre[j], _ab_weight(ab_w_in[j])).reshape(bsz, seq, AB_N)
            prm = jnp.concatenate([_row128(gdn_a_log[j]), _row128(gdn_dt_bias[j]),
                                   jnp.zeros((6, LANES), F32)], axis=0)
            o_a = gated_deltanet(z3, gdn_conv_w[j], prm, gdn_norm[j])
            kc, vc = nsa_compress(z3, nsa_pe_k[j], nsa_pe_v[j], nsa_cmp_k1[j], nsa_cmp_k2[j],
                                  nsa_cmp_v1[j], nsa_cmp_v2[j])
            o_b = nsa_attention(z3, kc, vc)
            w_out, post = ab_w_out[j], ab_norm_post[j]
        else:
            z3 = norm_matmul(xf, cd_norm_pre[j], _cd_weight(cd_w_in[j])).reshape(bsz, seq, CD_N)
            o_a = hgrn2(z3, lb_table[li], hgrn_norm[j])
            cum, cumt = fox_gates(z3, _row128(fox_f_bias[j]))
            o_b = fox_attention(z3, cum, cumt)
            w_out, post = cd_w_out[j], cd_norm_post[j]
        w_out = w_out.astype(BF16)
        xf = outproj(o_a.reshape(m, half), o_b.reshape(m, half), w_out[:half], w_out[half:], xf, post)
        xf = conv_ffn(xf, ffn_norm_pre[li], ffn_w_up[li].astype(BF16), ffn_conv_w[li], ffn_conv_b[li],
                      ffn_w_down[li].astype(BF16), ffn_norm_post[li], seq)
        xf = ple(xf, p[li].reshape(m, -1), ple_gate_norm[li], ple_w_gate[li].astype(BF16),
                 ple_w_proj[li].astype(BF16), ple_norm_post[li])
    return xf.reshape(bsz, seq, dm)
```

```python
import functools

import numpy as np
import jax
import jax.numpy as jnp
from jax import lax
from jax.experimental import pallas as pl
from jax.experimental.pallas import tpu as pltpu

F32 = jnp.float32
BF16 = jnp.bfloat16
HI = lax.Precision.HIGHEST
NT = (((1,), (1,)), ((), ()))
TN = (((0,), (0,)), ((), ()))

D_MODEL = 4096
HEAD_DIM = 128
N_HEADS = 16
G_NSA = 4
HPG_NSA = 4
L_CMP = 32
S_CMP = 16
L_SLC = 64
N_SEL = 8
WINDOW = 512
FORCE_SCORE = 1e4
GDN_CHUNK = 64
GDN_GROUP = 256
HGRN_CHUNK = 256
D_FF = 11008
EPS = 1e-6
NEG = -1e30
LANES = 128
SMALL_W = 512

AB_QKV, AB_GATE, AB_NQ, AB_NKV, AB_SMALL = 0, 48, 64, 80, 104
AB_N = 108 * LANES
CD_Q, CD_F, CD_I, CD_G, CD_FQKV, CD_SMALL = 0, 16, 32, 48, 64, 112
CD_N = 116 * LANES

VMEM_LIMIT = 56 * 1024 * 1024


def _cparams(sem):
    return pltpu.CompilerParams(dimension_semantics=sem, vmem_limit_bytes=VMEM_LIMIT)


def _rms(x, w):
    return x * lax.rsqrt(jnp.mean(x * x, axis=-1, keepdims=True) + EPS) * w


def _silu(x):
    return x * jax.nn.sigmoid(x)


def _lane_col(x, lane_idx, lane):
    return jnp.sum(jnp.where(lane_idx == lane, x, 0.0), axis=-1, keepdims=True)


def _split2(x):
    hi = x.astype(BF16)
    return hi, (x - hi.astype(F32)).astype(BF16)


def _dot_sel(sel, x):
    n = x.shape[1]
    hi = x.astype(BF16)
    r1 = x - hi.astype(F32)
    mid = r1.astype(BF16)
    lo = (r1 - mid.astype(F32)).astype(BF16)
    y = jnp.dot(sel, jnp.concatenate([hi, mid, lo], axis=1), preferred_element_type=F32)
    return (y[:, :n] + y[:, n:2 * n]) + y[:, 2 * n:]


def _norm_matmul_body(x_ref, nw_ref, w_ref, o_ref, h_ref):
    @pl.when(pl.program_id(1) == 0)
    def _():
        h_ref[...] = _rms(x_ref[...], nw_ref[...]).astype(BF16)

    o_ref[...] = jnp.dot(h_ref[...], w_ref[...], preferred_element_type=F32)


def norm_matmul(x, nw, w, tm=512, tn=512):
    m, k = x.shape
    n = w.shape[1]
    return pl.pallas_call(
        _norm_matmul_body,
        grid=(m // tm, n // tn),
        in_specs=[pl.BlockSpec((tm, k), lambda i, j: (i, 0)),
                  pl.BlockSpec((1, k), lambda i, j: (0, 0)),
                  pl.BlockSpec((k, tn), lambda i, j: (0, j))],
        out_specs=pl.BlockSpec((tm, tn), lambda i, j: (i, j)),
        out_shape=jax.ShapeDtypeStruct((m, n), F32),
        scratch_shapes=[pltpu.VMEM((tm, k), BF16)],
        compiler_params=_cparams(("parallel", "arbitrary")),
        name="norm_matmul",
    )(x, nw.reshape(1, k), w)


def _outproj_body(oa_ref, ob_ref, wa_ref, wb_ref, x_ref, nw_ref, o_ref, *, tn):
    j = pl.program_id(1)
    y = jnp.dot(oa_ref[...], wa_ref[...], preferred_element_type=F32)
    y = y + jnp.dot(ob_ref[...], wb_ref[...], preferred_element_type=F32)
    o_ref[:, pl.ds(pl.multiple_of(j * tn, tn), tn)] = y

    @pl.when(j == pl.num_programs(1) - 1)
    def _():
        o_ref[...] = x_ref[...] + _rms(o_ref[...], nw_ref[...])


def outproj(oa, ob, wa, wb, x, nw, tm=256, tn=512):
    m, ka = oa.shape
    n = wa.shape[1]
    return pl.pallas_call(
        functools.partial(_outproj_body, tn=tn),
        grid=(m // tm, n // tn),
        in_specs=[pl.BlockSpec((tm, ka), lambda i, j: (i, 0)),
                  pl.BlockSpec((tm, ka), lambda i, j: (i, 0)),
                  pl.BlockSpec((ka, tn), lambda i, j: (0, j)),
                  pl.BlockSpec((ka, tn), lambda i, j: (0, j)),
                  pl.BlockSpec((tm, n), lambda i, j: (i, 0)),
                  pl.BlockSpec((1, n), lambda i, j: (0, 0))],
        out_specs=pl.BlockSpec((tm, n), lambda i, j: (i, 0)),
        out_shape=jax.ShapeDtypeStruct((m, n), F32),
        compiler_params=_cparams(("parallel", "arbitrary")),
        name="outproj",
    )(oa, ob, wa, wb, x, nw.reshape(1, n))


FFN_HALO = 16


def _ffn_body(x_ref, xh_ref, nw_ref, wg_ref, wu_ref, cwg_ref, cwu_ref, cbg_ref, cbu_ref,
              wd_ref, pw_ref, o_ref, h_s, *, tm, seq):
    i = pl.program_id(0)
    j = pl.program_id(1)

    @pl.when(j == 0)
    def _():
        keep = jnp.where((i * tm) % seq == 0, 0.0, 1.0)
        h_s[0:FFN_HALO, :] = (_rms(xh_ref[...], nw_ref[...]) * keep).astype(BF16)
        h_s[FFN_HALO:, :] = _rms(x_ref[...], nw_ref[...]).astype(BF16)
        o_ref[...] = jnp.zeros_like(o_ref)

    h = h_s[...]

    def branch(w_ref, cw_ref, cb_ref):
        u = jnp.dot(h, w_ref[...], preferred_element_type=F32)
        cw = cw_ref[...]
        y = u * cw[2:3] + pltpu.roll(u, 1, 0) * cw[1:2] + pltpu.roll(u, 2, 0) * cw[0:1]
        return y[FFN_HALO:] + cb_ref[...]

    g = branch(wg_ref, cwg_ref, cbg_ref)
    u = branch(wu_ref, cwu_ref, cbu_ref)
    act = (_silu(g) * u).astype(BF16)
    o_ref[...] += jnp.dot(act, wd_ref[...], preferred_element_type=F32)

    @pl.when(j == pl.num_programs(1) - 1)
    def _():
        o_ref[...] = x_ref[...] + _rms(o_ref[...], pw_ref[...])


def conv_ffn(x, nw, w_up, conv_w, conv_b, w_down, pw, seq, tm=512, tf=256):
    m, k = x.shape
    f = w_down.shape[0]
    nf = f // tf
    hb = tm // FFN_HALO
    cb = conv_b.reshape(1, 2 * f)
    return pl.pallas_call(
        functools.partial(_ffn_body, tm=tm, seq=seq),
        grid=(m // tm, nf),
        in_specs=[pl.BlockSpec((tm, k), lambda i, j: (i, 0), pipeline_mode=pl.Buffered(1)),
                  pl.BlockSpec((FFN_HALO, k), lambda i, j: (jnp.maximum(i * hb - 1, 0), 0)),
                  pl.BlockSpec((1, k), lambda i, j: (0, 0)),
                  pl.BlockSpec((k, tf), lambda i, j: (0, j)),
                  pl.BlockSpec((k, tf), lambda i, j: (0, nf + j)),
                  pl.BlockSpec((3, tf), lambda i, j: (0, j)),
                  pl.BlockSpec((3, tf), lambda i, j: (0, nf + j)),
                  pl.BlockSpec((1, tf), lambda i, j: (0, j)),
                  pl.BlockSpec((1, tf), lambda i, j: (0, nf + j)),
                  pl.BlockSpec((tf, k), lambda i, j: (j, 0)),
                  pl.BlockSpec((1, k), lambda i, j: (0, 0))],
        out_specs=pl.BlockSpec((tm, k), lambda i, j: (i, 0)),
        out_shape=jax.ShapeDtypeStruct((m, k), F32),
        scratch_shapes=[pltpu.VMEM((tm + FFN_HALO, k), BF16)],
        compiler_params=_cparams(("parallel", "arbitrary")),
        name="conv_ffn",
    )(x, x, nw.reshape(1, k), w_up, w_up, conv_w, conv_w, cb, cb, w_down, pw.reshape(1, k))


def _ple_body(x_ref, p_ref, gnw_ref, wg_ref, wp_ref, pnw_ref, o_ref, h_s, *, tn):
    j = pl.program_id(1)

    @pl.when(j == 0)
    def _():
        h_s[...] = _rms(x_ref[...], gnw_ref[...]).astype(BF16)

    gate = jax.nn.sigmoid(jnp.dot(h_s[...], wg_ref[...], preferred_element_type=F32))
    proj = jnp.dot(p_ref[...].astype(BF16), wp_ref[...], preferred_element_type=F32)
    o_ref[:, pl.ds(pl.multiple_of(j * tn, tn), tn)] = gate * proj

    @pl.when(j == pl.num_programs(1) - 1)
    def _():
        o_ref[...] = x_ref[...] + _rms(o_ref[...], pnw_ref[...])


def ple(x, p, gnw, wg, wp, pnw, tm=256, tn=512):
    m, k = x.shape
    dp = p.shape[1]
    return pl.pallas_call(
        functools.partial(_ple_body, tn=tn),
        grid=(m // tm, k // tn),
        in_specs=[pl.BlockSpec((tm, k), lambda i, j: (i, 0)),
                  pl.BlockSpec((tm, dp), lambda i, j: (i, 0)),
                  pl.BlockSpec((1, k), lambda i, j: (0, 0)),
                  pl.BlockSpec((k, tn), lambda i, j: (0, j)),
                  pl.BlockSpec((dp, tn), lambda i, j: (0, j)),
                  pl.BlockSpec((1, k), lambda i, j: (0, 0))],
        out_specs=pl.BlockSpec((tm, k), lambda i, j: (i, 0)),
        out_shape=jax.ShapeDtypeStruct((m, k), F32),
        scratch_shapes=[pltpu.VMEM((tm, k), BF16)],
        compiler_params=_cparams(("parallel", "arbitrary")),
        name="ple",
    )(x, p, gnw.reshape(1, k), wg, wp, pnw.reshape(1, k))


def _gdn_ltri():
    r = np.arange(GDN_GROUP)[:, None]
    t = np.arange(GDN_GROUP)[None, :]
    return jnp.asarray((((r // GDN_CHUNK) == (t // GDN_CHUNK)) & (r >= t)).astype(np.float32), dtype=BF16)


def _gdn_body(zq_ref, zk_ref, zv_ref, zg_ref, sm_ref, cwq_ref, cwk_ref, cwv_ref, prm_ref, nw_ref, ltri_ref,
              o_ref, q_s, k_s, v_s, g_s, b_s, o0_s, n0_s, gl_s, qe_s, m_s, *, seq):
    hd = pl.program_id(1)
    c = GDN_CHUNK
    gs = GDN_GROUP
    row = lax.broadcasted_iota(jnp.int32, (seq, LANES), 0)
    lane = lax.broadcasted_iota(jnp.int32, (seq, LANES), 1)

    def conv_silu(z_ref, w_ref):
        z = z_ref[0]
        w = w_ref[...]
        y = z * w[3:4]
        for s in (1, 2, 3):
            y = y + jnp.where(row >= s, pltpu.roll(z, s, 0), 0.0) * w[3 - s:4 - s]
        return _silu(y)

    q = conv_silu(zq_ref, cwq_ref)
    k = conv_silu(zk_ref, cwk_ref)
    q_s[...] = q * lax.rsqrt(jnp.sum(q * q, -1, keepdims=True) + EPS) * (HEAD_DIM ** -0.5)
    k_s[...] = k * lax.rsqrt(jnp.sum(k * k, -1, keepdims=True) + EPS)
    v_s[...] = conv_silu(zv_ref, cwv_ref)

    sm = sm_ref[0]
    glog = -jnp.exp(prm_ref[0:1, :]) * jax.nn.softplus(sm + prm_ref[1:2, :])
    g_s[...] = jnp.broadcast_to(_lane_col(glog, lane, hd), (seq, LANES))
    b_s[...] = jnp.broadcast_to(_lane_col(jax.nn.sigmoid(sm), lane, N_HEADS + hd), (seq, LANES))

    nper = gs // c
    rs = lax.broadcasted_iota(jnp.int32, (c, gs), 0)
    cs = lax.broadcasted_iota(jnp.int32, (c, gs), 1)
    jj = cs % c
    cblk = cs // c
    incl = rs >= jj
    strict = rs > jj
    eye_side = (rs == jj).astype(F32)
    r2 = lax.broadcasted_iota(jnp.int32, (gs, gs), 0)
    c2 = lax.broadcasted_iota(jnp.int32, (gs, gs), 1)
    bdm = ((r2 // c) == (c2 // c)).astype(F32)
    bdm_b = bdm.astype(BF16)
    lane_c = lax.broadcasted_iota(jnp.int32, (c, LANES), 1)
    lane_g = lax.broadcasted_iota(jnp.int32, (gs, LANES), 1)
    ones3 = (lane_c < 3).astype(F32).astype(BF16)

    def side_col(xb):
        left = jnp.where(lane_c < c, xb[0:c], xb[c:2 * c])
        right = jnp.where(lane_c < c, xb[2 * c:3 * c], xb[3 * c:4 * c])
        return jnp.concatenate([left, right], axis=1)

    def to_side(full):
        out = full[0:c]
        for ci in range(1, nper):
            out = jnp.where(cblk == ci, full[ci * c:(ci + 1) * c], out)
        return out

    def bd(xs, mask):
        return jnp.concatenate([xs] * nper, axis=0) * mask

    def mm3(ph, pl_, xh, xl):
        bh = bd(xh, bdm_b)
        lhs = jnp.concatenate([ph, ph, pl_], axis=1)
        rhs = jnp.concatenate([bh, bd(xl, bdm_b), bh], axis=0)
        return jnp.dot(lhs, rhs, preferred_element_type=F32)

    def group(gi, carry):
        off = pl.multiple_of(gi * gs, gs)
        qg = q_s[pl.ds(off, gs), :]
        kg = k_s[pl.ds(off, gs), :]
        vg = v_s[pl.ds(off, gs), :]
        bg = b_s[pl.ds(off, gs), :]
        gam = _dot_sel(ltri_ref[...], g_s[pl.ds(off, gs), :])
        g_hi = gam.astype(BF16).astype(F32)
        g_mid = (gam - g_hi).astype(BF16).astype(F32)
        g_lo = gam - g_hi - g_mid
        gam3 = jnp.where(lane_g == 0, g_hi, jnp.where(lane_g == 1, g_mid, jnp.where(lane_g == 2, g_lo, 0.0)))
        gam_row = lax.dot_general(ones3, gam3.astype(BF16), NT, preferred_element_type=F32)
        dec = jnp.where(incl, jnp.exp(jnp.where(incl, side_col(gam) - gam_row, 0.0)), 0.0)
        qb = qg.astype(BF16)
        kb = kg.astype(BF16)
        kk = to_side(lax.dot_general(kb, kb, NT, preferred_element_type=F32))
        a = jnp.where(strict, side_col(bg) * dec * kk, 0.0)
        tinv = eye_side - a
        xh, xl = _split2(a)
        for _ in range(5):
            xh, xl = _split2(mm3(xh, xl, xh, xl))
            th, tl = _split2(tinv)
            tinv = tinv + mm3(th, tl, xh, xl)
        th, tl = _split2(tinv)
        eg = jnp.exp(gam)
        rh, rl = _split2(jnp.concatenate([vg * bg, kg * (bg * eg)], axis=1))
        bth = bd(th, bdm_b)
        sol = jnp.dot(jnp.concatenate([bth, bth, bd(tl, bdm_b)], axis=1),
                      jnp.concatenate([rh, rl, rh], axis=0), preferred_element_type=F32)
        solb = sol.astype(BF16)
        qk = bd(dec, bdm) * lax.dot_general(qb, kb, NT, preferred_element_type=F32)
        qkuw = jnp.dot(qk.astype(BF16), solb, preferred_element_type=F32)
        o0_s[pl.ds(off, gs), :] = qkuw[:, :HEAD_DIM]
        qe_s[pl.ds(off, gs), :] = (qg * eg - qkuw[:, HEAD_DIM:]).astype(BF16)
        for ci in range(nper):
            lo = ci * c
            gl = gam[lo + c - 1:lo + c, :]
            kd = (kg[lo:lo + c] * jnp.exp(gl - gam[lo:lo + c])).astype(BF16)
            kds = lax.dot_general(kd, solb[lo:lo + c], TN, preferred_element_type=F32)
            so = pl.multiple_of((gi * nper + ci) * HEAD_DIM, HEAD_DIM)
            n0_s[pl.ds(so, HEAD_DIM), :] = kds[:, :HEAD_DIM]
            m_s[pl.ds(so, HEAD_DIM), :] = (-kds[:, HEAD_DIM:]).astype(BF16)
            gl_s[pl.ds(pl.multiple_of((gi * nper + ci) * 8, 8), 8), :] = jnp.broadcast_to(jnp.exp(gl), (8, LANES))
        return carry

    lax.fori_loop(0, seq // gs, group, 0)

    def step(n, s):
        off = pl.multiple_of(n * c, c)
        so = pl.multiple_of(n * HEAD_DIM, HEAD_DIM)
        sb = s.astype(BF16)
        o0_s[pl.ds(off, c), :] = o0_s[pl.ds(off, c), :] + jnp.dot(
            qe_s[pl.ds(off, c), :], sb, preferred_element_type=F32)
        return (gl_s[pl.ds(pl.multiple_of(n * 8, 8), 1), :] * s + n0_s[pl.ds(so, HEAD_DIM), :]
                + jnp.dot(m_s[pl.ds(so, HEAD_DIM), :], sb, preferred_element_type=F32))

    lax.fori_loop(0, seq // c, step, jnp.zeros((HEAD_DIM, HEAD_DIM), F32))
    o_ref[0] = (_rms(o0_s[...], nw_ref[...]) * _silu(zg_ref[0])).astype(BF16)


def gated_deltanet(z3, conv_w, prm, norm_w):
    bsz, seq, _ = z3.shape
    zspec = lambda cb: pl.BlockSpec((1, seq, LANES), lambda b, h: (b, 0, cb + h))
    wspec = lambda cb: pl.BlockSpec((4, LANES), lambda b, h: (0, cb + h))
    nchunk = seq // GDN_CHUNK
    return pl.pallas_call(
        functools.partial(_gdn_body, seq=seq),
        grid=(bsz, N_HEADS),
        in_specs=[zspec(AB_QKV), zspec(AB_QKV + 16), zspec(AB_QKV + 32), zspec(AB_GATE),
                  pl.BlockSpec((1, seq, LANES), lambda b, h: (b, 0, AB_SMALL)),
                  wspec(0), wspec(16), wspec(32),
                  pl.BlockSpec((8, LANES), lambda b, h: (0, 0)),
                  pl.BlockSpec((1, LANES), lambda b, h: (0, 0)),
                  pl.BlockSpec((GDN_GROUP, GDN_GROUP), lambda b, h: (0, 0))],
        out_specs=pl.BlockSpec((1, seq, LANES), lambda b, h: (b, 0, h)),
        out_shape=jax.ShapeDtypeStruct((bsz, seq, N_HEADS * HEAD_DIM), BF16),
        scratch_shapes=[pltpu.VMEM((seq, LANES), F32) for _ in range(6)]
        + [pltpu.VMEM((nchunk * HEAD_DIM, HEAD_DIM), F32),
           pltpu.VMEM((nchunk * 8, LANES), F32),
           pltpu.VMEM((seq, LANES), BF16),
           pltpu.VMEM((nchunk * HEAD_DIM, HEAD_DIM), BF16)],
        compiler_params=_cparams(("parallel", "parallel")),
        name="gated_deltanet",
    )(z3, z3, z3, z3, z3, conv_w, conv_w, conv_w, prm, norm_w.reshape(1, HEAD_DIM), _gdn_ltri())


HGRN_LEVELS = tuple(HGRN_CHUNK >> (i + 1) for i in range(8))


def _hgrn_selectors():
    c = HGRN_CHUNK
    r = np.arange(c)[:, None]
    t = np.arange(c)[None, :]
    mats = [r >= t]
    for s in HGRN_LEVELS:
        isq = ((r // s) % 2) == 1
        mats.append(((r // s) == (t // s)) & ((isq & (t <= r)) | (~isq & (t > r))))
    return jnp.asarray(np.concatenate(mats, axis=0).astype(np.float32), dtype=BF16)


def _hgrn_body(zq_ref, zf_ref, zi_ref, zg_ref, lb_ref, nw_ref, sel_ref, o_ref, *, seq):
    c = HGRN_CHUNK
    r = lax.broadcasted_iota(jnp.int32, (c, c), 0)
    cc = lax.broadcasted_iota(jnp.int32, (c, c), 1)
    diag = r == cc
    x = r ^ cc
    lev = jnp.zeros((c, c), jnp.int32)
    for bit in range(1, 8):
        lev = lev + (x >= (1 << bit)).astype(jnp.int32)
    lev = jnp.where(r > cc, lev, -1)
    rr = lax.broadcasted_iota(jnp.int32, (c, LANES), 0)
    lb = lb_ref[...]

    def chunk(n, st):
        off = pl.multiple_of(n * c, c)
        fx = zf_ref[0, pl.ds(off, c), :]
        q = _silu(zq_ref[0, pl.ds(off, c), :])
        vb = zi_ref[0, pl.ds(off, c), :].astype(BF16)
        lf = jnp.log(lb + (1.0 - lb) * jax.nn.sigmoid(fx))
        k = (1.0 - lb) * jax.nn.sigmoid(-fx)
        sums = _dot_sel(sel_ref[...], lf)
        b = sums[0:c]
        att = jnp.where(diag, jnp.sum(q * k, -1, keepdims=True), 0.0)
        for i, s in enumerate(HGRN_LEVELS):
            e = jnp.exp(sums[(i + 1) * c:(i + 2) * c])
            isq = ((rr // s) % 2) == 1
            qs = jnp.where(isq, q * e, 0.0).astype(BF16)
            ks = jnp.where(isq, 0.0, k * e).astype(BF16)
            part = lax.dot_general(qs, ks, NT, preferred_element_type=F32)
            att = jnp.where(lev == (7 - i), part, att)
        bl = b[c - 1:c, :]
        o = lax.dot_general((q * jnp.exp(b)).astype(BF16), st.astype(BF16), NT, preferred_element_type=F32)
        o = o + jnp.dot(att.astype(BF16), vb, preferred_element_type=F32)
        o = _rms(o, nw_ref[...]) * _silu(zg_ref[0, pl.ds(off, c), :])
        o_ref[0, pl.ds(off, c), :] = o.astype(BF16)
        kd = (k * jnp.exp(bl - b)).astype(BF16)
        return st * jnp.exp(bl) + lax.dot_general(vb, kd, TN, preferred_element_type=F32)

    lax.fori_loop(0, seq // c, chunk, jnp.zeros((HEAD_DIM, HEAD_DIM), F32))


def hgrn2(z3, lb, norm_w):
    bsz, seq, _ = z3.shape
    zspec = lambda cb: pl.BlockSpec((1, seq, LANES), lambda b, h: (b, 0, cb + h))
    sel = _hgrn_selectors()
    return pl.pallas_call(
        functools.partial(_hgrn_body, seq=seq),
        grid=(bsz, N_HEADS),
        in_specs=[zspec(CD_Q), zspec(CD_F), zspec(CD_I), zspec(CD_G),
                  pl.BlockSpec((1, LANES), lambda b, h: (0, h)),
                  pl.BlockSpec((1, LANES), lambda b, h: (0, 0)),
                  pl.BlockSpec(sel.shape, lambda b, h: (0, 0))],
        out_specs=pl.BlockSpec((1, seq, LANES), lambda b, h: (b, 0, h)),
        out_shape=jax.ShapeDtypeStruct((bsz, seq, N_HEADS * HEAD_DIM), BF16),
        compiler_params=_cparams(("parallel", "parallel")),
        name="hgrn2",
    )(z3, z3, z3, z3, lb.reshape(1, N_HEADS * HEAD_DIM), norm_w.reshape(1, HEAD_DIM), sel)


FOX_BLK = 256


def _fox_gate_body(sm_ref, bias_ref, cumt_ref, *, seq):
    c = FOX_BLK
    r = lax.broadcasted_iota(jnp.int32, (c, c), 0)
    cc = lax.broadcasted_iota(jnp.int32, (c, c), 1)
    ltri = (r >= cc).astype(F32)
    r1 = lax.broadcasted_iota(jnp.int32, (LANES, LANES), 0)
    c1 = lax.broadcasted_iota(jnp.int32, (LANES, LANES), 1)
    eye = (r1 == c1).astype(F32)

    def blk(n, carry):
        off = pl.multiple_of(n * c, c)
        ls = jax.nn.log_sigmoid(sm_ref[0, pl.ds(off, c), :] + bias_ref[...])
        cum = carry + jnp.dot(ltri, ls, precision=HI, preferred_element_type=F32)
        cumt_ref[0, :, pl.ds(off, c)] = lax.dot_general(eye, cum, NT, precision=HI, preferred_element_type=F32)
        return cum[c - 1:c, :]

    lax.fori_loop(0, seq // c, blk, jnp.zeros((1, LANES), F32))


def fox_gates(z3, bias_row):
    bsz, seq, _ = z3.shape
    return pl.pallas_call(
        functools.partial(_fox_gate_body, seq=seq),
        grid=(bsz,),
        in_specs=[pl.BlockSpec((1, seq, LANES), lambda b: (b, 0, CD_SMALL)),
                  pl.BlockSpec((1, LANES), lambda b: (0, 0))],
        out_specs=pl.BlockSpec((1, LANES, seq), lambda b: (b, 0, 0)),
        out_shape=jax.ShapeDtypeStruct((bsz, LANES, seq), F32),
        compiler_params=_cparams(("parallel",)),
        name="fox_gates",
    )(z3, bias_row)


ATT_ROWS = 128
ATT_TK = 256


def _online(s, vj, carry):
    m, l, acc = carry
    m_new = jnp.maximum(m, jnp.max(s, -1, keepdims=True))
    p = jnp.exp(s - m_new)
    alpha = jnp.exp(m - m_new)
    l = alpha * l + jnp.sum(p, -1, keepdims=True)
    acc = alpha * acc + jnp.dot(p.astype(BF16), vj, preferred_element_type=F32)
    return m_new, l, acc


def _online_multi(s_list, vj, carry):
    ps, stats = [], []
    for s, (m, l, _) in zip(s_list, carry):
        m_new = jnp.maximum(m, jnp.max(s, -1, keepdims=True))
        p = jnp.exp(s - m_new)
        alpha = jnp.exp(m - m_new)
        ps.append(p.astype(BF16))
        stats.append((m_new, alpha, alpha * l + jnp.sum(p, -1, keepdims=True)))
    pv = jnp.dot(jnp.concatenate(ps, axis=0), vj, preferred_element_type=F32)
    return tuple((m_new, l, alpha * acc + pv[i * ATT_ROWS:(i + 1) * ATT_ROWS])
                 for i, ((m_new, alpha, l), (_, _, acc)) in enumerate(zip(stats, carry)))


def _flash_loop(lo, hi, qk_fn, step_fn, carry):
    def body(j, c):
        s_cur, inner = c
        s_next = qk_fn(jnp.minimum(j + 1, jnp.maximum(hi - 1, lo)))
        return s_next, step_fn(j, s_cur, inner)

    return lax.fori_loop(lo, hi, body, (qk_fn(lo), carry))[1]


def _online_init(n):
    one = (jnp.full((ATT_ROWS, 1), NEG, F32), jnp.zeros((ATT_ROWS, 1), F32), jnp.zeros((ATT_ROWS, HEAD_DIM), F32))
    return tuple(one for _ in range(n))


def _fox_attn_body(q_ref, k_ref, v_ref, cumt_ref, o_ref, *, tq):
    hd = pl.program_id(1)
    qi = pl.program_id(2)
    tk = ATT_TK
    nch = tq // ATT_ROWS
    ndiag = tq // tk
    qall = (q_ref[0] * (HEAD_DIM ** -0.5)).astype(BF16)
    qs = [qall[ch * ATT_ROWS:(ch + 1) * ATT_ROWS] for ch in range(nch)]
    row = lax.broadcasted_iota(jnp.int32, (ATT_ROWS, tk), 0)
    col = lax.broadcasted_iota(jnp.int32, (ATT_ROWS, tk), 1)

    def load(j):
        off = pl.multiple_of(j * tk, tk)
        kj = k_ref[0, pl.ds(off, tk), :].astype(BF16)
        vj = v_ref[0, pl.ds(off, tk), :].astype(BF16)
        ck = cumt_ref[0, pl.ds(hd % 8, 1), pl.ds(off, tk)]
        return kj, vj, ck

    def qk(j):
        off = pl.multiple_of(j * tk, tk)
        return lax.dot_general(qall, k_ref[0, pl.ds(off, tk), :].astype(BF16), NT, preferred_element_type=F32)

    def full_blk(j, s, carry):
        off = pl.multiple_of(j * tk, tk)
        vj = v_ref[0, pl.ds(off, tk), :].astype(BF16)
        ck = cumt_ref[0, pl.ds(hd % 8, 1), pl.ds(off, tk)]
        return _online_multi([s[ch * ATT_ROWS:(ch + 1) * ATT_ROWS] - ck for ch in range(nch)], vj, carry)

    carry = _flash_loop(0, qi * ndiag, qk, full_blk, _online_init(nch))
    carry = list(carry)
    for d in range(ndiag):
        kj, vj, ck = load(qi * ndiag + d)
        for ch in range(nch):
            if d * tk > ch * ATT_ROWS + ATT_ROWS - 1:
                continue
            s = lax.dot_general(qs[ch], kj, NT, preferred_element_type=F32) - ck
            if d * tk + tk - 1 > ch * ATT_ROWS:
                s = jnp.where(d * tk + col <= ch * ATT_ROWS + row, s, NEG)
            carry[ch] = _online(s, vj, carry[ch])
    for ch in range(nch):
        _, l, acc = carry[ch]
        o_ref[0, ch * ATT_ROWS:(ch + 1) * ATT_ROWS, :] = (acc / l).astype(BF16)


def fox_attention(z3, cumt):
    bsz, seq, _ = z3.shape
    tq = min(512, seq)
    full = lambda cb: pl.BlockSpec((1, seq, LANES), lambda b, h, i: (b, 0, cb + h))
    return pl.pallas_call(
        functools.partial(_fox_attn_body, tq=tq),
        grid=(bsz, N_HEADS, seq // tq),
        in_specs=[pl.BlockSpec((1, tq, LANES), lambda b, h, i: (b, i, CD_FQKV + h)),
                  full(CD_FQKV + 16), full(CD_FQKV + 32),
                  pl.BlockSpec((1, 8, seq), lambda b, h, i: (b, h // 8, 0))],
        out_specs=pl.BlockSpec((1, tq, LANES), lambda b, h, i: (b, i, h)),
        out_shape=jax.ShapeDtypeStruct((bsz, seq, N_HEADS * HEAD_DIM), BF16),
        compiler_params=_cparams(("parallel", "parallel", "arbitrary")),
        name="fox_attention",
    )(z3, z3, z3, cumt)


def _nsa_cmp_body(zk_ref, zv_ref, pek_ref, pev_ref, wk1_ref, wk2_ref, wv1_ref, wv2_ref, kc_ref, vc_ref, *, nblk):
    def compress(z_ref, pe_ref, w1_ref, w2_ref):
        u1 = jnp.zeros((nblk, HEAD_DIM), F32)
        u2 = jnp.zeros((nblk, HEAD_DIM), F32)
        for l in range(S_CMP):
            zl = z_ref[0, :, l, :]
            u1 = u1 + jnp.dot(zl, w1_ref[l * HEAD_DIM:(l + 1) * HEAD_DIM, :], precision=HI,
                              preferred_element_type=F32)
            u2 = u2 + jnp.dot(zl, w1_ref[(S_CMP + l) * HEAD_DIM:(S_CMP + l + 1) * HEAD_DIM, :], precision=HI,
                              preferred_element_type=F32)
        pe = jnp.dot(jnp.broadcast_to(pe_ref[...], (8, L_CMP * HEAD_DIM)), w1_ref[...], precision=HI,
                     preferred_element_type=F32)[0:1]
        hmid = u1 + pltpu.roll(u2, nblk - 1, 0) + pe
        return jnp.dot(_silu(hmid), w2_ref[...], precision=HI, preferred_element_type=F32)

    kc_ref[0, 0] = compress(zk_ref, pek_ref, wk1_ref, wk2_ref)
    vc_ref[0, 0] = compress(zv_ref, pev_ref, wv1_ref, wv2_ref)


def nsa_compress(z3, pe_k, pe_v, wk1, wk2, wv1, wv2):
    bsz, seq, n = z3.shape
    nblk = seq // S_CMP
    z4 = z3.reshape(bsz, nblk, S_CMP, n)
    zspec = lambda cb: pl.BlockSpec((1, nblk, S_CMP, LANES), lambda b, g: (b, 0, 0, cb + g))
    wfull = lambda a: pl.BlockSpec(a.shape, lambda b, g: (0, 0))
    pek = pe_k.reshape(1, L_CMP * HEAD_DIM)
    pev = pe_v.reshape(1, L_CMP * HEAD_DIM)
    out = jax.ShapeDtypeStruct((bsz, G_NSA, nblk, HEAD_DIM), F32)
    ospec = pl.BlockSpec((1, 1, nblk, HEAD_DIM), lambda b, g: (b, g, 0, 0))
    return pl.pallas_call(
        functools.partial(_nsa_cmp_body, nblk=nblk),
        grid=(bsz, G_NSA),
        in_specs=[zspec(AB_NKV), zspec(AB_NKV + G_NSA), wfull(pek), wfull(pev),
                  wfull(wk1), wfull(wk2), wfull(wv1), wfull(wv2)],
        out_specs=[ospec, ospec],
        out_shape=[out, out],
        compiler_params=_cparams(("parallel", "parallel")),
        name="nsa_compress",
    )(z4, z4, pek, pev, wk1, wk2, wv1, wv2)


NSA_TQ = 128


def _nsa_expand(seq):
    m = np.arange(LANES)[:, None]
    t = np.arange(seq)[None, :]
    return jnp.asarray(((t // L_SLC) == m).astype(np.float32), dtype=BF16)


def _nsa_attn_body(q_ref, kc_ref, vc_ref, ks_ref, vs_ref, kw_ref, vw_ref, sm_ref, exp_ref, o_ref, *, seq):
    g = pl.program_id(1)
    qi = pl.program_id(2)
    tq = NSA_TQ
    rows = HPG_NSA * tq
    nblk = seq // S_CMP
    n_slc = seq // L_SLC
    n_top = min(N_SEL, n_slc)
    q0 = qi * tq

    qall = q_ref[0]
    qs = jnp.concatenate([qall[:, p * HEAD_DIM:(p + 1) * HEAD_DIM] for p in range(HPG_NSA)], axis=0)
    qs = qs * (HEAD_DIM ** -0.5)
    qb = qs.astype(BF16)
    tpos = q0 + lax.broadcasted_iota(jnp.int32, (rows, 1), 0) % tq

    kc = kc_ref[0, 0]
    vc = vc_ref[0, 0]
    sc = lax.dot_general(qs, kc, NT, precision=HI, preferred_element_type=F32)
    nidx = lax.broadcasted_iota(jnp.int32, (rows, nblk), 1)
    cmask = (nidx * S_CMP + (L_CMP - 1) <= tpos) & (nidx <= nblk - 2)
    scm = jnp.where(cmask, sc, NEG)
    mc = jnp.max(scm, -1, keepdims=True)
    ec = jnp.where(cmask, jnp.exp(scm - mc), 0.0)
    dc = jnp.sum(ec, -1, keepdims=True)
    p_cmp = ec / jnp.where(dc > 0, dc, 1.0)
    o_cmp = jnp.dot(p_cmp.astype(BF16), vc.astype(BF16), preferred_element_type=F32)

    psum = p_cmp[0:tq]
    for p in range(1, HPG_NSA):
        psum = psum + p_cmp[p * tq:(p + 1) * tq]
    cs = lax.broadcasted_iota(jnp.int32, (nblk, LANES), 0) * S_CMP
    ss = lax.broadcasted_iota(jnp.int32, (nblk, LANES), 1) * L_SLC
    overlap = ((cs < ss + L_SLC) & (cs + L_CMP > ss) & (cs <= seq - L_CMP) & (ss < seq)).astype(F32)
    imp = jnp.dot(psum, overlap, precision=HI, preferred_element_type=F32)
    blk = lax.broadcasted_iota(jnp.int32, (tq, LANES), 1)
    cur = (q0 + lax.broadcasted_iota(jnp.int32, (tq, LANES), 0)) // L_SLC
    valid = blk <= cur
    forced = (blk == 0) | (blk == cur) | (blk == cur - 1)
    score = jnp.where(valid, jnp.where(forced, FORCE_SCORE, imp), NEG)
    rank = jnp.zeros((tq, LANES), jnp.int32)
    for mp in range(n_slc):
        colv = score[:, mp:mp + 1]
        rank = rank + ((colv > score) | ((colv == score) & (mp < blk))).astype(jnp.int32)
    selb = (valid & (rank < n_top)).astype(F32).astype(BF16)

    tk = ATT_TK
    qrow = q0 + lax.broadcasted_iota(jnp.int32, (tq, tk), 0)
    col = lax.broadcasted_iota(jnp.int32, (tq, tk), 1)

    def branch(k_ref, v_ref, lo, hi, bias_fn):
        def qk(j):
            off = pl.multiple_of(j * tk, tk)
            return lax.dot_general(qb, k_ref[0, pl.ds(off, tk), :].astype(BF16), NT,
                                   preferred_element_type=F32)

        def step(j, s, carry):
            off = pl.multiple_of(j * tk, tk)
            vj = v_ref[0, pl.ds(off, tk), :].astype(BF16)
            bias = bias_fn(off)
            return _online_multi([s[p * tq:(p + 1) * tq] + bias for p in range(HPG_NSA)], vj, carry)

        return [acc / l for (_, l, acc) in _flash_loop(lo, hi, qk, step, _online_init(HPG_NSA))]

    def slc_bias(off):
        sel = jnp.dot(selb, exp_ref[:, pl.ds(off, tk)], preferred_element_type=F32)
        return jnp.where((sel > 0.5) & (off + col <= qrow), 0.0, NEG)

    last = (q0 + tq - 1) // tk
    o_slc = branch(ks_ref, vs_ref, 0, last + 1, slc_bias)

    def win_bias(off):
        d = qrow - (off + col)
        return jnp.where((d >= 0) & (d < WINDOW), 0.0, NEG)

    o_win = branch(kw_ref, vw_ref, jnp.maximum(q0 - (WINDOW - 1), 0) // tk, last + 1, win_bias)

    sg = jax.nn.sigmoid(sm_ref[0])
    for p in range(HPG_NSA):
        sl = slice(p * tq, (p + 1) * tq)
        base = 2 * N_HEADS + g * HPG_NSA + p
        o = (_lane_col(sg, blk, base) * o_cmp[sl] + _lane_col(sg, blk, base + N_HEADS) * o_slc[p]
             + _lane_col(sg, blk, base + 2 * N_HEADS) * o_win[p])
        o_ref[0, :, p * HEAD_DIM:(p + 1) * HEAD_DIM] = o.astype(BF16)


def nsa_attention(z3, kc, vc):
    bsz, seq, _ = z3.shape
    tq = NSA_TQ
    nblk = seq // S_CMP
    full = lambda cb: pl.BlockSpec((1, seq, LANES), lambda b, g, i: (b, 0, cb + g))
    cspec = pl.BlockSpec((1, 1, nblk, HEAD_DIM), lambda b, g, i: (b, g, 0, 0))
    qw = HPG_NSA * HEAD_DIM
    return pl.pallas_call(
        functools.partial(_nsa_attn_body, seq=seq),
        grid=(bsz, G_NSA, seq // tq),
        in_specs=[pl.BlockSpec((1, tq, qw), lambda b, g, i: (b, i, AB_NQ * LANES // qw + g)),
                  cspec, cspec,
                  full(AB_NKV + 2 * G_NSA), full(AB_NKV + 3 * G_NSA),
                  full(AB_NKV + 4 * G_NSA), full(AB_NKV + 5 * G_NSA),
                  pl.BlockSpec((1, tq, LANES), lambda b, g, i: (b, i, AB_SMALL)),
                  pl.BlockSpec((LANES, seq), lambda b, g, i: (0, 0))],
        out_specs=pl.BlockSpec((1, tq, qw), lambda b, g, i: (b, i, g)),
        out_shape=jax.ShapeDtypeStruct((bsz, seq, N_HEADS * HEAD_DIM), BF16),
        compiler_params=_cparams(("parallel", "parallel", "arbitrary")),
        name="nsa_attention",
    )(z3, kc, vc, z3, z3, z3, z3, z3, _nsa_expand(seq))


def _pad_cols(w, width):
    return jnp.pad(w, ((0, 0), (0, width - w.shape[1])))


def _ab_weight(w):
    small = _pad_cols(jnp.concatenate([w[:, 6144:6176], w[:, 13344:13392]], axis=1), SMALL_W)
    return jnp.concatenate([w[:, :6144], w[:, 6176:8224], w[:, 8224:10272], w[:, 10272:13344], small],
                           axis=1).astype(BF16)


def _cd_weight(w):
    return jnp.concatenate([w[:, :14336], _pad_cols(w[:, 14336:14352], SMALL_W)], axis=1).astype(BF16)


def _row128(v):
    return jnp.pad(v.astype(F32), (0, LANES - v.shape[0])).reshape(1, LANES)


def kernel(x, p, ab_norm_pre, ab_norm_post, ab_w_in, gdn_conv_w, gdn_a_log, gdn_dt_bias, gdn_norm, nsa_pe_k, nsa_pe_v, nsa_cmp_k1, nsa_cmp_k2, nsa_cmp_v1, nsa_cmp_v2, ab_w_out, cd_norm_pre, cd_norm_post, cd_w_in, hgrn_lb_logits, hgrn_norm, fox_f_bias, cd_w_out, ffn_norm_pre, ffn_norm_post, ffn_w_up, ffn_conv_w, ffn_conv_b, ffn_w_down, ple_w_proj, ple_gate_norm, ple_w_gate, ple_norm_post):
    bsz, seq, dm = x.shape
    depth = p.shape[0]
    m = bsz * seq
    half = N_HEADS * HEAD_DIM
    sm_ = jax.nn.softmax(hgrn_lb_logits.astype(F32), axis=0)
    lb_table = jnp.cumsum(sm_, axis=0) - sm_[0]
    xf = x.reshape(m, dm)
    for li in range(depth):
        j = li // 2
        if li % 2 == 0:
            z3 = norm_matmul(xf, ab_norm_pre[j], _ab_weight(ab_w_in[j])).reshape(bsz, seq, AB_N)
            prm = jnp.concatenate([_row128(gdn_a_log[j]), _row128(gdn_dt_bias[j]),
                                   jnp.zeros((6, LANES), F32)], axis=0)
            o_a = gated_deltanet(z3, gdn_conv_w[j], prm, gdn_norm[j])
            kc, vc = nsa_compress(z3, nsa_pe_k[j], nsa_pe_v[j], nsa_cmp_k1[j], nsa_cmp_k2[j],
                                  nsa_cmp_v1[j], nsa_cmp_v2[j])
            o_b = nsa_attention(z3, kc, vc)
            w_out, post = ab_w_out[j], ab_norm_post[j]
        else:
            z3 = norm_matmul(xf, cd_norm_pre[j], _cd_weight(cd_w_in[j])).reshape(bsz, seq, CD_N)
            o_a = hgrn2(z3, lb_table[li], hgrn_norm[j])
            o_b = fox_attention(z3, fox_gates(z3, _row128(fox_f_bias[j])))
            w_out, post = cd_w_out[j], cd_norm_post[j]
        w_out = w_out.astype(BF16)
        xf = outproj(o_a.reshape(m, half), o_b.reshape(m, half), w_out[:half], w_out[half:], xf, post)
        xf = conv_ffn(xf, ffn_norm_pre[li], ffn_w_up[li].astype(BF16), ffn_conv_w[li], ffn_conv_b[li],
                      ffn_w_down[li].astype(BF16), ffn_norm_post[li], seq)
        xf = ple(xf, p[li].reshape(m, -1), ple_gate_norm[li], ple_w_gate[li].astype(BF16),
                 ple_w_proj[li].astype(BF16), ple_norm_post[li])
    return xf.reshape(bsz, seq, dm)
```

```python
import functools

import numpy as np
import jax
import jax.numpy as jnp
from jax import lax
from jax.experimental import pallas as pl
from jax.experimental.pallas import tpu as pltpu

F32 = jnp.float32
BF16 = jnp.bfloat16
HI = lax.Precision.HIGHEST
NT = (((1,), (1,)), ((), ()))
TN = (((0,), (0,)), ((), ()))

D_MODEL = 4096
HEAD_DIM = 128
N_HEADS = 16
G_NSA = 4
HPG_NSA = 4
L_CMP = 32
S_CMP = 16
L_SLC = 64
N_SEL = 8
WINDOW = 512
FORCE_SCORE = 1e4
GDN_CHUNK = 64
GDN_GROUP = 256
GDN_LOCKSTEP = 8
HGRN_CHUNK = 256
D_FF = 11008
EPS = 1e-6
NEG = -1e30
LANES = 128
SMALL_W = 512

AB_QKV, AB_GATE, AB_NQ, AB_NKV, AB_SMALL = 0, 48, 64, 80, 104
AB_N = 108 * LANES
CD_Q, CD_F, CD_I, CD_G, CD_FQKV, CD_SMALL = 0, 16, 32, 48, 64, 112
CD_N = 116 * LANES

VMEM_LIMIT = 56 * 1024 * 1024


def _cparams(sem):
    return pltpu.CompilerParams(dimension_semantics=sem, vmem_limit_bytes=VMEM_LIMIT)


def _rms(x, w):
    return x * lax.rsqrt(jnp.mean(x * x, axis=-1, keepdims=True) + EPS) * w


def _silu(x):
    return x * jax.nn.sigmoid(x)


def _lane_col(x, lane_idx, lane):
    return jnp.sum(jnp.where(lane_idx == lane, x, 0.0), axis=-1, keepdims=True)


def _split2(x):
    hi = x.astype(BF16)
    return hi, (x - hi.astype(F32)).astype(BF16)


def _dot_sel(sel, x):
    n = x.shape[1]
    hi = x.astype(BF16)
    r1 = x - hi.astype(F32)
    mid = r1.astype(BF16)
    lo = (r1 - mid.astype(F32)).astype(BF16)
    y = jnp.dot(sel, jnp.concatenate([hi, mid, lo], axis=1), preferred_element_type=F32)
    return (y[:, :n] + y[:, n:2 * n]) + y[:, 2 * n:]


def _norm_matmul_body(x_ref, nw_ref, w_ref, o_ref, h_ref):
    @pl.when(pl.program_id(1) == 0)
    def _():
        h_ref[...] = _rms(x_ref[...], nw_ref[...]).astype(BF16)

    o_ref[...] = jnp.dot(h_ref[...], w_ref[...], preferred_element_type=F32)


def norm_matmul(x, nw, w, tm=512, tn=512):
    m, k = x.shape
    n = w.shape[1]
    return pl.pallas_call(
        _norm_matmul_body,
        grid=(m // tm, n // tn),
        in_specs=[pl.BlockSpec((tm, k), lambda i, j: (i, 0)),
                  pl.BlockSpec((1, k), lambda i, j: (0, 0)),
                  pl.BlockSpec((k, tn), lambda i, j: (0, j))],
        out_specs=pl.BlockSpec((tm, tn), lambda i, j: (i, j)),
        out_shape=jax.ShapeDtypeStruct((m, n), F32),
        scratch_shapes=[pltpu.VMEM((tm, k), BF16)],
        compiler_params=_cparams(("parallel", "arbitrary")),
        name="norm_matmul",
    )(x, nw.reshape(1, k), w)


def _outproj_body(oa_ref, ob_ref, wa_ref, wb_ref, x_ref, nw_ref, o_ref, *, tn):
    j = pl.program_id(1)
    y = jnp.dot(oa_ref[...], wa_ref[...], preferred_element_type=F32)
    y = y + jnp.dot(ob_ref[...], wb_ref[...], preferred_element_type=F32)
    o_ref[:, pl.ds(pl.multiple_of(j * tn, tn), tn)] = y

    @pl.when(j == pl.num_programs(1) - 1)
    def _():
        o_ref[...] = x_ref[...] + _rms(o_ref[...], nw_ref[...])


def outproj(oa, ob, w, x, nw, tm=512, tn=512):
    m, ka = oa.shape
    n = w.shape[1]
    return pl.pallas_call(
        functools.partial(_outproj_body, tn=tn),
        grid=(m // tm, n // tn),
        in_specs=[pl.BlockSpec((tm, ka), lambda i, j: (i, 0)),
                  pl.BlockSpec((tm, ka), lambda i, j: (i, 0)),
                  pl.BlockSpec((ka, tn), lambda i, j: (0, j)),
                  pl.BlockSpec((ka, tn), lambda i, j: (1, j)),
                  pl.BlockSpec((tm, n), lambda i, j: (i, 0), pipeline_mode=pl.Buffered(1)),
                  pl.BlockSpec((1, n), lambda i, j: (0, 0))],
        out_specs=pl.BlockSpec((tm, n), lambda i, j: (i, 0)),
        out_shape=jax.ShapeDtypeStruct((m, n), F32),
        compiler_params=_cparams(("parallel", "arbitrary")),
        name="outproj",
    )(oa, ob, w, w, x, nw.reshape(1, n))


FFN_HALO = 16


def _ffn_body(x_ref, xh_ref, nw_ref, wg_ref, wu_ref, cwg_ref, cwu_ref, cbg_ref, cbu_ref,
              wd_ref, pw_ref, o_ref, h_s, *, tm, seq):
    i = pl.program_id(0)
    j = pl.program_id(1)

    @pl.when(j == 0)
    def _():
        keep = jnp.where((i * tm) % seq == 0, 0.0, 1.0)
        h_s[0:FFN_HALO, :] = (_rms(xh_ref[...], nw_ref[...]) * keep).astype(BF16)
        h_s[FFN_HALO:, :] = _rms(x_ref[...], nw_ref[...]).astype(BF16)
        o_ref[...] = jnp.zeros_like(o_ref)

    h = h_s[...]

    def branch(w_ref, cw_ref, cb_ref):
        u = jnp.dot(h, w_ref[...], preferred_element_type=F32)
        cw = cw_ref[...]
        y = u * cw[2:3] + pltpu.roll(u, 1, 0) * cw[1:2] + pltpu.roll(u, 2, 0) * cw[0:1]
        return y[FFN_HALO:] + cb_ref[...]

    g = branch(wg_ref, cwg_ref, cbg_ref)
    u = branch(wu_ref, cwu_ref, cbu_ref)
    act = (_silu(g) * u).astype(BF16)
    o_ref[...] += jnp.dot(act, wd_ref[...], preferred_element_type=F32)

    @pl.when(j == pl.num_programs(1) - 1)
    def _():
        o_ref[...] = x_ref[...] + _rms(o_ref[...], pw_ref[...])


def conv_ffn(x, li, nw, w_up, conv_w, conv_b, w_down, pw, seq, tm=512, tf=256):
    m, k = x.shape
    nl, f, _ = w_down.shape
    nf = f // tf
    hb = tm // FFN_HALO
    cb = conv_b.reshape(nl, 1, 2 * f)
    row = pl.BlockSpec((None, 1, k), lambda i, j: (li, 0, 0))
    return pl.pallas_call(
        functools.partial(_ffn_body, tm=tm, seq=seq),
        grid=(m // tm, nf),
        in_specs=[pl.BlockSpec((tm, k), lambda i, j: (i, 0), pipeline_mode=pl.Buffered(1)),
                  pl.BlockSpec((FFN_HALO, k), lambda i, j: (jnp.maximum(i * hb - 1, 0), 0)),
                  row,
                  pl.BlockSpec((None, k, tf), lambda i, j: (li, 0, j)),
                  pl.BlockSpec((None, k, tf), lambda i, j: (li, 0, nf + j)),
                  pl.BlockSpec((None, 3, tf), lambda i, j: (li, 0, j)),
                  pl.BlockSpec((None, 3, tf), lambda i, j: (li, 0, nf + j)),
                  pl.BlockSpec((None, 1, tf), lambda i, j: (li, 0, j)),
                  pl.BlockSpec((None, 1, tf), lambda i, j: (li, 0, nf + j)),
                  pl.BlockSpec((None, tf, k), lambda i, j: (li, j, 0)),
                  row],
        out_specs=pl.BlockSpec((tm, k), lambda i, j: (i, 0)),
        out_shape=jax.ShapeDtypeStruct((m, k), F32),
        scratch_shapes=[pltpu.VMEM((tm + FFN_HALO, k), BF16)],
        compiler_params=_cparams(("parallel", "arbitrary")),
        name="conv_ffn",
    )(x, x, nw.reshape(nl, 1, k), w_up, w_up, conv_w, conv_w, cb, cb, w_down, pw.reshape(nl, 1, k))


def _ple_body(x_ref, p_ref, gnw_ref, wg_ref, wp_ref, pnw_ref, o_ref, h_s, *, tn):
    j = pl.program_id(1)

    @pl.when(j == 0)
    def _():
        h_s[...] = _rms(x_ref[...], gnw_ref[...]).astype(BF16)

    gate = jax.nn.sigmoid(jnp.dot(h_s[...], wg_ref[...], preferred_element_type=F32))
    proj = jnp.dot(p_ref[...].astype(BF16), wp_ref[...], preferred_element_type=F32)
    o_ref[:, pl.ds(pl.multiple_of(j * tn, tn), tn)] = gate * proj

    @pl.when(j == pl.num_programs(1) - 1)
    def _():
        o_ref[...] = x_ref[...] + _rms(o_ref[...], pnw_ref[...])


def ple(x, li, p, gnw, wg, wp, pnw, tm=512, tn=512):
    m, k = x.shape
    nl, _, dp = p.shape
    row = pl.BlockSpec((None, 1, k), lambda i, j: (li, 0, 0))
    return pl.pallas_call(
        functools.partial(_ple_body, tn=tn),
        grid=(m // tm, k // tn),
        in_specs=[pl.BlockSpec((tm, k), lambda i, j: (i, 0), pipeline_mode=pl.Buffered(1)),
                  pl.BlockSpec((None, tm, dp), lambda i, j: (li, i, 0)),
                  row,
                  pl.BlockSpec((None, k, tn), lambda i, j: (li, 0, j)),
                  pl.BlockSpec((None, dp, tn), lambda i, j: (li, 0, j)),
                  row],
        out_specs=pl.BlockSpec((tm, k), lambda i, j: (i, 0)),
        out_shape=jax.ShapeDtypeStruct((m, k), F32),
        scratch_shapes=[pltpu.VMEM((tm, k), BF16)],
        compiler_params=_cparams(("parallel", "arbitrary")),
        name="ple",
    )(x, p, gnw.reshape(nl, 1, k), wg, wp, pnw.reshape(nl, 1, k))


def _gdn_ltri():
    r = np.arange(GDN_GROUP)[:, None]
    t = np.arange(GDN_GROUP)[None, :]
    return jnp.asarray((((r // GDN_CHUNK) == (t // GDN_CHUNK)) & (r >= t)).astype(np.float32), dtype=BF16)


def _gdn_body(zq_ref, zk_ref, zv_ref, zg_ref, sm_ref, cwq_ref, cwk_ref, cwv_ref, prm_ref, nw_ref, ltri_ref,
              o_ref, q_s, k_s, v_s, g_s, b_s, o0_s, n0_s, gl_s, qe_s, m_s, *, seq):
    hd = pl.program_id(1)
    c = GDN_CHUNK
    gs = GDN_GROUP
    row = lax.broadcasted_iota(jnp.int32, (seq, LANES), 0)
    lane = lax.broadcasted_iota(jnp.int32, (seq, LANES), 1)

    def conv_silu(z_ref, w_ref):
        z = z_ref[0]
        w = w_ref[...]
        y = z * w[3:4]
        for s in (1, 2, 3):
            y = y + jnp.where(row >= s, pltpu.roll(z, s, 0), 0.0) * w[3 - s:4 - s]
        return _silu(y)

    q = conv_silu(zq_ref, cwq_ref)
    k = conv_silu(zk_ref, cwk_ref)
    q_s[...] = q * lax.rsqrt(jnp.sum(q * q, -1, keepdims=True) + EPS) * (HEAD_DIM ** -0.5)
    k_s[...] = k * lax.rsqrt(jnp.sum(k * k, -1, keepdims=True) + EPS)
    v_s[...] = conv_silu(zv_ref, cwv_ref)

    sm = sm_ref[0]
    glog = -jnp.exp(prm_ref[0:1, :]) * jax.nn.softplus(sm + prm_ref[1:2, :])
    g_s[...] = jnp.broadcast_to(_lane_col(glog, lane, hd), (seq, LANES))
    b_s[...] = jnp.broadcast_to(_lane_col(jax.nn.sigmoid(sm), lane, N_HEADS + hd), (seq, LANES))

    nper = gs // c
    rs = lax.broadcasted_iota(jnp.int32, (c, gs), 0)
    cs = lax.broadcasted_iota(jnp.int32, (c, gs), 1)
    jj = cs % c
    cblk = cs // c
    incl = rs >= jj
    strict = rs > jj
    eye_side = (rs == jj).astype(F32)
    r2 = lax.broadcasted_iota(jnp.int32, (gs, gs), 0)
    c2 = lax.broadcasted_iota(jnp.int32, (gs, gs), 1)
    bdm = ((r2 // c) == (c2 // c)).astype(F32)
    bdm_b = bdm.astype(BF16)
    lane_c = lax.broadcasted_iota(jnp.int32, (c, LANES), 1)
    lane_g = lax.broadcasted_iota(jnp.int32, (gs, LANES), 1)
    ones3 = (lane_c < 3).astype(F32).astype(BF16)

    def side_col(xb):
        left = jnp.where(lane_c < c, xb[0:c], xb[c:2 * c])
        right = jnp.where(lane_c < c, xb[2 * c:3 * c], xb[3 * c:4 * c])
        return jnp.concatenate([left, right], axis=1)

    def to_side(full):
        out = full[0:c]
        for ci in range(1, nper):
            out = jnp.where(cblk == ci, full[ci * c:(ci + 1) * c], out)
        return out

    def bd(xs, mask):
        return jnp.concatenate([xs] * nper, axis=0) * mask

    def mm3(ph, pl_, xh, xl):
        bh = bd(xh, bdm_b)
        lhs = jnp.concatenate([ph, ph, pl_], axis=1)
        rhs = jnp.concatenate([bh, bd(xl, bdm_b), bh], axis=0)
        return jnp.dot(lhs, rhs, preferred_element_type=F32)

    def group_pre(gi):
        off = pl.multiple_of(gi * gs, gs)
        qg = q_s[pl.ds(off, gs), :]
        kg = k_s[pl.ds(off, gs), :]
        bg = b_s[pl.ds(off, gs), :]
        gam = _dot_sel(ltri_ref[...], g_s[pl.ds(off, gs), :])
        g_hi = gam.astype(BF16).astype(F32)
        g_mid = (gam - g_hi).astype(BF16).astype(F32)
        g_lo = gam - g_hi - g_mid
        gam3 = jnp.where(lane_g == 0, g_hi, jnp.where(lane_g == 1, g_mid, jnp.where(lane_g == 2, g_lo, 0.0)))
        gam_row = lax.dot_general(ones3, gam3.astype(BF16), NT, preferred_element_type=F32)
        dec = jnp.where(incl, jnp.exp(jnp.where(incl, side_col(gam) - gam_row, 0.0)), 0.0)
        qb = qg.astype(BF16)
        kb = kg.astype(BF16)
        kk = to_side(lax.dot_general(kb, kb, NT, preferred_element_type=F32))
        a = jnp.where(strict, side_col(bg) * dec * kk, 0.0)
        return dict(off=off, qg=qg, kg=kg, bg=bg, gam=gam, dec=dec, qb=qb, kb=kb, a=a)

    def group_post(gi, st, tinv):
        off, qg, kg, bg, gam, dec, qb, kb = (st[n] for n in ("off", "qg", "kg", "bg", "gam", "dec", "qb", "kb"))
        vg = v_s[pl.ds(off, gs), :]
        th, tl = _split2(tinv)
        eg = jnp.exp(gam)
        rh, rl = _split2(jnp.concatenate([vg * bg, kg * (bg * eg)], axis=1))
        bth = bd(th, bdm_b)
        sol = jnp.dot(jnp.concatenate([bth, bth, bd(tl, bdm_b)], axis=1),
                      jnp.concatenate([rh, rl, rh], axis=0), preferred_element_type=F32)
        solb = sol.astype(BF16)
        qk = bd(dec, bdm) * lax.dot_general(qb, kb, NT, preferred_element_type=F32)
        qkuw = jnp.dot(qk.astype(BF16), solb, preferred_element_type=F32)
        o0_s[pl.ds(off, gs), :] = qkuw[:, :HEAD_DIM]
        qe_s[pl.ds(off, gs), :] = (qg * eg - qkuw[:, HEAD_DIM:]).astype(BF16)
        for ci in range(nper):
            lo = ci * c
            gl = gam[lo + c - 1:lo + c, :]
            kd = (kg[lo:lo + c] * jnp.exp(gl - gam[lo:lo + c])).astype(BF16)
            kds = lax.dot_general(kd, solb[lo:lo + c], TN, preferred_element_type=F32)
            so = pl.multiple_of((gi * nper + ci) * HEAD_DIM, HEAD_DIM)
            n0_s[pl.ds(so, HEAD_DIM), :] = kds[:, :HEAD_DIM]
            m_s[pl.ds(so, HEAD_DIM), :] = (-kds[:, HEAD_DIM:]).astype(BF16)
            gl_s[pl.ds(pl.multiple_of((gi * nper + ci) * 8, 8), 8), :] = jnp.broadcast_to(jnp.exp(gl), (8, LANES))

    ngroups = seq // gs
    nlock = GDN_LOCKSTEP if ngroups % GDN_LOCKSTEP == 0 else 1

    def groups(it, carry):
        gis = [it + u * (ngroups // nlock) for u in range(nlock)]
        sts = [group_pre(gi) for gi in gis]
        tinvs = [eye_side - st["a"] for st in sts]
        xs = [_split2(st["a"]) for st in sts]
        for _ in range(5):
            xs = [_split2(mm3(xh, xl, xh, xl)) for xh, xl in xs]
            ts = [_split2(t) for t in tinvs]
            tinvs = [t + mm3(th, tl, xh, xl) for t, (th, tl), (xh, xl) in zip(tinvs, ts, xs)]
        for gi, st, t in zip(gis, sts, tinvs):
            group_post(gi, st, t)
        return carry

    lax.fori_loop(0, ngroups // nlock, groups, 0)

    def step(n, s):
        off = pl.multiple_of(n * c, c)
        so = pl.multiple_of(n * HEAD_DIM, HEAD_DIM)
        sb = s.astype(BF16)
        o0_s[pl.ds(off, c), :] = o0_s[pl.ds(off, c), :] + jnp.dot(
            qe_s[pl.ds(off, c), :], sb, preferred_element_type=F32)
        return (gl_s[pl.ds(pl.multiple_of(n * 8, 8), 1), :] * s + n0_s[pl.ds(so, HEAD_DIM), :]
                + jnp.dot(m_s[pl.ds(so, HEAD_DIM), :], sb, preferred_element_type=F32))

    lax.fori_loop(0, seq // c, step, jnp.zeros((HEAD_DIM, HEAD_DIM), F32))
    o_ref[0] = (_rms(o0_s[...], nw_ref[...]) * _silu(zg_ref[0])).astype(BF16)


def gated_deltanet(z3, conv_w, prm, norm_w):
    bsz, seq, _ = z3.shape
    zspec = lambda cb: pl.BlockSpec((1, seq, LANES), lambda b, h: (b, 0, cb + h))
    wspec = lambda cb: pl.BlockSpec((4, LANES), lambda b, h: (0, cb + h))
    nchunk = seq // GDN_CHUNK
    return pl.pallas_call(
        functools.partial(_gdn_body, seq=seq),
        grid=(bsz, N_HEADS),
        in_specs=[zspec(AB_QKV), zspec(AB_QKV + 16), zspec(AB_QKV + 32), zspec(AB_GATE),
                  pl.BlockSpec((1, seq, LANES), lambda b, h: (b, 0, AB_SMALL)),
                  wspec(0), wspec(16), wspec(32),
                  pl.BlockSpec((8, LANES), lambda b, h: (0, 0)),
                  pl.BlockSpec((1, LANES), lambda b, h: (0, 0)),
                  pl.BlockSpec((GDN_GROUP, GDN_GROUP), lambda b, h: (0, 0))],
        out_specs=pl.BlockSpec((1, seq, LANES), lambda b, h: (b, 0, h)),
        out_shape=jax.ShapeDtypeStruct((bsz, seq, N_HEADS * HEAD_DIM), BF16),
        scratch_shapes=[pltpu.VMEM((seq, LANES), F32) for _ in range(6)]
        + [pltpu.VMEM((nchunk * HEAD_DIM, HEAD_DIM), F32),
           pltpu.VMEM((nchunk * 8, LANES), F32),
           pltpu.VMEM((seq, LANES), BF16),
           pltpu.VMEM((nchunk * HEAD_DIM, HEAD_DIM), BF16)],
        compiler_params=_cparams(("parallel", "parallel")),
        name="gated_deltanet",
    )(z3, z3, z3, z3, z3, conv_w, conv_w, conv_w, prm, norm_w.reshape(1, HEAD_DIM), _gdn_ltri())


HGRN_LEVELS = tuple(HGRN_CHUNK >> (i + 1) for i in range(8))


def _hgrn_selectors():
    c = HGRN_CHUNK
    r = np.arange(c)[:, None]
    t = np.arange(c)[None, :]
    mats = [r >= t]
    for s in HGRN_LEVELS:
        isq = ((r // s) % 2) == 1
        mats.append(((r // s) == (t // s)) & ((isq & (t <= r)) | (~isq & (t > r))))
    return jnp.asarray(np.concatenate(mats, axis=0).astype(np.float32), dtype=BF16)


def _hgrn_body(zq_ref, zf_ref, zi_ref, zg_ref, lb_ref, nw_ref, sel_ref, o_ref, *, seq):
    c = HGRN_CHUNK
    r = lax.broadcasted_iota(jnp.int32, (c, c), 0)
    cc = lax.broadcasted_iota(jnp.int32, (c, c), 1)
    diag = r == cc
    x = r ^ cc
    lev = jnp.zeros((c, c), jnp.int32)
    for bit in range(1, 8):
        lev = lev + (x >= (1 << bit)).astype(jnp.int32)
    lev = jnp.where(r > cc, lev, -1)
    rr = lax.broadcasted_iota(jnp.int32, (c, LANES), 0)
    lb = lb_ref[...]

    def chunk(n, st):
        off = pl.multiple_of(n * c, c)
        fx = zf_ref[0, pl.ds(off, c), :]
        q = _silu(zq_ref[0, pl.ds(off, c), :])
        vb = zi_ref[0, pl.ds(off, c), :].astype(BF16)
        lf = jnp.log(lb + (1.0 - lb) * jax.nn.sigmoid(fx))
        k = (1.0 - lb) * jax.nn.sigmoid(-fx)
        sums = _dot_sel(sel_ref[...], lf)
        b = sums[0:c]
        att = jnp.where(diag, jnp.sum(q * k, -1, keepdims=True), 0.0)
        for i, s in enumerate(HGRN_LEVELS):
            e = jnp.exp(sums[(i + 1) * c:(i + 2) * c])
            isq = ((rr // s) % 2) == 1
            qs = jnp.where(isq, q * e, 0.0).astype(BF16)
            ks = jnp.where(isq, 0.0, k * e).astype(BF16)
            part = lax.dot_general(qs, ks, NT, preferred_element_type=F32)
            att = jnp.where(lev == (7 - i), part, att)
        bl = b[c - 1:c, :]
        o = lax.dot_general((q * jnp.exp(b)).astype(BF16), st.astype(BF16), NT, preferred_element_type=F32)
        o = o + jnp.dot(att.astype(BF16), vb, preferred_element_type=F32)
        o = _rms(o, nw_ref[...]) * _silu(zg_ref[0, pl.ds(off, c), :])
        o_ref[0, pl.ds(off, c), :] = o.astype(BF16)
        kd = (k * jnp.exp(bl - b)).astype(BF16)
        return st * jnp.exp(bl) + lax.dot_general(vb, kd, TN, preferred_element_type=F32)

    lax.fori_loop(0, seq // c, chunk, jnp.zeros((HEAD_DIM, HEAD_DIM), F32))


def hgrn2(z3, lb, norm_w):
    bsz, seq, _ = z3.shape
    zspec = lambda cb: pl.BlockSpec((1, seq, LANES), lambda b, h: (b, 0, cb + h))
    sel = _hgrn_selectors()
    return pl.pallas_call(
        functools.partial(_hgrn_body, seq=seq),
        grid=(bsz, N_HEADS),
        in_specs=[zspec(CD_Q), zspec(CD_F), zspec(CD_I), zspec(CD_G),
                  pl.BlockSpec((1, LANES), lambda b, h: (0, h)),
                  pl.BlockSpec((1, LANES), lambda b, h: (0, 0)),
                  pl.BlockSpec(sel.shape, lambda b, h: (0, 0))],
        out_specs=pl.BlockSpec((1, seq, LANES), lambda b, h: (b, 0, h)),
        out_shape=jax.ShapeDtypeStruct((bsz, seq, N_HEADS * HEAD_DIM), BF16),
        compiler_params=_cparams(("parallel", "parallel")),
        name="hgrn2",
    )(z3, z3, z3, z3, lb.reshape(1, N_HEADS * HEAD_DIM), norm_w.reshape(1, HEAD_DIM), sel)


FOX_BLK = 256


def _fox_gate_body(sm_ref, bias_ref, cumt_ref, *, seq):
    c = FOX_BLK
    r = lax.broadcasted_iota(jnp.int32, (c, c), 0)
    cc = lax.broadcasted_iota(jnp.int32, (c, c), 1)
    ltri = (r >= cc).astype(F32)
    r1 = lax.broadcasted_iota(jnp.int32, (LANES, LANES), 0)
    c1 = lax.broadcasted_iota(jnp.int32, (LANES, LANES), 1)
    eye = (r1 == c1).astype(F32)

    def blk(n, carry):
        off = pl.multiple_of(n * c, c)
        ls = jax.nn.log_sigmoid(sm_ref[0, pl.ds(off, c), :] + bias_ref[...])
        cum = carry + jnp.dot(ltri, ls, precision=HI, preferred_element_type=F32)
        cumt_ref[0, :, pl.ds(off, c)] = lax.dot_general(eye, cum, NT, precision=HI, preferred_element_type=F32)
        return cum[c - 1:c, :]

    lax.fori_loop(0, seq // c, blk, jnp.zeros((1, LANES), F32))


def fox_gates(z3, bias_row):
    bsz, seq, _ = z3.shape
    return pl.pallas_call(
        functools.partial(_fox_gate_body, seq=seq),
        grid=(bsz,),
        in_specs=[pl.BlockSpec((1, seq, LANES), lambda b: (b, 0, CD_SMALL)),
                  pl.BlockSpec((1, LANES), lambda b: (0, 0))],
        out_specs=pl.BlockSpec((1, LANES, seq), lambda b: (b, 0, 0)),
        out_shape=jax.ShapeDtypeStruct((bsz, LANES, seq), F32),
        compiler_params=_cparams(("parallel",)),
        name="fox_gates",
    )(z3, bias_row)


ATT_ROWS = 128
ATT_TK = 256


def _online(s, vj, carry):
    m, l, acc = carry
    m_new = jnp.maximum(m, jnp.max(s, -1, keepdims=True))
    p = jnp.exp(s - m_new)
    alpha = jnp.exp(m - m_new)
    l = alpha * l + jnp.sum(p, -1, keepdims=True)
    acc = alpha * acc + jnp.dot(p.astype(BF16), vj, preferred_element_type=F32)
    return m_new, l, acc


def _online_multi(s_list, vj, carry):
    ps, stats = [], []
    for s, (m, l, _) in zip(s_list, carry):
        m_new = jnp.maximum(m, jnp.max(s, -1, keepdims=True))
        p = jnp.exp(s - m_new)
        alpha = jnp.exp(m - m_new)
        ps.append(p.astype(BF16))
        stats.append((m_new, alpha, alpha * l + jnp.sum(p, -1, keepdims=True)))
    pv = jnp.dot(jnp.concatenate(ps, axis=0), vj, preferred_element_type=F32)
    return tuple((m_new, l, alpha * acc + pv[i * ATT_ROWS:(i + 1) * ATT_ROWS])
                 for i, ((m_new, alpha, l), (_, _, acc)) in enumerate(zip(stats, carry)))


def _flash_loop(lo, hi, qk_fn, step_fn, carry):
    def body(j, c):
        s_cur, inner = c
        s_next = qk_fn(jnp.minimum(j + 1, jnp.maximum(hi - 1, lo)))
        return s_next, step_fn(j, s_cur, inner)

    return lax.fori_loop(lo, hi, body, (qk_fn(lo), carry))[1]


def _online_init(n):
    one = (jnp.full((ATT_ROWS, 1), NEG, F32), jnp.zeros((ATT_ROWS, 1), F32), jnp.zeros((ATT_ROWS, HEAD_DIM), F32))
    return tuple(one for _ in range(n))


def _fox_attn_body(q_ref, k_ref, v_ref, cumt_ref, o_ref, *, tq):
    hd = pl.program_id(1)
    qi = pl.program_id(2)
    tk = ATT_TK
    nch = tq // ATT_ROWS
    ndiag = tq // tk
    qall = (q_ref[0] * (HEAD_DIM ** -0.5)).astype(BF16)
    qs = [qall[ch * ATT_ROWS:(ch + 1) * ATT_ROWS] for ch in range(nch)]
    row = lax.broadcasted_iota(jnp.int32, (ATT_ROWS, tk), 0)
    col = lax.broadcasted_iota(jnp.int32, (ATT_ROWS, tk), 1)

    def load(j):
        off = pl.multiple_of(j * tk, tk)
        kj = k_ref[0, pl.ds(off, tk), :].astype(BF16)
        vj = v_ref[0, pl.ds(off, tk), :].astype(BF16)
        ck = cumt_ref[0, pl.ds(hd % 8, 1), pl.ds(off, tk)]
        return kj, vj, ck

    def qk(j):
        off = pl.multiple_of(j * tk, tk)
        return lax.dot_general(qall, k_ref[0, pl.ds(off, tk), :].astype(BF16), NT, preferred_element_type=F32)

    def full_blk(j, s, carry):
        off = pl.multiple_of(j * tk, tk)
        vj = v_ref[0, pl.ds(off, tk), :].astype(BF16)
        ck = cumt_ref[0, pl.ds(hd % 8, 1), pl.ds(off, tk)]
        return _online_multi([s[ch * ATT_ROWS:(ch + 1) * ATT_ROWS] - ck for ch in range(nch)], vj, carry)

    carry = _flash_loop(0, qi * ndiag, qk, full_blk, _online_init(nch))
    carry = list(carry)
    for d in range(ndiag):
        kj, vj, ck = load(qi * ndiag + d)
        for ch in range(nch):
            if d * tk > ch * ATT_ROWS + ATT_ROWS - 1:
                continue
            s = lax.dot_general(qs[ch], kj, NT, preferred_element_type=F32) - ck
            if d * tk + tk - 1 > ch * ATT_ROWS:
                s = jnp.where(d * tk + col <= ch * ATT_ROWS + row, s, NEG)
            carry[ch] = _online(s, vj, carry[ch])
    for ch in range(nch):
        _, l, acc = carry[ch]
        o_ref[0, ch * ATT_ROWS:(ch + 1) * ATT_ROWS, :] = (acc / l).astype(BF16)


def fox_attention(z3, cumt):
    bsz, seq, _ = z3.shape
    tq = min(512, seq)
    full = lambda cb: pl.BlockSpec((1, seq, LANES), lambda b, h, i: (b, 0, cb + h))
    return pl.pallas_call(
        functools.partial(_fox_attn_body, tq=tq),
        grid=(bsz, N_HEADS, seq // tq),
        in_specs=[pl.BlockSpec((1, tq, LANES), lambda b, h, i: (b, i, CD_FQKV + h)),
                  full(CD_FQKV + 16), full(CD_FQKV + 32),
                  pl.BlockSpec((1, 8, seq), lambda b, h, i: (b, h // 8, 0))],
        out_specs=pl.BlockSpec((1, tq, LANES), lambda b, h, i: (b, i, h)),
        out_shape=jax.ShapeDtypeStruct((bsz, seq, N_HEADS * HEAD_DIM), BF16),
        compiler_params=_cparams(("parallel", "parallel", "arbitrary")),
        name="fox_attention",
    )(z3, z3, z3, cumt)


def _nsa_cmp_body(zk_ref, zv_ref, pek_ref, pev_ref, wk1_ref, wk2_ref, wv1_ref, wv2_ref, kc_ref, vc_ref, *, nblk):
    def compress(z_ref, pe_ref, w1_ref, w2_ref):
        u1 = jnp.zeros((nblk, HEAD_DIM), F32)
        u2 = jnp.zeros((nblk, HEAD_DIM), F32)
        for l in range(S_CMP):
            zl = z_ref[0, :, l, :]
            u1 = u1 + jnp.dot(zl, w1_ref[l * HEAD_DIM:(l + 1) * HEAD_DIM, :], precision=HI,
                              preferred_element_type=F32)
            u2 = u2 + jnp.dot(zl, w1_ref[(S_CMP + l) * HEAD_DIM:(S_CMP + l + 1) * HEAD_DIM, :], precision=HI,
                              preferred_element_type=F32)
        pe = jnp.dot(jnp.broadcast_to(pe_ref[...], (8, L_CMP * HEAD_DIM)), w1_ref[...], precision=HI,
                     preferred_element_type=F32)[0:1]
        hmid = u1 + pltpu.roll(u2, nblk - 1, 0) + pe
        return jnp.dot(_silu(hmid), w2_ref[...], precision=HI, preferred_element_type=F32)

    kc_ref[0, 0] = compress(zk_ref, pek_ref, wk1_ref, wk2_ref)
    vc_ref[0, 0] = compress(zv_ref, pev_ref, wv1_ref, wv2_ref)


def nsa_compress(z3, pe_k, pe_v, wk1, wk2, wv1, wv2):
    bsz, seq, n = z3.shape
    nblk = seq // S_CMP
    z4 = z3.reshape(bsz, nblk, S_CMP, n)
    zspec = lambda cb: pl.BlockSpec((1, nblk, S_CMP, LANES), lambda b, g: (b, 0, 0, cb + g))
    wfull = lambda a: pl.BlockSpec(a.shape, lambda b, g: (0, 0))
    pek = pe_k.reshape(1, L_CMP * HEAD_DIM)
    pev = pe_v.reshape(1, L_CMP * HEAD_DIM)
    out = jax.ShapeDtypeStruct((bsz, G_NSA, nblk, HEAD_DIM), F32)
    ospec = pl.BlockSpec((1, 1, nblk, HEAD_DIM), lambda b, g: (b, g, 0, 0))
    return pl.pallas_call(
        functools.partial(_nsa_cmp_body, nblk=nblk),
        grid=(bsz, G_NSA),
        in_specs=[zspec(AB_NKV), zspec(AB_NKV + G_NSA), wfull(pek), wfull(pev),
                  wfull(wk1), wfull(wk2), wfull(wv1), wfull(wv2)],
        out_specs=[ospec, ospec],
        out_shape=[out, out],
        compiler_params=_cparams(("parallel", "parallel")),
        name="nsa_compress",
    )(z4, z4, pek, pev, wk1, wk2, wv1, wv2)


NSA_TQ = 128


def _nsa_expand(seq):
    m = np.arange(LANES)[:, None]
    t = np.arange(seq)[None, :]
    return jnp.asarray(((t // L_SLC) == m).astype(np.float32), dtype=BF16)


def _nsa_attn_body(q_ref, kc_ref, vc_ref, ks_ref, vs_ref, kw_ref, vw_ref, sm_ref, exp_ref, o_ref, *, seq):
    g = pl.program_id(1)
    qi = pl.program_id(2)
    tq = NSA_TQ
    rows = HPG_NSA * tq
    nblk = seq // S_CMP
    n_slc = seq // L_SLC
    n_top = min(N_SEL, n_slc)
    q0 = qi * tq

    qall = q_ref[0]
    qs = jnp.concatenate([qall[:, p * HEAD_DIM:(p + 1) * HEAD_DIM] for p in range(HPG_NSA)], axis=0)
    qs = qs * (HEAD_DIM ** -0.5)
    qb = qs.astype(BF16)
    tpos = q0 + lax.broadcasted_iota(jnp.int32, (rows, 1), 0) % tq

    kc = kc_ref[0, 0]
    vc = vc_ref[0, 0]
    sc = lax.dot_general(qs, kc, NT, precision=HI, preferred_element_type=F32)
    nidx = lax.broadcasted_iota(jnp.int32, (rows, nblk), 1)
    cmask = (nidx * S_CMP + (L_CMP - 1) <= tpos) & (nidx <= nblk - 2)
    scm = jnp.where(cmask, sc, NEG)
    mc = jnp.max(scm, -1, keepdims=True)
    ec = jnp.where(cmask, jnp.exp(scm - mc), 0.0)
    dc = jnp.sum(ec, -1, keepdims=True)
    p_cmp = ec / jnp.where(dc > 0, dc, 1.0)
    o_cmp = jnp.dot(p_cmp.astype(BF16), vc.astype(BF16), preferred_element_type=F32)

    psum = p_cmp[0:tq]
    for p in range(1, HPG_NSA):
        psum = psum + p_cmp[p * tq:(p + 1) * tq]
    cs = lax.broadcasted_iota(jnp.int32, (nblk, LANES), 0) * S_CMP
    ss = lax.broadcasted_iota(jnp.int32, (nblk, LANES), 1) * L_SLC
    overlap = ((cs < ss + L_SLC) & (cs + L_CMP > ss) & (cs <= seq - L_CMP) & (ss < seq)).astype(F32)
    imp = jnp.dot(psum, overlap, precision=HI, preferred_element_type=F32)
    blk = lax.broadcasted_iota(jnp.int32, (tq, LANES), 1)
    cur = (q0 + lax.broadcasted_iota(jnp.int32, (tq, LANES), 0)) // L_SLC
    valid = blk <= cur
    forced = (blk == 0) | (blk == cur) | (blk == cur - 1)
    score = jnp.where(valid, jnp.where(forced, FORCE_SCORE, imp), NEG)
    rank = jnp.zeros((tq, LANES), jnp.int32)
    for mp in range(n_slc):
        colv = score[:, mp:mp + 1]
        rank = rank + ((colv > score) | ((colv == score) & (mp < blk))).astype(jnp.int32)
    selb = (valid & (rank < n_top)).astype(F32).astype(BF16)

    tk = ATT_TK
    qrow = q0 + lax.broadcasted_iota(jnp.int32, (tq, tk), 0)
    col = lax.broadcasted_iota(jnp.int32, (tq, tk), 1)

    def branch(k_ref, v_ref, lo, hi, bias_fn):
        def qk(j):
            off = pl.multiple_of(j * tk, tk)
            return lax.dot_general(qb, k_ref[0, pl.ds(off, tk), :].astype(BF16), NT,
                                   preferred_element_type=F32)

        def step(j, s, carry):
            off = pl.multiple_of(j * tk, tk)
            vj = v_ref[0, pl.ds(off, tk), :].astype(BF16)
            bias = bias_fn(off)
            return _online_multi([s[p * tq:(p + 1) * tq] + bias for p in range(HPG_NSA)], vj, carry)

        return [acc / l for (_, l, acc) in _flash_loop(lo, hi, qk, step, _online_init(HPG_NSA))]

    def slc_bias(off):
        sel = jnp.dot(selb, exp_ref[:, pl.ds(off, tk)], preferred_element_type=F32)
        return jnp.where((sel > 0.5) & (off + col <= qrow), 0.0, NEG)

    last = (q0 + tq - 1) // tk
    o_slc = branch(ks_ref, vs_ref, 0, last + 1, slc_bias)

    def win_bias(off):
        d = qrow - (off + col)
        return jnp.where((d >= 0) & (d < WINDOW), 0.0, NEG)

    o_win = branch(kw_ref, vw_ref, jnp.maximum(q0 - (WINDOW - 1), 0) // tk, last + 1, win_bias)

    sg = jax.nn.sigmoid(sm_ref[0])
    for p in range(HPG_NSA):
        sl = slice(p * tq, (p + 1) * tq)
        base = 2 * N_HEADS + g * HPG_NSA + p
        o = (_lane_col(sg, blk, base) * o_cmp[sl] + _lane_col(sg, blk, base + N_HEADS) * o_slc[p]
             + _lane_col(sg, blk, base + 2 * N_HEADS) * o_win[p])
        o_ref[0, :, p * HEAD_DIM:(p + 1) * HEAD_DIM] = o.astype(BF16)


def nsa_attention(z3, kc, vc):
    bsz, seq, _ = z3.shape
    tq = NSA_TQ
    nblk = seq // S_CMP
    full = lambda cb: pl.BlockSpec((1, seq, LANES), lambda b, g, i: (b, 0, cb + g))
    cspec = pl.BlockSpec((1, 1, nblk, HEAD_DIM), lambda b, g, i: (b, g, 0, 0))
    qw = HPG_NSA * HEAD_DIM
    return pl.pallas_call(
        functools.partial(_nsa_attn_body, seq=seq),
        grid=(bsz, G_NSA, seq // tq),
        in_specs=[pl.BlockSpec((1, tq, qw), lambda b, g, i: (b, i, AB_NQ * LANES // qw + g)),
                  cspec, cspec,
                  full(AB_NKV + 2 * G_NSA), full(AB_NKV + 3 * G_NSA),
                  full(AB_NKV + 4 * G_NSA), full(AB_NKV + 5 * G_NSA),
                  pl.BlockSpec((1, tq, LANES), lambda b, g, i: (b, i, AB_SMALL)),
                  pl.BlockSpec((LANES, seq), lambda b, g, i: (0, 0))],
        out_specs=pl.BlockSpec((1, tq, qw), lambda b, g, i: (b, i, g)),
        out_shape=jax.ShapeDtypeStruct((bsz, seq, N_HEADS * HEAD_DIM), BF16),
        compiler_params=_cparams(("parallel", "parallel", "arbitrary")),
        name="nsa_attention",
    )(z3, kc, vc, z3, z3, z3, z3, z3, _nsa_expand(seq))


def _pad_cols(w, width):
    return jnp.pad(w, ((0, 0), (0, width - w.shape[1])))


def _ab_weight(w):
    small = _pad_cols(jnp.concatenate([w[:, 6144:6176], w[:, 13344:13392]], axis=1), SMALL_W)
    return jnp.concatenate([w[:, :6144], w[:, 6176:8224], w[:, 8224:10272], w[:, 10272:13344], small],
                           axis=1).astype(BF16)


def _cd_weight(w):
    return jnp.concatenate([w[:, :14336], _pad_cols(w[:, 14336:14352], SMALL_W)], axis=1).astype(BF16)


def _row128(v):
    return jnp.pad(v.astype(F32), (0, LANES - v.shape[0])).reshape(1, LANES)


def kernel(x, p, ab_norm_pre, ab_norm_post, ab_w_in, gdn_conv_w, gdn_a_log, gdn_dt_bias, gdn_norm, nsa_pe_k, nsa_pe_v, nsa_cmp_k1, nsa_cmp_k2, nsa_cmp_v1, nsa_cmp_v2, ab_w_out, cd_norm_pre, cd_norm_post, cd_w_in, hgrn_lb_logits, hgrn_norm, fox_f_bias, cd_w_out, ffn_norm_pre, ffn_norm_post, ffn_w_up, ffn_conv_w, ffn_conv_b, ffn_w_down, ple_w_proj, ple_gate_norm, ple_w_gate, ple_norm_post):
    bsz, seq, dm = x.shape
    depth = p.shape[0]
    m = bsz * seq
    half = N_HEADS * HEAD_DIM
    sm_ = jax.nn.softmax(hgrn_lb_logits.astype(F32), axis=0)
    lb_table = jnp.cumsum(sm_, axis=0) - sm_[0]
    xf = x.reshape(m, dm)
    pf = p.reshape(depth, m, -1)
    w_up, w_down = ffn_w_up.astype(BF16), ffn_w_down.astype(BF16)
    w_gate, w_proj = ple_w_gate.astype(BF16), ple_w_proj.astype(BF16)
    for li in range(depth):
        j = li // 2
        if li % 2 == 0:
            z3 = norm_matmul(xf, ab_norm_pre[j], _ab_weight(ab_w_in[j])).reshape(bsz, seq, AB_N)
            prm = jnp.concatenate([_row128(gdn_a_log[j]), _row128(gdn_dt_bias[j]),
                                   jnp.zeros((6, LANES), F32)], axis=0)
            o_a = gated_deltanet(z3, gdn_conv_w[j], prm, gdn_norm[j])
            kc, vc = nsa_compress(z3, nsa_pe_k[j], nsa_pe_v[j], nsa_cmp_k1[j], nsa_cmp_k2[j],
                                  nsa_cmp_v1[j], nsa_cmp_v2[j])
            o_b = nsa_attention(z3, kc, vc)
            w_out, post = ab_w_out[j], ab_norm_post[j]
        else:
            z3 = norm_matmul(xf, cd_norm_pre[j], _cd_weight(cd_w_in[j])).reshape(bsz, seq, CD_N)
            o_a = hgrn2(z3, lb_table[li], hgrn_norm[j])
            o_b = fox_attention(z3, fox_gates(z3, _row128(fox_f_bias[j])))
            w_out, post = cd_w_out[j], cd_norm_post[j]
        xf = outproj(o_a.reshape(m, half), o_b.reshape(m, half), w_out.astype(BF16), xf, post)
        xf = conv_ffn(xf, li, ffn_norm_pre, w_up, ffn_conv_w, ffn_conv_b, w_down, ffn_norm_post, seq)
        xf = ple(xf, li, pf, ple_gate_norm, w_gate, w_proj, ple_norm_post)
    return xf.reshape(bsz, seq, dm)
```

```python
import functools

import numpy as np
import jax
import jax.numpy as jnp
from jax import lax
from jax.experimental import pallas as pl
from jax.experimental.pallas import tpu as pltpu

F32 = jnp.float32
BF16 = jnp.bfloat16
HI = lax.Precision.HIGHEST
NT = (((1,), (1,)), ((), ()))
TN = (((0,), (0,)), ((), ()))

D_MODEL = 4096
HEAD_DIM = 128
N_HEADS = 16
G_NSA = 4
HPG_NSA = 4
L_CMP = 32
S_CMP = 16
L_SLC = 64
N_SEL = 8
WINDOW = 512
FORCE_SCORE = 1e4
GDN_CHUNK = 64
GDN_GROUP = 256
GDN_LOCKSTEP = 8
HGRN_CHUNK = 256
D_FF = 11008
EPS = 1e-6
NEG = -1e30
LANES = 128
SMALL_W = 512

AB_QKV, AB_GATE, AB_NQ, AB_NKV, AB_SMALL = 0, 48, 64, 80, 104
AB_N = 108 * LANES
CD_Q, CD_F, CD_I, CD_G, CD_FQKV, CD_SMALL = 0, 16, 32, 48, 64, 112
CD_N = 116 * LANES

VMEM_LIMIT = 56 * 1024 * 1024


def _cparams(sem):
    return pltpu.CompilerParams(dimension_semantics=sem, vmem_limit_bytes=VMEM_LIMIT)


def _rms(x, w):
    return x * lax.rsqrt(jnp.mean(x * x, axis=-1, keepdims=True) + EPS) * w


def _silu(x):
    return x * jax.nn.sigmoid(x)


def _lane_col(x, lane_idx, lane):
    return jnp.sum(jnp.where(lane_idx == lane, x, 0.0), axis=-1, keepdims=True)


def _split2(x):
    hi = x.astype(BF16)
    return hi, (x - hi.astype(F32)).astype(BF16)


def _dot_sel(sel, x):
    n = x.shape[1]
    hi = x.astype(BF16)
    r1 = x - hi.astype(F32)
    mid = r1.astype(BF16)
    lo = (r1 - mid.astype(F32)).astype(BF16)
    y = jnp.dot(sel, jnp.concatenate([hi, mid, lo], axis=1), preferred_element_type=F32)
    return (y[:, :n] + y[:, n:2 * n]) + y[:, 2 * n:]


def _norm_matmul_body(x_ref, nw_ref, w_ref, o_ref, h_ref):
    @pl.when(pl.program_id(1) == 0)
    def _():
        h_ref[...] = _rms(x_ref[...], nw_ref[...]).astype(BF16)

    o_ref[...] = jnp.dot(h_ref[...], w_ref[...], preferred_element_type=F32)


def norm_matmul(x, nw, w, tm=512, tn=512):
    m, k = x.shape
    n = w.shape[1]
    return pl.pallas_call(
        _norm_matmul_body,
        grid=(m // tm, n // tn),
        in_specs=[pl.BlockSpec((tm, k), lambda i, j: (i, 0)),
                  pl.BlockSpec((1, k), lambda i, j: (0, 0)),
                  pl.BlockSpec((k, tn), lambda i, j: (0, j))],
        out_specs=pl.BlockSpec((tm, tn), lambda i, j: (i, j)),
        out_shape=jax.ShapeDtypeStruct((m, n), F32),
        scratch_shapes=[pltpu.VMEM((tm, k), BF16)],
        compiler_params=_cparams(("parallel", "arbitrary")),
        name="norm_matmul",
    )(x, nw.reshape(1, k), w)


def _outproj_body(oa_ref, ob_ref, wa_ref, wb_ref, x_ref, nw_ref, o_ref, *, tn):
    j = pl.program_id(1)
    y = jnp.dot(oa_ref[...], wa_ref[...], preferred_element_type=F32)
    y = y + jnp.dot(ob_ref[...], wb_ref[...], preferred_element_type=F32)
    o_ref[:, pl.ds(pl.multiple_of(j * tn, tn), tn)] = y

    @pl.when(j == pl.num_programs(1) - 1)
    def _():
        o_ref[...] = x_ref[...] + _rms(o_ref[...], nw_ref[...])


def outproj(oa, ob, w, x, nw, tm=512, tn=512):
    m, ka = oa.shape
    n = w.shape[1]
    return pl.pallas_call(
        functools.partial(_outproj_body, tn=tn),
        grid=(m // tm, n // tn),
        in_specs=[pl.BlockSpec((tm, ka), lambda i, j: (i, 0)),
                  pl.BlockSpec((tm, ka), lambda i, j: (i, 0)),
                  pl.BlockSpec((ka, tn), lambda i, j: (0, j)),
                  pl.BlockSpec((ka, tn), lambda i, j: (1, j)),
                  pl.BlockSpec((tm, n), lambda i, j: (i, 0), pipeline_mode=pl.Buffered(1)),
                  pl.BlockSpec((1, n), lambda i, j: (0, 0))],
        out_specs=pl.BlockSpec((tm, n), lambda i, j: (i, 0)),
        out_shape=jax.ShapeDtypeStruct((m, n), F32),
        compiler_params=_cparams(("parallel", "arbitrary")),
        name="outproj",
    )(oa, ob, w, w, x, nw.reshape(1, n))


FFN_HALO = 16


def _ffn_body(x_ref, xh_ref, nw_ref, wg_ref, wu_ref, cwg_ref, cwu_ref, cbg_ref, cbu_ref,
              wd_ref, pw_ref, o_ref, h_s, *, tm, seq):
    i = pl.program_id(0)
    j = pl.program_id(1)

    @pl.when(j == 0)
    def _():
        keep = jnp.where((i * tm) % seq == 0, 0.0, 1.0)
        h_s[0:FFN_HALO, :] = (_rms(xh_ref[...], nw_ref[...]) * keep).astype(BF16)
        h_s[FFN_HALO:, :] = _rms(x_ref[...], nw_ref[...]).astype(BF16)
        o_ref[...] = jnp.zeros_like(o_ref)

    h = h_s[...]

    def branch(w_ref, cw_ref, cb_ref):
        u = jnp.dot(h, w_ref[...], preferred_element_type=F32)
        cw = cw_ref[...]
        y = u * cw[2:3] + pltpu.roll(u, 1, 0) * cw[1:2] + pltpu.roll(u, 2, 0) * cw[0:1]
        return y[FFN_HALO:] + cb_ref[...]

    g = branch(wg_ref, cwg_ref, cbg_ref)
    u = branch(wu_ref, cwu_ref, cbu_ref)
    act = (_silu(g) * u).astype(BF16)
    o_ref[...] += jnp.dot(act, wd_ref[...], preferred_element_type=F32)

    @pl.when(j == pl.num_programs(1) - 1)
    def _():
        o_ref[...] = x_ref[...] + _rms(o_ref[...], pw_ref[...])


def conv_ffn(x, li, nw, w_up, conv_w, conv_b, w_down, pw, seq, tm=512, tf=256):
    m, k = x.shape
    nl, f, _ = w_down.shape
    nf = f // tf
    hb = tm // FFN_HALO
    cb = conv_b.reshape(nl, 1, 2 * f)
    row = pl.BlockSpec((None, 1, k), lambda i, j: (li, 0, 0))
    return pl.pallas_call(
        functools.partial(_ffn_body, tm=tm, seq=seq),
        grid=(m // tm, nf),
        in_specs=[pl.BlockSpec((tm, k), lambda i, j: (i, 0), pipeline_mode=pl.Buffered(1)),
                  pl.BlockSpec((FFN_HALO, k), lambda i, j: (jnp.maximum(i * hb - 1, 0), 0)),
                  row,
                  pl.BlockSpec((None, k, tf), lambda i, j: (li, 0, j)),
                  pl.BlockSpec((None, k, tf), lambda i, j: (li, 0, nf + j)),
                  pl.BlockSpec((None, 3, tf), lambda i, j: (li, 0, j)),
                  pl.BlockSpec((None, 3, tf), lambda i, j: (li, 0, nf + j)),
                  pl.BlockSpec((None, 1, tf), lambda i, j: (li, 0, j)),
                  pl.BlockSpec((None, 1, tf), lambda i, j: (li, 0, nf + j)),
                  pl.BlockSpec((None, tf, k), lambda i, j: (li, j, 0)),
                  row],
        out_specs=pl.BlockSpec((tm, k), lambda i, j: (i, 0)),
        out_shape=jax.ShapeDtypeStruct((m, k), F32),
        scratch_shapes=[pltpu.VMEM((tm + FFN_HALO, k), BF16)],
        compiler_params=_cparams(("parallel", "arbitrary")),
        name="conv_ffn",
    )(x, x, nw.reshape(nl, 1, k), w_up, w_up, conv_w, conv_w, cb, cb, w_down, pw.reshape(nl, 1, k))


def _ple_body(x_ref, p_ref, gnw_ref, wg_ref, wp_ref, pnw_ref, o_ref, h_s, *, tn):
    j = pl.program_id(1)

    @pl.when(j == 0)
    def _():
        h_s[...] = _rms(x_ref[...], gnw_ref[...]).astype(BF16)

    gate = jax.nn.sigmoid(jnp.dot(h_s[...], wg_ref[...], preferred_element_type=F32))
    proj = jnp.dot(p_ref[...].astype(BF16), wp_ref[...], preferred_element_type=F32)
    o_ref[:, pl.ds(pl.multiple_of(j * tn, tn), tn)] = gate * proj

    @pl.when(j == pl.num_programs(1) - 1)
    def _():
        o_ref[...] = x_ref[...] + _rms(o_ref[...], pnw_ref[...])


def ple(x, li, p, gnw, wg, wp, pnw, tm=512, tn=512):
    m, k = x.shape
    nl, _, dp = p.shape
    row = pl.BlockSpec((None, 1, k), lambda i, j: (li, 0, 0))
    return pl.pallas_call(
        functools.partial(_ple_body, tn=tn),
        grid=(m // tm, k // tn),
        in_specs=[pl.BlockSpec((tm, k), lambda i, j: (i, 0), pipeline_mode=pl.Buffered(1)),
                  pl.BlockSpec((None, tm, dp), lambda i, j: (li, i, 0)),
                  row,
                  pl.BlockSpec((None, k, tn), lambda i, j: (li, 0, j)),
                  pl.BlockSpec((None, dp, tn), lambda i, j: (li, 0, j)),
                  row],
        out_specs=pl.BlockSpec((tm, k), lambda i, j: (i, 0)),
        out_shape=jax.ShapeDtypeStruct((m, k), F32),
        scratch_shapes=[pltpu.VMEM((tm, k), BF16)],
        compiler_params=_cparams(("parallel", "arbitrary")),
        name="ple",
    )(x, p, gnw.reshape(nl, 1, k), wg, wp, pnw.reshape(nl, 1, k))


def _gdn_ltri():
    r = np.arange(GDN_GROUP)[:, None]
    t = np.arange(GDN_GROUP)[None, :]
    return jnp.asarray((((r // GDN_CHUNK) == (t // GDN_CHUNK)) & (r >= t)).astype(np.float32), dtype=BF16)


def _gdn_body(zq_ref, zk_ref, zv_ref, zg_ref, sm_ref, cwq_ref, cwk_ref, cwv_ref, prm_ref, nw_ref, ltri_ref,
              o_ref, q_s, k_s, v_s, g_s, b_s, o0_s, n0_s, gl_s, qe_s, m_s, *, seq):
    hd = pl.program_id(1)
    c = GDN_CHUNK
    gs = GDN_GROUP
    row = lax.broadcasted_iota(jnp.int32, (seq, LANES), 0)
    lane = lax.broadcasted_iota(jnp.int32, (seq, LANES), 1)

    def conv_silu(z_ref, w_ref):
        z = z_ref[0]
        w = w_ref[...]
        y = z * w[3:4]
        for s in (1, 2, 3):
            y = y + jnp.where(row >= s, pltpu.roll(z, s, 0), 0.0) * w[3 - s:4 - s]
        return _silu(y)

    q = conv_silu(zq_ref, cwq_ref)
    k = conv_silu(zk_ref, cwk_ref)
    q_s[...] = q * lax.rsqrt(jnp.sum(q * q, -1, keepdims=True) + EPS) * (HEAD_DIM ** -0.5)
    k_s[...] = k * lax.rsqrt(jnp.sum(k * k, -1, keepdims=True) + EPS)
    v_s[...] = conv_silu(zv_ref, cwv_ref)

    sm = sm_ref[0]
    glog = -jnp.exp(prm_ref[0:1, :]) * jax.nn.softplus(sm + prm_ref[1:2, :])
    g_s[...] = jnp.broadcast_to(_lane_col(glog, lane, hd), (seq, LANES))
    b_s[...] = jnp.broadcast_to(_lane_col(jax.nn.sigmoid(sm), lane, N_HEADS + hd), (seq, LANES))

    nper = gs // c
    rs = lax.broadcasted_iota(jnp.int32, (c, gs), 0)
    cs = lax.broadcasted_iota(jnp.int32, (c, gs), 1)
    jj = cs % c
    cblk = cs // c
    incl = rs >= jj
    strict = rs > jj
    eye_side = (rs == jj).astype(F32)
    r2 = lax.broadcasted_iota(jnp.int32, (gs, gs), 0)
    c2 = lax.broadcasted_iota(jnp.int32, (gs, gs), 1)
    bdm = ((r2 // c) == (c2 // c)).astype(F32)
    bdm_b = bdm.astype(BF16)
    lane_c = lax.broadcasted_iota(jnp.int32, (c, LANES), 1)
    lane_g = lax.broadcasted_iota(jnp.int32, (gs, LANES), 1)
    ones3 = (lane_c < 3).astype(F32).astype(BF16)

    def side_col(xb):
        left = jnp.where(lane_c < c, xb[0:c], xb[c:2 * c])
        right = jnp.where(lane_c < c, xb[2 * c:3 * c], xb[3 * c:4 * c])
        return jnp.concatenate([left, right], axis=1)

    def to_side(full):
        out = full[0:c]
        for ci in range(1, nper):
            out = jnp.where(cblk == ci, full[ci * c:(ci + 1) * c], out)
        return out

    def bd(xs, mask):
        return jnp.concatenate([xs] * nper, axis=0) * mask

    def mm3(ph, pl_, xh, xl):
        bh = bd(xh, bdm_b)
        lhs = jnp.concatenate([ph, ph, pl_], axis=1)
        rhs = jnp.concatenate([bh, bd(xl, bdm_b), bh], axis=0)
        return jnp.dot(lhs, rhs, preferred_element_type=F32)

    def group_pre(gi):
        off = pl.multiple_of(gi * gs, gs)
        qg = q_s[pl.ds(off, gs), :]
        kg = k_s[pl.ds(off, gs), :]
        bg = b_s[pl.ds(off, gs), :]
        gam = _dot_sel(ltri_ref[...], g_s[pl.ds(off, gs), :])
        g_hi = gam.astype(BF16).astype(F32)
        g_mid = (gam - g_hi).astype(BF16).astype(F32)
        g_lo = gam - g_hi - g_mid
        gam3 = jnp.where(lane_g == 0, g_hi, jnp.where(lane_g == 1, g_mid, jnp.where(lane_g == 2, g_lo, 0.0)))
        gam_row = lax.dot_general(ones3, gam3.astype(BF16), NT, preferred_element_type=F32)
        dec = jnp.where(incl, jnp.exp(jnp.where(incl, side_col(gam) - gam_row, 0.0)), 0.0)
        qb = qg.astype(BF16)
        kb = kg.astype(BF16)
        kk = to_side(lax.dot_general(kb, kb, NT, preferred_element_type=F32))
        a = jnp.where(strict, side_col(bg) * dec * kk, 0.0)
        return dict(off=off, qg=qg, kg=kg, bg=bg, gam=gam, dec=dec, qb=qb, kb=kb, a=a)

    def group_post(gi, st, tinv):
        off, qg, kg, bg, gam, dec, qb, kb = (st[n] for n in ("off", "qg", "kg", "bg", "gam", "dec", "qb", "kb"))
        vg = v_s[pl.ds(off, gs), :]
        th, tl = _split2(tinv)
        eg = jnp.exp(gam)
        rh, rl = _split2(jnp.concatenate([vg * bg, kg * (bg * eg)], axis=1))
        bth = bd(th, bdm_b)
        sol = jnp.dot(jnp.concatenate([bth, bth, bd(tl, bdm_b)], axis=1),
                      jnp.concatenate([rh, rl, rh], axis=0), preferred_element_type=F32)
        solb = sol.astype(BF16)
        qk = bd(dec, bdm) * lax.dot_general(qb, kb, NT, preferred_element_type=F32)
        qkuw = jnp.dot(qk.astype(BF16), solb, preferred_element_type=F32)
        o0_s[pl.ds(off, gs), :] = qkuw[:, :HEAD_DIM]
        qe_s[pl.ds(off, gs), :] = (qg * eg - qkuw[:, HEAD_DIM:]).astype(BF16)
        for ci in range(nper):
            lo = ci * c
            gl = gam[lo + c - 1:lo + c, :]
            kd = (kg[lo:lo + c] * jnp.exp(gl - gam[lo:lo + c])).astype(BF16)
            kds = lax.dot_general(kd, solb[lo:lo + c], TN, preferred_element_type=F32)
            so = pl.multiple_of((gi * nper + ci) * HEAD_DIM, HEAD_DIM)
            n0_s[pl.ds(so, HEAD_DIM), :] = kds[:, :HEAD_DIM]
            m_s[pl.ds(so, HEAD_DIM), :] = (-kds[:, HEAD_DIM:]).astype(BF16)
            gl_s[pl.ds(pl.multiple_of((gi * nper + ci) * 8, 8), 8), :] = jnp.broadcast_to(jnp.exp(gl), (8, LANES))

    ngroups = seq // gs
    nlock = GDN_LOCKSTEP if ngroups % GDN_LOCKSTEP == 0 else 1

    def groups(it, carry):
        gis = [it + u * (ngroups // nlock) for u in range(nlock)]
        sts = [group_pre(gi) for gi in gis]
        tinvs = [eye_side - st["a"] for st in sts]
        xs = [_split2(st["a"]) for st in sts]
        for _ in range(5):
            xs = [_split2(mm3(xh, xl, xh, xl)) for xh, xl in xs]
            ts = [_split2(t) for t in tinvs]
            tinvs = [t + mm3(th, tl, xh, xl) for t, (th, tl), (xh, xl) in zip(tinvs, ts, xs)]
        for gi, st, t in zip(gis, sts, tinvs):
            group_post(gi, st, t)
        return carry

    lax.fori_loop(0, ngroups // nlock, groups, 0)

    def step(n, s):
        off = pl.multiple_of(n * c, c)
        so = pl.multiple_of(n * HEAD_DIM, HEAD_DIM)
        sb = s.astype(BF16)
        o0_s[pl.ds(off, c), :] = o0_s[pl.ds(off, c), :] + jnp.dot(
            qe_s[pl.ds(off, c), :], sb, preferred_element_type=F32)
        return (gl_s[pl.ds(pl.multiple_of(n * 8, 8), 1), :] * s + n0_s[pl.ds(so, HEAD_DIM), :]
                + jnp.dot(m_s[pl.ds(so, HEAD_DIM), :], sb, preferred_element_type=F32))

    lax.fori_loop(0, seq // c, step, jnp.zeros((HEAD_DIM, HEAD_DIM), F32))
    o_ref[0] = (_rms(o0_s[...], nw_ref[...]) * _silu(zg_ref[0])).astype(BF16)


def gated_deltanet(z3, conv_w, prm, norm_w):
    bsz, seq, _ = z3.shape
    zspec = lambda cb: pl.BlockSpec((1, seq, LANES), lambda b, h: (b, 0, cb + h))
    wspec = lambda cb: pl.BlockSpec((4, LANES), lambda b, h: (0, cb + h))
    nchunk = seq // GDN_CHUNK
    return pl.pallas_call(
        functools.partial(_gdn_body, seq=seq),
        grid=(bsz, N_HEADS),
        in_specs=[zspec(AB_QKV), zspec(AB_QKV + 16), zspec(AB_QKV + 32), zspec(AB_GATE),
                  pl.BlockSpec((1, seq, LANES), lambda b, h: (b, 0, AB_SMALL)),
                  wspec(0), wspec(16), wspec(32),
                  pl.BlockSpec((8, LANES), lambda b, h: (0, 0)),
                  pl.BlockSpec((1, LANES), lambda b, h: (0, 0)),
                  pl.BlockSpec((GDN_GROUP, GDN_GROUP), lambda b, h: (0, 0))],
        out_specs=pl.BlockSpec((1, seq, LANES), lambda b, h: (b, 0, h)),
        out_shape=jax.ShapeDtypeStruct((bsz, seq, N_HEADS * HEAD_DIM), BF16),
        scratch_shapes=[pltpu.VMEM((seq, LANES), F32) for _ in range(6)]
        + [pltpu.VMEM((nchunk * HEAD_DIM, HEAD_DIM), F32),
           pltpu.VMEM((nchunk * 8, LANES), F32),
           pltpu.VMEM((seq, LANES), BF16),
           pltpu.VMEM((nchunk * HEAD_DIM, HEAD_DIM), BF16)],
        compiler_params=_cparams(("parallel", "parallel")),
        name="gated_deltanet",
    )(z3, z3, z3, z3, z3, conv_w, conv_w, conv_w, prm, norm_w.reshape(1, HEAD_DIM), _gdn_ltri())


HGRN_LEVELS = tuple(HGRN_CHUNK >> (i + 1) for i in range(8))


def _hgrn_selectors():
    c = HGRN_CHUNK
    r = np.arange(c)[:, None]
    t = np.arange(c)[None, :]
    mats = [r >= t]
    for s in HGRN_LEVELS:
        isq = ((r // s) % 2) == 1
        mats.append(((r // s) == (t // s)) & ((isq & (t <= r)) | (~isq & (t > r))))
    return jnp.asarray(np.concatenate(mats, axis=0).astype(np.float32), dtype=BF16)


def _hgrn_body(zq_ref, zf_ref, zi_ref, zg_ref, lb_ref, nw_ref, sel_ref, o_ref, *, seq):
    c = HGRN_CHUNK
    r = lax.broadcasted_iota(jnp.int32, (c, c), 0)
    cc = lax.broadcasted_iota(jnp.int32, (c, c), 1)
    diag = r == cc
    x = r ^ cc
    lev = jnp.zeros((c, c), jnp.int32)
    for bit in range(1, 8):
        lev = lev + (x >= (1 << bit)).astype(jnp.int32)
    lev = jnp.where(r > cc, lev, -1)
    rr = lax.broadcasted_iota(jnp.int32, (c, LANES), 0)
    lb = lb_ref[...]

    def chunk(n, st):
        off = pl.multiple_of(n * c, c)
        fx = zf_ref[0, pl.ds(off, c), :]
        q = _silu(zq_ref[0, pl.ds(off, c), :])
        vb = zi_ref[0, pl.ds(off, c), :].astype(BF16)
        lf = jnp.log(lb + (1.0 - lb) * jax.nn.sigmoid(fx))
        k = (1.0 - lb) * jax.nn.sigmoid(-fx)
        sums = _dot_sel(sel_ref[...], lf)
        b = sums[0:c]
        att = jnp.where(diag, jnp.sum(q * k, -1, keepdims=True), 0.0)
        for i, s in enumerate(HGRN_LEVELS):
            e = jnp.exp(sums[(i + 1) * c:(i + 2) * c])
            isq = ((rr // s) % 2) == 1
            qs = jnp.where(isq, q * e, 0.0).astype(BF16)
            ks = jnp.where(isq, 0.0, k * e).astype(BF16)
            part = lax.dot_general(qs, ks, NT, preferred_element_type=F32)
            att = jnp.where(lev == (7 - i), part, att)
        bl = b[c - 1:c, :]
        o = lax.dot_general((q * jnp.exp(b)).astype(BF16), st.astype(BF16), NT, preferred_element_type=F32)
        o = o + jnp.dot(att.astype(BF16), vb, preferred_element_type=F32)
        o = _rms(o, nw_ref[...]) * _silu(zg_ref[0, pl.ds(off, c), :])
        o_ref[0, pl.ds(off, c), :] = o.astype(BF16)
        kd = (k * jnp.exp(bl - b)).astype(BF16)
        return st * jnp.exp(bl) + lax.dot_general(vb, kd, TN, preferred_element_type=F32)

    lax.fori_loop(0, seq // c, chunk, jnp.zeros((HEAD_DIM, HEAD_DIM), F32))


def hgrn2(z3, lb, norm_w):
    bsz, seq, _ = z3.shape
    zspec = lambda cb: pl.BlockSpec((1, seq, LANES), lambda b, h: (b, 0, cb + h))
    sel = _hgrn_selectors()
    return pl.pallas_call(
        functools.partial(_hgrn_body, seq=seq),
        grid=(bsz, N_HEADS),
        in_specs=[zspec(CD_Q), zspec(CD_F), zspec(CD_I), zspec(CD_G),
                  pl.BlockSpec((1, LANES), lambda b, h: (0, h)),
                  pl.BlockSpec((1, LANES), lambda b, h: (0, 0)),
                  pl.BlockSpec(sel.shape, lambda b, h: (0, 0))],
        out_specs=pl.BlockSpec((1, seq, LANES), lambda b, h: (b, 0, h)),
        out_shape=jax.ShapeDtypeStruct((bsz, seq, N_HEADS * HEAD_DIM), BF16),
        compiler_params=_cparams(("parallel", "parallel")),
        name="hgrn2",
    )(z3, z3, z3, z3, lb.reshape(1, N_HEADS * HEAD_DIM), norm_w.reshape(1, HEAD_DIM), sel)


FOX_BLK = 256


def _fox_gate_body(sm_ref, bias_ref, cumt_ref, *, seq):
    c = FOX_BLK
    r = lax.broadcasted_iota(jnp.int32, (c, c), 0)
    cc = lax.broadcasted_iota(jnp.int32, (c, c), 1)
    ltri = (r >= cc).astype(F32)
    r1 = lax.broadcasted_iota(jnp.int32, (LANES, LANES), 0)
    c1 = lax.broadcasted_iota(jnp.int32, (LANES, LANES), 1)
    eye = (r1 == c1).astype(F32)

    def blk(n, carry):
        off = pl.multiple_of(n * c, c)
        ls = jax.nn.log_sigmoid(sm_ref[0, pl.ds(off, c), :] + bias_ref[...])
        cum = carry + jnp.dot(ltri, ls, precision=HI, preferred_element_type=F32)
        cumt_ref[0, :, pl.ds(off, c)] = lax.dot_general(eye, cum, NT, precision=HI, preferred_element_type=F32)
        return cum[c - 1:c, :]

    lax.fori_loop(0, seq // c, blk, jnp.zeros((1, LANES), F32))


def fox_gates(z3, bias_row):
    bsz, seq, _ = z3.shape
    return pl.pallas_call(
        functools.partial(_fox_gate_body, seq=seq),
        grid=(bsz,),
        in_specs=[pl.BlockSpec((1, seq, LANES), lambda b: (b, 0, CD_SMALL)),
                  pl.BlockSpec((1, LANES), lambda b: (0, 0))],
        out_specs=pl.BlockSpec((1, LANES, seq), lambda b: (b, 0, 0)),
        out_shape=jax.ShapeDtypeStruct((bsz, LANES, seq), F32),
        compiler_params=_cparams(("parallel",)),
        name="fox_gates",
    )(z3, bias_row)


ATT_ROWS = 128
ATT_TK = 256


def _for_blocks(n, unroll, body, carry=None):
    if isinstance(n, int):
        for j in range(n):
            carry = body(j, carry)
        return carry

    def it(i, c):
        for u in range(unroll):
            c = body(i * unroll + u, c)
        return c

    return lax.fori_loop(0, n // unroll, it, carry)


def _softmax_pv(s_scr, nblk, unroll, v_fn, m_scr, l_scr, acc_scr):
    _, rows, bw = s_scr.shape
    nlb = bw // LANES
    for ch in range(rows // ATT_ROWS):
        rs = slice(ch * ATT_ROWS, (ch + 1) * ATT_ROWS)

        def pmax(j, mel):
            for b in range(nlb):
                mel = jnp.maximum(mel, s_scr[j, rs, b * LANES:(b + 1) * LANES])
            return mel

        mel = _for_blocks(nblk, unroll, pmax, jnp.full((ATT_ROWS, LANES), NEG, F32))
        m_scr[rs, :] = jnp.broadcast_to(jnp.max(mel, -1, keepdims=True), (ATT_ROWS, LANES))
    l_scr[...] = jnp.zeros_like(l_scr)
    acc_scr[...] = jnp.zeros_like(acc_scr)

    def ppv(j, c):
        pcs = []
        for ch in range(rows // ATT_ROWS):
            rs = slice(ch * ATT_ROWS, (ch + 1) * ATT_ROWS)
            mb = m_scr[rs, :]
            lacc = l_scr[rs, :]
            ps = []
            for b in range(nlb):
                pb = jnp.exp(s_scr[j, rs, b * LANES:(b + 1) * LANES] - mb)
                lacc = lacc + pb
                ps.append(pb.astype(BF16))
            l_scr[rs, :] = lacc
            pcs.append(jnp.concatenate(ps, axis=1))
        acc_scr[...] += jnp.dot(jnp.concatenate(pcs, axis=0), v_fn(j), preferred_element_type=F32)
        return c

    _for_blocks(nblk, unroll, ppv)
    return acc_scr[...] / jnp.sum(l_scr[...], -1, keepdims=True)


def _softmax_scratch(rows, nblk, bw):
    return [pltpu.VMEM((nblk, rows, bw), F32), pltpu.VMEM((rows, LANES), F32),
            pltpu.VMEM((rows, LANES), F32), pltpu.VMEM((rows, HEAD_DIM), F32)]


def _fox_attn_body(q_ref, k_ref, v_ref, cumt_ref, o_ref, s_scr, m_scr, l_scr, acc_scr, *, tq):
    hd = pl.program_id(1)
    qi = pl.program_id(2)
    qall = (q_ref[0] * (HEAD_DIM ** -0.5)).astype(BF16)

    def scores(j):
        off = pl.multiple_of(j * tq, tq)
        kj = k_ref[0, pl.ds(off, tq), :].astype(BF16)
        ck = cumt_ref[0, pl.ds(hd % 8, 1), pl.ds(off, tq)]
        return lax.dot_general(qall, kj, NT, preferred_element_type=F32) - ck

    row = lax.broadcasted_iota(jnp.int32, (tq, tq), 0)
    col = lax.broadcasted_iota(jnp.int32, (tq, tq), 1)

    def v_blk(j):
        return v_ref[0, pl.ds(pl.multiple_of(j * tq, tq), tq), :].astype(BF16)

    def tile(nfull):
        for j in range(nfull):
            s_scr[j] = scores(j)
        s_scr[nfull] = jnp.where(col <= row, scores(nfull), NEG)
        o_ref[0] = _softmax_pv(s_scr, nfull + 1, 1, v_blk, m_scr, l_scr, acc_scr).astype(BF16)

    lax.switch(qi, [functools.partial(tile, n) for n in range(s_scr.shape[0])])


def fox_attention(z3, cumt):
    bsz, seq, _ = z3.shape
    tq = min(512, seq)
    full = lambda cb: pl.BlockSpec((1, seq, LANES), lambda b, h, i: (b, 0, cb + h))
    return pl.pallas_call(
        functools.partial(_fox_attn_body, tq=tq),
        grid=(bsz, N_HEADS, seq // tq),
        in_specs=[pl.BlockSpec((1, tq, LANES), lambda b, h, i: (b, i, CD_FQKV + h)),
                  full(CD_FQKV + 16), full(CD_FQKV + 32),
                  pl.BlockSpec((1, 8, seq), lambda b, h, i: (b, h // 8, 0))],
        out_specs=pl.BlockSpec((1, tq, LANES), lambda b, h, i: (b, i, h)),
        out_shape=jax.ShapeDtypeStruct((bsz, seq, N_HEADS * HEAD_DIM), BF16),
        scratch_shapes=_softmax_scratch(tq, seq // tq, tq),
        compiler_params=_cparams(("parallel", "parallel", "arbitrary")),
        name="fox_attention",
    )(z3, z3, z3, cumt)


def _nsa_cmp_body(zk_ref, zv_ref, pek_ref, pev_ref, wk1_ref, wk2_ref, wv1_ref, wv2_ref, kc_ref, vc_ref, *, nblk):
    def compress(z_ref, pe_ref, w1_ref, w2_ref):
        u1 = jnp.zeros((nblk, HEAD_DIM), F32)
        u2 = jnp.zeros((nblk, HEAD_DIM), F32)
        for l in range(S_CMP):
            zl = z_ref[0, :, l, :]
            u1 = u1 + jnp.dot(zl, w1_ref[l * HEAD_DIM:(l + 1) * HEAD_DIM, :], precision=HI,
                              preferred_element_type=F32)
            u2 = u2 + jnp.dot(zl, w1_ref[(S_CMP + l) * HEAD_DIM:(S_CMP + l + 1) * HEAD_DIM, :], precision=HI,
                              preferred_element_type=F32)
        pe = jnp.dot(jnp.broadcast_to(pe_ref[...], (8, L_CMP * HEAD_DIM)), w1_ref[...], precision=HI,
                     preferred_element_type=F32)[0:1]
        hmid = u1 + pltpu.roll(u2, nblk - 1, 0) + pe
        return jnp.dot(_silu(hmid), w2_ref[...], precision=HI, preferred_element_type=F32)

    kc_ref[0, 0] = compress(zk_ref, pek_ref, wk1_ref, wk2_ref)
    vc_ref[0, 0] = compress(zv_ref, pev_ref, wv1_ref, wv2_ref)


def nsa_compress(z3, pe_k, pe_v, wk1, wk2, wv1, wv2):
    bsz, seq, n = z3.shape
    nblk = seq // S_CMP
    z4 = z3.reshape(bsz, nblk, S_CMP, n)
    zspec = lambda cb: pl.BlockSpec((1, nblk, S_CMP, LANES), lambda b, g: (b, 0, 0, cb + g))
    wfull = lambda a: pl.BlockSpec(a.shape, lambda b, g: (0, 0))
    pek = pe_k.reshape(1, L_CMP * HEAD_DIM)
    pev = pe_v.reshape(1, L_CMP * HEAD_DIM)
    out = jax.ShapeDtypeStruct((bsz, G_NSA, nblk, HEAD_DIM), F32)
    ospec = pl.BlockSpec((1, 1, nblk, HEAD_DIM), lambda b, g: (b, g, 0, 0))
    return pl.pallas_call(
        functools.partial(_nsa_cmp_body, nblk=nblk),
        grid=(bsz, G_NSA),
        in_specs=[zspec(AB_NKV), zspec(AB_NKV + G_NSA), wfull(pek), wfull(pev),
                  wfull(wk1), wfull(wk2), wfull(wv1), wfull(wv2)],
        out_specs=[ospec, ospec],
        out_shape=[out, out],
        compiler_params=_cparams(("parallel", "parallel")),
        name="nsa_compress",
    )(z4, z4, pek, pev, wk1, wk2, wv1, wv2)


NSA_TQ = 128


def _nsa_expand(seq):
    m = np.arange(LANES)[:, None]
    t = np.arange(seq)[None, :]
    return jnp.asarray(((t // L_SLC) == m).astype(np.float32), dtype=BF16)


NSA_UNROLL = 2
NSA_WIN_BLOCKS = max(e // ATT_TK - max(e - (WINDOW + NSA_TQ - 2), 0) // ATT_TK + 1
                     for e in range(NSA_TQ - 1, 8 * WINDOW, NSA_TQ))


def _nsa_attn_body(q_ref, kc_ref, vc_ref, ks_ref, vs_ref, kw_ref, vw_ref, sm_ref, exp_ref, o_ref,
                   s_scr, m_scr, l_scr, acc_scr, *, seq):
    g = pl.program_id(1)
    qi = pl.program_id(2)
    tq = NSA_TQ
    rows = HPG_NSA * tq
    nblk = seq // S_CMP
    n_slc = seq // L_SLC
    n_top = min(N_SEL, n_slc)
    q0 = qi * tq

    qall = q_ref[0]
    qs = jnp.concatenate([qall[:, p * HEAD_DIM:(p + 1) * HEAD_DIM] for p in range(HPG_NSA)], axis=0)
    qs = qs * (HEAD_DIM ** -0.5)
    qb = qs.astype(BF16)
    tpos = q0 + lax.broadcasted_iota(jnp.int32, (rows, 1), 0) % tq

    kc = kc_ref[0, 0]
    vc = vc_ref[0, 0]
    sc = lax.dot_general(qs, kc, NT, precision=HI, preferred_element_type=F32)
    nidx = lax.broadcasted_iota(jnp.int32, (rows, nblk), 1)
    cmask = (nidx * S_CMP + (L_CMP - 1) <= tpos) & (nidx <= nblk - 2)
    scm = jnp.where(cmask, sc, NEG)
    mc = jnp.max(scm, -1, keepdims=True)
    ec = jnp.where(cmask, jnp.exp(scm - mc), 0.0)
    dc = jnp.sum(ec, -1, keepdims=True)
    p_cmp = ec / jnp.where(dc > 0, dc, 1.0)
    o_cmp = jnp.dot(p_cmp.astype(BF16), vc.astype(BF16), preferred_element_type=F32)

    psum = p_cmp[0:tq]
    for p in range(1, HPG_NSA):
        psum = psum + p_cmp[p * tq:(p + 1) * tq]
    cs = lax.broadcasted_iota(jnp.int32, (nblk, LANES), 0) * S_CMP
    ss = lax.broadcasted_iota(jnp.int32, (nblk, LANES), 1) * L_SLC
    overlap = ((cs < ss + L_SLC) & (cs + L_CMP > ss) & (cs <= seq - L_CMP) & (ss < seq)).astype(F32)
    imp = jnp.dot(psum, overlap, precision=HI, preferred_element_type=F32)
    blk = lax.broadcasted_iota(jnp.int32, (tq, LANES), 1)
    cur = (q0 + lax.broadcasted_iota(jnp.int32, (tq, LANES), 0)) // L_SLC
    valid = blk <= cur
    forced = (blk == 0) | (blk == cur) | (blk == cur - 1)
    score = jnp.where(valid, jnp.where(forced, FORCE_SCORE, imp), NEG)
    rank = jnp.zeros((tq, LANES), jnp.int32)
    for mp in range(n_slc):
        colv = score[:, mp:mp + 1]
        rank = rank + ((colv > score) | ((colv == score) & (mp < blk))).astype(jnp.int32)
    selb = (valid & (rank < n_top)).astype(F32).astype(BF16)

    tk = ATT_TK
    qrow = q0 + lax.broadcasted_iota(jnp.int32, (tq, tk), 0)
    col = lax.broadcasted_iota(jnp.int32, (tq, tk), 1)

    def branch(k_ref, v_ref, first, nb, bias_fn):
        def scores(j, c):
            off = pl.multiple_of((first + j) * tk, tk)
            s = lax.dot_general(qb, k_ref[0, pl.ds(off, tk), :].astype(BF16), NT,
                                preferred_element_type=F32)
            bias = bias_fn(off)
            for p in range(HPG_NSA):
                s_scr[j, p * tq:(p + 1) * tq, :] = s[p * tq:(p + 1) * tq] + bias
            return c

        _for_blocks(nb, NSA_UNROLL, scores)

        def v_blk(j):
            return v_ref[0, pl.ds(pl.multiple_of((first + j) * tk, tk), tk), :].astype(BF16)

        return _softmax_pv(s_scr, nb, NSA_UNROLL, v_blk, m_scr, l_scr, acc_scr)

    def slc_bias(off):
        sel = jnp.dot(selb, exp_ref[:, pl.ds(off, tk)], preferred_element_type=F32)
        return jnp.where((sel > 0.5) & (off + col <= qrow), 0.0, NEG)

    last = (q0 + tq - 1) // tk
    o_slc = branch(ks_ref, vs_ref, 0, (last // NSA_UNROLL + 1) * NSA_UNROLL, slc_bias)

    def win_bias(off):
        d = qrow - (off + col)
        return jnp.where((d >= 0) & (d < WINDOW), 0.0, NEG)

    nwin = min(NSA_WIN_BLOCKS, seq // tk)
    o_win = branch(kw_ref, vw_ref, jnp.maximum(last - (nwin - 1), 0), nwin, win_bias)

    sg = jax.nn.sigmoid(sm_ref[0])
    for p in range(HPG_NSA):
        sl = slice(p * tq, (p + 1) * tq)
        base = 2 * N_HEADS + g * HPG_NSA + p
        o = (_lane_col(sg, blk, base) * o_cmp[sl] + _lane_col(sg, blk, base + N_HEADS) * o_slc[sl]
             + _lane_col(sg, blk, base + 2 * N_HEADS) * o_win[sl])
        o_ref[0, :, p * HEAD_DIM:(p + 1) * HEAD_DIM] = o.astype(BF16)


def nsa_attention(z3, kc, vc):
    bsz, seq, _ = z3.shape
    tq = NSA_TQ
    nblk = seq // S_CMP
    full = lambda cb: pl.BlockSpec((1, seq, LANES), lambda b, g, i: (b, 0, cb + g))
    cspec = pl.BlockSpec((1, 1, nblk, HEAD_DIM), lambda b, g, i: (b, g, 0, 0))
    qw = HPG_NSA * HEAD_DIM
    return pl.pallas_call(
        functools.partial(_nsa_attn_body, seq=seq),
        grid=(bsz, G_NSA, seq // tq),
        in_specs=[pl.BlockSpec((1, tq, qw), lambda b, g, i: (b, i, AB_NQ * LANES // qw + g)),
                  cspec, cspec,
                  full(AB_NKV + 2 * G_NSA), full(AB_NKV + 3 * G_NSA),
                  full(AB_NKV + 4 * G_NSA), full(AB_NKV + 5 * G_NSA),
                  pl.BlockSpec((1, tq, LANES), lambda b, g, i: (b, i, AB_SMALL)),
                  pl.BlockSpec((LANES, seq), lambda b, g, i: (0, 0))],
        out_specs=pl.BlockSpec((1, tq, qw), lambda b, g, i: (b, i, g)),
        out_shape=jax.ShapeDtypeStruct((bsz, seq, N_HEADS * HEAD_DIM), BF16),
        scratch_shapes=_softmax_scratch(HPG_NSA * tq, seq // ATT_TK, ATT_TK),
        compiler_params=_cparams(("parallel", "parallel", "arbitrary")),
        name="nsa_attention",
    )(z3, kc, vc, z3, z3, z3, z3, z3, _nsa_expand(seq))


def _pad_cols(w, width):
    return jnp.pad(w, ((0, 0), (0, width - w.shape[1])))


def _ab_weight(w):
    small = _pad_cols(jnp.concatenate([w[:, 6144:6176], w[:, 13344:13392]], axis=1), SMALL_W)
    return jnp.concatenate([w[:, :6144], w[:, 6176:8224], w[:, 8224:10272], w[:, 10272:13344], small],
                           axis=1).astype(BF16)


def _cd_weight(w):
    return jnp.concatenate([w[:, :14336], _pad_cols(w[:, 14336:14352], SMALL_W)], axis=1).astype(BF16)


def _row128(v):
    return jnp.pad(v.astype(F32), (0, LANES - v.shape[0])).reshape(1, LANES)


def kernel(x, p, ab_norm_pre, ab_norm_post, ab_w_in, gdn_conv_w, gdn_a_log, gdn_dt_bias, gdn_norm, nsa_pe_k, nsa_pe_v, nsa_cmp_k1, nsa_cmp_k2, nsa_cmp_v1, nsa_cmp_v2, ab_w_out, cd_norm_pre, cd_norm_post, cd_w_in, hgrn_lb_logits, hgrn_norm, fox_f_bias, cd_w_out, ffn_norm_pre, ffn_norm_post, ffn_w_up, ffn_conv_w, ffn_conv_b, ffn_w_down, ple_w_proj, ple_gate_norm, ple_w_gate, ple_norm_post):
    bsz, seq, dm = x.shape
    depth = p.shape[0]
    m = bsz * seq
    half = N_HEADS * HEAD_DIM
    sm_ = jax.nn.softmax(hgrn_lb_logits.astype(F32), axis=0)
    lb_table = jnp.cumsum(sm_, axis=0) - sm_[0]
    xf = x.reshape(m, dm)
    pf = p.reshape(depth, m, -1)
    w_up, w_down = ffn_w_up.astype(BF16), ffn_w_down.astype(BF16)
    w_gate, w_proj = ple_w_gate.astype(BF16), ple_w_proj.astype(BF16)
    for li in range(depth):
        j = li // 2
        if li % 2 == 0:
            z3 = norm_matmul(xf, ab_norm_pre[j], _ab_weight(ab_w_in[j])).reshape(bsz, seq, AB_N)
            prm = jnp.concatenate([_row128(gdn_a_log[j]), _row128(gdn_dt_bias[j]),
                                   jnp.zeros((6, LANES), F32)], axis=0)
            o_a = gated_deltanet(z3, gdn_conv_w[j], prm, gdn_norm[j])
            kc, vc = nsa_compress(z3, nsa_pe_k[j], nsa_pe_v[j], nsa_cmp_k1[j], nsa_cmp_k2[j],
                                  nsa_cmp_v1[j], nsa_cmp_v2[j])
            o_b = nsa_attention(z3, kc, vc)
            w_out, post = ab_w_out[j], ab_norm_post[j]
        else:
            z3 = norm_matmul(xf, cd_norm_pre[j], _cd_weight(cd_w_in[j])).reshape(bsz, seq, CD_N)
            o_a = hgrn2(z3, lb_table[li], hgrn_norm[j])
            o_b = fox_attention(z3, fox_gates(z3, _row128(fox_f_bias[j])))
            w_out, post = cd_w_out[j], cd_norm_post[j]
        xf = outproj(o_a.reshape(m, half), o_b.reshape(m, half), w_out.astype(BF16), xf, post)
        xf = conv_ffn(xf, li, ffn_norm_pre, w_up, ffn_conv_w, ffn_conv_b, w_down, ffn_norm_post, seq)
        xf = ple(xf, li, pf, ple_gate_norm, w_gate, w_proj, ple_norm_post)
    return xf.reshape(bsz, seq, dm)
```

```python
import functools

import numpy as np
import jax
import jax.numpy as jnp
from jax import lax
from jax.experimental import pallas as pl
from jax.experimental.pallas import tpu as pltpu

F32 = jnp.float32
BF16 = jnp.bfloat16
HI = lax.Precision.HIGHEST
NT = (((1,), (1,)), ((), ()))
TN = (((0,), (0,)), ((), ()))

D_MODEL = 4096
HEAD_DIM = 128
N_HEADS = 16
G_NSA = 4
HPG_NSA = 4
L_CMP = 32
S_CMP = 16
L_SLC = 64
N_SEL = 8
WINDOW = 512
FORCE_SCORE = 1e4
GDN_CHUNK = 64
GDN_GROUP = 256
GDN_LOCKSTEP = 8
HGRN_CHUNK = 256
D_FF = 11008
EPS = 1e-6
NEG = -1e30
LANES = 128
SMALL_W = 512

AB_QKV, AB_GATE, AB_NQ, AB_NKV, AB_SMALL = 0, 48, 64, 80, 104
AB_N = 108 * LANES
CD_Q, CD_F, CD_I, CD_G, CD_FQKV, CD_SMALL = 0, 16, 32, 48, 64, 112
CD_N = 116 * LANES

VMEM_LIMIT = 56 * 1024 * 1024


def _cparams(sem):
    return pltpu.CompilerParams(dimension_semantics=sem, vmem_limit_bytes=VMEM_LIMIT)


def _rms(x, w):
    return x * lax.rsqrt(jnp.mean(x * x, axis=-1, keepdims=True) + EPS) * w


def _silu(x):
    return x * jax.nn.sigmoid(x)


def _lane_col(x, lane_idx, lane):
    return jnp.sum(jnp.where(lane_idx == lane, x, 0.0), axis=-1, keepdims=True)


def _split2(x):
    hi = x.astype(BF16)
    return hi, (x - hi.astype(F32)).astype(BF16)


def _dot_sel(sel, x):
    n = x.shape[1]
    hi, lo = _split2(x)
    y = jnp.dot(sel, jnp.concatenate([hi, lo], axis=1), preferred_element_type=F32)
    return y[:, :n] + y[:, n:]


def _norm_matmul_body(x_ref, nw_ref, w_ref, o_ref, h_ref):
    @pl.when(pl.program_id(1) == 0)
    def _():
        h_ref[...] = _rms(x_ref[...], nw_ref[...]).astype(BF16)

    o_ref[...] = jnp.dot(h_ref[...], w_ref[...], preferred_element_type=F32)


def norm_matmul(x, nw, w, tm=512, tn=512):
    m, k = x.shape
    n = w.shape[1]
    return pl.pallas_call(
        _norm_matmul_body,
        grid=(m // tm, n // tn),
        in_specs=[pl.BlockSpec((tm, k), lambda i, j: (i, 0)),
                  pl.BlockSpec((1, k), lambda i, j: (0, 0)),
                  pl.BlockSpec((k, tn), lambda i, j: (0, j))],
        out_specs=pl.BlockSpec((tm, tn), lambda i, j: (i, j)),
        out_shape=jax.ShapeDtypeStruct((m, n), F32),
        scratch_shapes=[pltpu.VMEM((tm, k), BF16)],
        compiler_params=_cparams(("parallel", "arbitrary")),
        name="norm_matmul",
    )(x, nw.reshape(1, k), w)


def _outproj_body(oa_ref, ob_ref, wa_ref, wb_ref, x_ref, nw_ref, o_ref, *, tn):
    j = pl.program_id(1)
    y = jnp.dot(oa_ref[...], wa_ref[...], preferred_element_type=F32)
    y = y + jnp.dot(ob_ref[...], wb_ref[...], preferred_element_type=F32)
    o_ref[:, pl.ds(pl.multiple_of(j * tn, tn), tn)] = y

    @pl.when(j == pl.num_programs(1) - 1)
    def _():
        o_ref[...] = x_ref[...] + _rms(o_ref[...], nw_ref[...])


def outproj(oa, ob, w, x, nw, tm=512, tn=512):
    m, ka = oa.shape
    n = w.shape[1]
    return pl.pallas_call(
        functools.partial(_outproj_body, tn=tn),
        grid=(m // tm, n // tn),
        in_specs=[pl.BlockSpec((tm, ka), lambda i, j: (i, 0)),
                  pl.BlockSpec((tm, ka), lambda i, j: (i, 0)),
                  pl.BlockSpec((ka, tn), lambda i, j: (0, j)),
                  pl.BlockSpec((ka, tn), lambda i, j: (1, j)),
                  pl.BlockSpec((tm, n), lambda i, j: (i, 0), pipeline_mode=pl.Buffered(1)),
                  pl.BlockSpec((1, n), lambda i, j: (0, 0))],
        out_specs=pl.BlockSpec((tm, n), lambda i, j: (i, 0)),
        out_shape=jax.ShapeDtypeStruct((m, n), F32),
        compiler_params=_cparams(("parallel", "arbitrary")),
        name="outproj",
    )(oa, ob, w, w, x, nw.reshape(1, n))


FFN_HALO = 16


def _ffn_body(x_ref, xh_ref, nw_ref, wg_ref, wu_ref, cwg_ref, cwu_ref, cbg_ref, cbu_ref,
              wd_ref, pw_ref, o_ref, h_s, *, tm, seq):
    i = pl.program_id(0)
    j = pl.program_id(1)

    @pl.when(j == 0)
    def _():
        keep = jnp.where((i * tm) % seq == 0, 0.0, 1.0)
        h_s[0:FFN_HALO, :] = (_rms(xh_ref[...], nw_ref[...]) * keep).astype(BF16)
        h_s[FFN_HALO:, :] = _rms(x_ref[...], nw_ref[...]).astype(BF16)
        o_ref[...] = jnp.zeros_like(o_ref)

    h = h_s[...]

    def branch(w_ref, cw_ref, cb_ref):
        u = jnp.dot(h, w_ref[...], preferred_element_type=F32)
        cw = cw_ref[...]
        y = u * cw[2:3] + pltpu.roll(u, 1, 0) * cw[1:2] + pltpu.roll(u, 2, 0) * cw[0:1]
        return y[FFN_HALO:] + cb_ref[...]

    g = branch(wg_ref, cwg_ref, cbg_ref)
    u = branch(wu_ref, cwu_ref, cbu_ref)
    act = (_silu(g) * u).astype(BF16)
    o_ref[...] += jnp.dot(act, wd_ref[...], preferred_element_type=F32)

    @pl.when(j == pl.num_programs(1) - 1)
    def _():
        o_ref[...] = x_ref[...] + _rms(o_ref[...], pw_ref[...])


def conv_ffn(x, li, nw, w_up, conv_w, conv_b, w_down, pw, seq, tm=512, tf=256):
    m, k = x.shape
    nl, f, _ = w_down.shape
    nf = f // tf
    hb = tm // FFN_HALO
    cb = conv_b.reshape(nl, 1, 2 * f)
    row = pl.BlockSpec((None, 1, k), lambda i, j: (li, 0, 0))
    return pl.pallas_call(
        functools.partial(_ffn_body, tm=tm, seq=seq),
        grid=(m // tm, nf),
        in_specs=[pl.BlockSpec((tm, k), lambda i, j: (i, 0), pipeline_mode=pl.Buffered(1)),
                  pl.BlockSpec((FFN_HALO, k), lambda i, j: (jnp.maximum(i * hb - 1, 0), 0)),
                  row,
                  pl.BlockSpec((None, k, tf), lambda i, j: (li, 0, j)),
                  pl.BlockSpec((None, k, tf), lambda i, j: (li, 0, nf + j)),
                  pl.BlockSpec((None, 3, tf), lambda i, j: (li, 0, j)),
                  pl.BlockSpec((None, 3, tf), lambda i, j: (li, 0, nf + j)),
                  pl.BlockSpec((None, 1, tf), lambda i, j: (li, 0, j)),
                  pl.BlockSpec((None, 1, tf), lambda i, j: (li, 0, nf + j)),
                  pl.BlockSpec((None, tf, k), lambda i, j: (li, j, 0)),
                  row],
        out_specs=pl.BlockSpec((tm, k), lambda i, j: (i, 0)),
        out_shape=jax.ShapeDtypeStruct((m, k), F32),
        scratch_shapes=[pltpu.VMEM((tm + FFN_HALO, k), BF16)],
        compiler_params=_cparams(("parallel", "arbitrary")),
        name="conv_ffn",
    )(x, x, nw.reshape(nl, 1, k), w_up, w_up, conv_w, conv_w, cb, cb, w_down, pw.reshape(nl, 1, k))


def _ple_body(x_ref, p_ref, gnw_ref, wg_ref, wp_ref, pnw_ref, o_ref, h_s, *, tn):
    j = pl.program_id(1)

    @pl.when(j == 0)
    def _():
        h_s[...] = _rms(x_ref[...], gnw_ref[...]).astype(BF16)

    gate = jax.nn.sigmoid(jnp.dot(h_s[...], wg_ref[...], preferred_element_type=F32))
    proj = jnp.dot(p_ref[...].astype(BF16), wp_ref[...], preferred_element_type=F32)
    o_ref[:, pl.ds(pl.multiple_of(j * tn, tn), tn)] = gate * proj

    @pl.when(j == pl.num_programs(1) - 1)
    def _():
        o_ref[...] = x_ref[...] + _rms(o_ref[...], pnw_ref[...])


def ple(x, li, p, gnw, wg, wp, pnw, tm=512, tn=512):
    m, k = x.shape
    nl, _, dp = p.shape
    row = pl.BlockSpec((None, 1, k), lambda i, j: (li, 0, 0))
    return pl.pallas_call(
        functools.partial(_ple_body, tn=tn),
        grid=(m // tm, k // tn),
        in_specs=[pl.BlockSpec((tm, k), lambda i, j: (i, 0), pipeline_mode=pl.Buffered(1)),
                  pl.BlockSpec((None, tm, dp), lambda i, j: (li, i, 0)),
                  row,
                  pl.BlockSpec((None, k, tn), lambda i, j: (li, 0, j)),
                  pl.BlockSpec((None, dp, tn), lambda i, j: (li, 0, j)),
                  row],
        out_specs=pl.BlockSpec((tm, k), lambda i, j: (i, 0)),
        out_shape=jax.ShapeDtypeStruct((m, k), F32),
        scratch_shapes=[pltpu.VMEM((tm, k), BF16)],
        compiler_params=_cparams(("parallel", "arbitrary")),
        name="ple",
    )(x, p, gnw.reshape(nl, 1, k), wg, wp, pnw.reshape(nl, 1, k))


def _gdn_ltri():
    r = np.arange(GDN_GROUP)[:, None]
    t = np.arange(GDN_GROUP)[None, :]
    return jnp.asarray((((r // GDN_CHUNK) == (t // GDN_CHUNK)) & (r >= t)).astype(np.float32), dtype=BF16)


def _gdn_body(zq_ref, zk_ref, zv_ref, zg_ref, sm_ref, cwq_ref, cwk_ref, cwv_ref, prm_ref, nw_ref, ltri_ref,
              o_ref, q_s, k_s, v_s, g_s, b_s, o0_s, n0_s, gl_s, qe_s, m_s, *, seq):
    hd = pl.program_id(1)
    c = GDN_CHUNK
    gs = GDN_GROUP
    row = lax.broadcasted_iota(jnp.int32, (seq, LANES), 0)
    lane = lax.broadcasted_iota(jnp.int32, (seq, LANES), 1)

    def conv_silu(z_ref, w_ref):
        z = z_ref[0]
        w = w_ref[...]
        y = z * w[3:4]
        for s in (1, 2, 3):
            y = y + jnp.where(row >= s, pltpu.roll(z, s, 0), 0.0) * w[3 - s:4 - s]
        return _silu(y)

    q = conv_silu(zq_ref, cwq_ref)
    k = conv_silu(zk_ref, cwk_ref)
    q_s[...] = q * lax.rsqrt(jnp.sum(q * q, -1, keepdims=True) + EPS) * (HEAD_DIM ** -0.5)
    k_s[...] = k * lax.rsqrt(jnp.sum(k * k, -1, keepdims=True) + EPS)
    v_s[...] = conv_silu(zv_ref, cwv_ref)

    sm = sm_ref[0]
    glog = -jnp.exp(prm_ref[0:1, :]) * jax.nn.softplus(sm + prm_ref[1:2, :])
    g_s[...] = jnp.broadcast_to(_lane_col(glog, lane, hd), (seq, LANES))
    b_s[...] = jnp.broadcast_to(_lane_col(jax.nn.sigmoid(sm), lane, N_HEADS + hd), (seq, LANES))

    nper = gs // c
    rs = lax.broadcasted_iota(jnp.int32, (c, gs), 0)
    cs = lax.broadcasted_iota(jnp.int32, (c, gs), 1)
    jj = cs % c
    cblk = cs // c
    incl = rs >= jj
    strict = rs > jj
    eye_side = (rs == jj).astype(F32)
    r2 = lax.broadcasted_iota(jnp.int32, (gs, gs), 0)
    c2 = lax.broadcasted_iota(jnp.int32, (gs, gs), 1)
    bdm = ((r2 // c) == (c2 // c)).astype(F32)
    bdm_b = bdm.astype(BF16)
    lane_c = lax.broadcasted_iota(jnp.int32, (c, LANES), 1)
    lane_g = lax.broadcasted_iota(jnp.int32, (gs, LANES), 1)
    ones3 = (lane_c < 3).astype(F32).astype(BF16)

    def side_col(xb):
        left = jnp.where(lane_c < c, xb[0:c], xb[c:2 * c])
        right = jnp.where(lane_c < c, xb[2 * c:3 * c], xb[3 * c:4 * c])
        return jnp.concatenate([left, right], axis=1)

    def to_side(full):
        out = full[0:c]
        for ci in range(1, nper):
            out = jnp.where(cblk == ci, full[ci * c:(ci + 1) * c], out)
        return out

    def bd(xs, mask):
        return jnp.concatenate([xs] * nper, axis=0) * mask

    def mm3(ph, pl_, xh, xl):
        bh = bd(xh, bdm_b)
        lhs = jnp.concatenate([ph, ph, pl_], axis=1)
        rhs = jnp.concatenate([bh, bd(xl, bdm_b), bh], axis=0)
        return jnp.dot(lhs, rhs, preferred_element_type=F32)

    def group_pre(gi):
        off = pl.multiple_of(gi * gs, gs)
        qg = q_s[pl.ds(off, gs), :]
        kg = k_s[pl.ds(off, gs), :]
        bg = b_s[pl.ds(off, gs), :]
        gam = _dot_sel(ltri_ref[...], g_s[pl.ds(off, gs), :])
        g_hi = gam.astype(BF16).astype(F32)
        g_mid = (gam - g_hi).astype(BF16).astype(F32)
        g_lo = gam - g_hi - g_mid
        gam3 = jnp.where(lane_g == 0, g_hi, jnp.where(lane_g == 1, g_mid, jnp.where(lane_g == 2, g_lo, 0.0)))
        gam_row = lax.dot_general(ones3, gam3.astype(BF16), NT, preferred_element_type=F32)
        dec = jnp.where(incl, jnp.exp(jnp.where(incl, side_col(gam) - gam_row, 0.0)), 0.0)
        qb = qg.astype(BF16)
        kb = kg.astype(BF16)
        kk = to_side(lax.dot_general(kb, kb, NT, preferred_element_type=F32))
        a = jnp.where(strict, side_col(bg) * dec * kk, 0.0)
        return dict(off=off, qg=qg, kg=kg, bg=bg, gam=gam, dec=dec, qb=qb, kb=kb, a=a)

    def group_post(gi, st, tinv):
        off, qg, kg, bg, gam, dec, qb, kb = (st[n] for n in ("off", "qg", "kg", "bg", "gam", "dec", "qb", "kb"))
        vg = v_s[pl.ds(off, gs), :]
        th, tl = _split2(tinv)
        eg = jnp.exp(gam)
        rh, rl = _split2(jnp.concatenate([vg * bg, kg * (bg * eg)], axis=1))
        bth = bd(th, bdm_b)
        sol = jnp.dot(jnp.concatenate([bth, bth, bd(tl, bdm_b)], axis=1),
                      jnp.concatenate([rh, rl, rh], axis=0), preferred_element_type=F32)
        solb = sol.astype(BF16)
        qk = bd(dec, bdm) * lax.dot_general(qb, kb, NT, preferred_element_type=F32)
        qkuw = jnp.dot(qk.astype(BF16), solb, preferred_element_type=F32)
        o0_s[pl.ds(off, gs), :] = qkuw[:, :HEAD_DIM]
        qe_s[pl.ds(off, gs), :] = (qg * eg - qkuw[:, HEAD_DIM:]).astype(BF16)
        for ci in range(nper):
            lo = ci * c
            gl = gam[lo + c - 1:lo + c, :]
            kd = (kg[lo:lo + c] * jnp.exp(gl - gam[lo:lo + c])).astype(BF16)
            kds = lax.dot_general(kd, solb[lo:lo + c], TN, preferred_element_type=F32)
            so = pl.multiple_of((gi * nper + ci) * HEAD_DIM, HEAD_DIM)
            n0_s[pl.ds(so, HEAD_DIM), :] = kds[:, :HEAD_DIM]
            m_s[pl.ds(so, HEAD_DIM), :] = (-kds[:, HEAD_DIM:]).astype(BF16)
            gl_s[pl.ds(pl.multiple_of((gi * nper + ci) * 8, 8), 8), :] = jnp.broadcast_to(jnp.exp(gl), (8, LANES))

    ngroups = seq // gs
    nlock = GDN_LOCKSTEP if ngroups % GDN_LOCKSTEP == 0 else 1

    def groups(it, carry):
        gis = [it + u * (ngroups // nlock) for u in range(nlock)]
        sts = [group_pre(gi) for gi in gis]
        tinvs = [eye_side - st["a"] for st in sts]
        xs = [_split2(st["a"]) for st in sts]
        for _ in range(5):
            xs = [_split2(mm3(xh, xl, xh, xl)) for xh, xl in xs]
            ts = [_split2(t) for t in tinvs]
            tinvs = [t + mm3(th, tl, xh, xl) for t, (th, tl), (xh, xl) in zip(tinvs, ts, xs)]
        for gi, st, t in zip(gis, sts, tinvs):
            group_post(gi, st, t)
        return carry

    lax.fori_loop(0, ngroups // nlock, groups, 0)

    def step(n, s):
        off = pl.multiple_of(n * c, c)
        so = pl.multiple_of(n * HEAD_DIM, HEAD_DIM)
        sb = s.astype(BF16)
        o0_s[pl.ds(off, c), :] = o0_s[pl.ds(off, c), :] + jnp.dot(
            qe_s[pl.ds(off, c), :], sb, preferred_element_type=F32)
        return (gl_s[pl.ds(pl.multiple_of(n * 8, 8), 1), :] * s + n0_s[pl.ds(so, HEAD_DIM), :]
                + jnp.dot(m_s[pl.ds(so, HEAD_DIM), :], sb, preferred_element_type=F32))

    lax.fori_loop(0, seq // c, step, jnp.zeros((HEAD_DIM, HEAD_DIM), F32))
    o_ref[0] = (_rms(o0_s[...], nw_ref[...]) * _silu(zg_ref[0])).astype(BF16)


def gated_deltanet(z3, conv_w, prm, norm_w):
    bsz, seq, _ = z3.shape
    zspec = lambda cb: pl.BlockSpec((1, seq, LANES), lambda b, h: (b, 0, cb + h))
    wspec = lambda cb: pl.BlockSpec((4, LANES), lambda b, h: (0, cb + h))
    nchunk = seq // GDN_CHUNK
    return pl.pallas_call(
        functools.partial(_gdn_body, seq=seq),
        grid=(bsz, N_HEADS),
        in_specs=[zspec(AB_QKV), zspec(AB_QKV + 16), zspec(AB_QKV + 32), zspec(AB_GATE),
                  pl.BlockSpec((1, seq, LANES), lambda b, h: (b, 0, AB_SMALL)),
                  wspec(0), wspec(16), wspec(32),
                  pl.BlockSpec((8, LANES), lambda b, h: (0, 0)),
                  pl.BlockSpec((1, LANES), lambda b, h: (0, 0)),
                  pl.BlockSpec((GDN_GROUP, GDN_GROUP), lambda b, h: (0, 0))],
        out_specs=pl.BlockSpec((1, seq, LANES), lambda b, h: (b, 0, h)),
        out_shape=jax.ShapeDtypeStruct((bsz, seq, N_HEADS * HEAD_DIM), BF16),
        scratch_shapes=[pltpu.VMEM((seq, LANES), F32) for _ in range(6)]
        + [pltpu.VMEM((nchunk * HEAD_DIM, HEAD_DIM), F32),
           pltpu.VMEM((nchunk * 8, LANES), F32),
           pltpu.VMEM((seq, LANES), BF16),
           pltpu.VMEM((nchunk * HEAD_DIM, HEAD_DIM), BF16)],
        compiler_params=_cparams(("parallel", "parallel")),
        name="gated_deltanet",
    )(z3, z3, z3, z3, z3, conv_w, conv_w, conv_w, prm, norm_w.reshape(1, HEAD_DIM), _gdn_ltri())


HGRN_LEVELS = tuple(HGRN_CHUNK >> (i + 1) for i in range(8))


def _hgrn_selectors():
    c = HGRN_CHUNK
    r = np.arange(c)[:, None]
    t = np.arange(c)[None, :]
    mats = [r >= t]
    for s in HGRN_LEVELS:
        isq = ((r // s) % 2) == 1
        mats.append(((r // s) == (t // s)) & ((isq & (t <= r)) | (~isq & (t > r))))
    return jnp.asarray(np.concatenate(mats, axis=0).astype(np.float32), dtype=BF16)


def _hgrn_body(zq_ref, zf_ref, zi_ref, zg_ref, lb_ref, nw_ref, sel_ref, o_ref, *, seq):
    c = HGRN_CHUNK
    r = lax.broadcasted_iota(jnp.int32, (c, c), 0)
    cc = lax.broadcasted_iota(jnp.int32, (c, c), 1)
    diag = r == cc
    x = r ^ cc
    lev = jnp.zeros((c, c), jnp.int32)
    for bit in range(1, 8):
        lev = lev + (x >= (1 << bit)).astype(jnp.int32)
    lev = jnp.where(r > cc, lev, -1)
    rr = lax.broadcasted_iota(jnp.int32, (c, LANES), 0)
    lb = lb_ref[...]

    def chunk(n, st):
        off = pl.multiple_of(n * c, c)
        fx = zf_ref[0, pl.ds(off, c), :]
        q = _silu(zq_ref[0, pl.ds(off, c), :])
        vb = zi_ref[0, pl.ds(off, c), :].astype(BF16)
        lf = jnp.log(lb + (1.0 - lb) * jax.nn.sigmoid(fx))
        k = (1.0 - lb) * jax.nn.sigmoid(-fx)
        sums = _dot_sel(sel_ref[...], lf)
        b = sums[0:c]
        att = jnp.where(diag, jnp.sum(q * k, -1, keepdims=True), 0.0)
        for i, s in enumerate(HGRN_LEVELS):
            e = jnp.exp(sums[(i + 1) * c:(i + 2) * c])
            isq = ((rr // s) % 2) == 1
            qs = jnp.where(isq, q * e, 0.0).astype(BF16)
            ks = jnp.where(isq, 0.0, k * e).astype(BF16)
            part = lax.dot_general(qs, ks, NT, preferred_element_type=F32)
            att = jnp.where(lev == (7 - i), part, att)
        bl = b[c - 1:c, :]
        o = lax.dot_general((q * jnp.exp(b)).astype(BF16), st.astype(BF16), NT, preferred_element_type=F32)
        o = o + jnp.dot(att.astype(BF16), vb, preferred_element_type=F32)
        o = _rms(o, nw_ref[...]) * _silu(zg_ref[0, pl.ds(off, c), :])
        o_ref[0, pl.ds(off, c), :] = o.astype(BF16)
        kd = (k * jnp.exp(bl - b)).astype(BF16)
        return st * jnp.exp(bl) + lax.dot_general(vb, kd, TN, preferred_element_type=F32)

    lax.fori_loop(0, seq // c, chunk, jnp.zeros((HEAD_DIM, HEAD_DIM), F32))


def hgrn2(z3, lb, norm_w):
    bsz, seq, _ = z3.shape
    zspec = lambda cb: pl.BlockSpec((1, seq, LANES), lambda b, h: (b, 0, cb + h))
    sel = _hgrn_selectors()
    return pl.pallas_call(
        functools.partial(_hgrn_body, seq=seq),
        grid=(bsz, N_HEADS),
        in_specs=[zspec(CD_Q), zspec(CD_F), zspec(CD_I), zspec(CD_G),
                  pl.BlockSpec((1, LANES), lambda b, h: (0, h)),
                  pl.BlockSpec((1, LANES), lambda b, h: (0, 0)),
                  pl.BlockSpec(sel.shape, lambda b, h: (0, 0))],
        out_specs=pl.BlockSpec((1, seq, LANES), lambda b, h: (b, 0, h)),
        out_shape=jax.ShapeDtypeStruct((bsz, seq, N_HEADS * HEAD_DIM), BF16),
        compiler_params=_cparams(("parallel", "parallel")),
        name="hgrn2",
    )(z3, z3, z3, z3, lb.reshape(1, N_HEADS * HEAD_DIM), norm_w.reshape(1, HEAD_DIM), sel)


FOX_BLK = 256


def _fox_gate_body(sm_ref, bias_ref, cumt_ref, *, seq):
    c = FOX_BLK
    r = lax.broadcasted_iota(jnp.int32, (c, c), 0)
    cc = lax.broadcasted_iota(jnp.int32, (c, c), 1)
    ltri = (r >= cc).astype(F32)
    r1 = lax.broadcasted_iota(jnp.int32, (LANES, LANES), 0)
    c1 = lax.broadcasted_iota(jnp.int32, (LANES, LANES), 1)
    eye = (r1 == c1).astype(F32)

    def blk(n, carry):
        off = pl.multiple_of(n * c, c)
        ls = jax.nn.log_sigmoid(sm_ref[0, pl.ds(off, c), :] + bias_ref[...])
        cum = carry + jnp.dot(ltri, ls, precision=HI, preferred_element_type=F32)
        cumt_ref[0, :, pl.ds(off, c)] = lax.dot_general(eye, cum, NT, precision=HI, preferred_element_type=F32)
        return cum[c - 1:c, :]

    lax.fori_loop(0, seq // c, blk, jnp.zeros((1, LANES), F32))


def fox_gates(z3, bias_row):
    bsz, seq, _ = z3.shape
    return pl.pallas_call(
        functools.partial(_fox_gate_body, seq=seq),
        grid=(bsz,),
        in_specs=[pl.BlockSpec((1, seq, LANES), lambda b: (b, 0, CD_SMALL)),
                  pl.BlockSpec((1, LANES), lambda b: (0, 0))],
        out_specs=pl.BlockSpec((1, LANES, seq), lambda b: (b, 0, 0)),
        out_shape=jax.ShapeDtypeStruct((bsz, LANES, seq), F32),
        compiler_params=_cparams(("parallel",)),
        name="fox_gates",
    )(z3, bias_row)


ATT_ROWS = 128
ATT_TK = 256


def _for_blocks(n, unroll, body, carry=None):
    if isinstance(n, int):
        for j in range(n):
            carry = body(j, carry)
        return carry

    def it(i, c):
        for u in range(unroll):
            c = body(i * unroll + u, c)
        return c

    return lax.fori_loop(0, n // unroll, it, carry)


def _softmax_pv(s_scr, nblk, unroll, v_fn, m_scr, l_scr, acc_scr):
    _, rows, bw = s_scr.shape
    nlb = bw // LANES
    for ch in range(rows // ATT_ROWS):
        rs = slice(ch * ATT_ROWS, (ch + 1) * ATT_ROWS)

        def pmax(j, mel):
            for b in range(nlb):
                mel = jnp.maximum(mel, s_scr[j, rs, b * LANES:(b + 1) * LANES])
            return mel

        mel = _for_blocks(nblk, unroll, pmax, jnp.full((ATT_ROWS, LANES), NEG, F32))
        m_scr[rs, :] = jnp.broadcast_to(jnp.max(mel, -1, keepdims=True), (ATT_ROWS, LANES))
    l_scr[...] = jnp.zeros_like(l_scr)
    acc_scr[...] = jnp.zeros_like(acc_scr)

    def ppv(j, c):
        pcs = []
        for ch in range(rows // ATT_ROWS):
            rs = slice(ch * ATT_ROWS, (ch + 1) * ATT_ROWS)
            mb = m_scr[rs, :]
            lacc = l_scr[rs, :]
            ps = []
            for b in range(nlb):
                pb = jnp.exp(s_scr[j, rs, b * LANES:(b + 1) * LANES] - mb)
                lacc = lacc + pb
                ps.append(pb.astype(BF16))
            l_scr[rs, :] = lacc
            pcs.append(jnp.concatenate(ps, axis=1))
        acc_scr[...] += jnp.dot(jnp.concatenate(pcs, axis=0), v_fn(j), preferred_element_type=F32)
        return c

    _for_blocks(nblk, unroll, ppv)
    return acc_scr[...] / jnp.sum(l_scr[...], -1, keepdims=True)


def _softmax_scratch(rows, nblk, bw):
    return [pltpu.VMEM((nblk, rows, bw), F32), pltpu.VMEM((rows, LANES), F32),
            pltpu.VMEM((rows, LANES), F32), pltpu.VMEM((rows, HEAD_DIM), F32)]


def _fox_attn_body(q_ref, k_ref, v_ref, cumt_ref, o_ref, s_scr, m_scr, l_scr, acc_scr, *, tq):
    hd = pl.program_id(1)
    qi = pl.program_id(2)
    qall = (q_ref[0] * (HEAD_DIM ** -0.5)).astype(BF16)

    def scores(j):
        off = pl.multiple_of(j * tq, tq)
        kj = k_ref[0, pl.ds(off, tq), :].astype(BF16)
        ck = cumt_ref[0, pl.ds(hd % 8, 1), pl.ds(off, tq)]
        return lax.dot_general(qall, kj, NT, preferred_element_type=F32) - ck

    row = lax.broadcasted_iota(jnp.int32, (tq, tq), 0)
    col = lax.broadcasted_iota(jnp.int32, (tq, tq), 1)

    def v_blk(j):
        return v_ref[0, pl.ds(pl.multiple_of(j * tq, tq), tq), :].astype(BF16)

    def tile(nfull):
        for j in range(nfull):
            s_scr[j] = scores(j)
        s_scr[nfull] = jnp.where(col <= row, scores(nfull), NEG)
        o_ref[0] = _softmax_pv(s_scr, nfull + 1, 1, v_blk, m_scr, l_scr, acc_scr).astype(BF16)

    lax.switch(qi, [functools.partial(tile, n) for n in range(s_scr.shape[0])])


def fox_attention(z3, cumt):
    bsz, seq, _ = z3.shape
    tq = min(512, seq)
    full = lambda cb: pl.BlockSpec((1, seq, LANES), lambda b, h, i: (b, 0, cb + h))
    return pl.pallas_call(
        functools.partial(_fox_attn_body, tq=tq),
        grid=(bsz, N_HEADS, seq // tq),
        in_specs=[pl.BlockSpec((1, tq, LANES), lambda b, h, i: (b, i, CD_FQKV + h)),
                  full(CD_FQKV + 16), full(CD_FQKV + 32),
                  pl.BlockSpec((1, 8, seq), lambda b, h, i: (b, h // 8, 0))],
        out_specs=pl.BlockSpec((1, tq, LANES), lambda b, h, i: (b, i, h)),
        out_shape=jax.ShapeDtypeStruct((bsz, seq, N_HEADS * HEAD_DIM), BF16),
        scratch_shapes=_softmax_scratch(tq, seq // tq, tq),
        compiler_params=_cparams(("parallel", "parallel", "arbitrary")),
        name="fox_attention",
    )(z3, z3, z3, cumt)


def _nsa_cmp_body(zk_ref, zv_ref, pek_ref, pev_ref, wk1_ref, wk2_ref, wv1_ref, wv2_ref, kc_ref, vc_ref, *, nblk):
    def compress(z_ref, pe_ref, w1_ref, w2_ref):
        u1 = jnp.zeros((nblk, HEAD_DIM), F32)
        u2 = jnp.zeros((nblk, HEAD_DIM), F32)
        for l in range(S_CMP):
            zl = z_ref[0, :, l, :]
            u1 = u1 + jnp.dot(zl, w1_ref[l * HEAD_DIM:(l + 1) * HEAD_DIM, :], precision=HI,
                              preferred_element_type=F32)
            u2 = u2 + jnp.dot(zl, w1_ref[(S_CMP + l) * HEAD_DIM:(S_CMP + l + 1) * HEAD_DIM, :], precision=HI,
                              preferred_element_type=F32)
        pe = jnp.dot(jnp.broadcast_to(pe_ref[...], (8, L_CMP * HEAD_DIM)), w1_ref[...], precision=HI,
                     preferred_element_type=F32)[0:1]
        hmid = u1 + pltpu.roll(u2, nblk - 1, 0) + pe
        return jnp.dot(_silu(hmid), w2_ref[...], precision=HI, preferred_element_type=F32)

    kc_ref[0, 0] = compress(zk_ref, pek_ref, wk1_ref, wk2_ref)
    vc_ref[0, 0] = compress(zv_ref, pev_ref, wv1_ref, wv2_ref)


def nsa_compress(z3, pe_k, pe_v, wk1, wk2, wv1, wv2):
    bsz, seq, n = z3.shape
    nblk = seq // S_CMP
    z4 = z3.reshape(bsz, nblk, S_CMP, n)
    zspec = lambda cb: pl.BlockSpec((1, nblk, S_CMP, LANES), lambda b, g: (b, 0, 0, cb + g))
    wfull = lambda a: pl.BlockSpec(a.shape, lambda b, g: (0, 0))
    pek = pe_k.reshape(1, L_CMP * HEAD_DIM)
    pev = pe_v.reshape(1, L_CMP * HEAD_DIM)
    out = jax.ShapeDtypeStruct((bsz, G_NSA, nblk, HEAD_DIM), F32)
    ospec = pl.BlockSpec((1, 1, nblk, HEAD_DIM), lambda b, g: (b, g, 0, 0))
    return pl.pallas_call(
        functools.partial(_nsa_cmp_body, nblk=nblk),
        grid=(bsz, G_NSA),
        in_specs=[zspec(AB_NKV), zspec(AB_NKV + G_NSA), wfull(pek), wfull(pev),
                  wfull(wk1), wfull(wk2), wfull(wv1), wfull(wv2)],
        out_specs=[ospec, ospec],
        out_shape=[out, out],
        compiler_params=_cparams(("parallel", "parallel")),
        name="nsa_compress",
    )(z4, z4, pek, pev, wk1, wk2, wv1, wv2)


NSA_TQ = 128


def _nsa_expand(seq):
    m = np.arange(LANES)[:, None]
    t = np.arange(seq)[None, :]
    return jnp.asarray(((t // L_SLC) == m).astype(np.float32), dtype=BF16)


NSA_UNROLL = 2
NSA_WIN_BLOCKS = max(e // ATT_TK - max(e - (WINDOW + NSA_TQ - 2), 0) // ATT_TK + 1
                     for e in range(NSA_TQ - 1, 8 * WINDOW, NSA_TQ))


def _nsa_attn_body(q_ref, kc_ref, vc_ref, ks_ref, vs_ref, kw_ref, vw_ref, sm_ref, exp_ref, o_ref,
                   s_scr, m_scr, l_scr, acc_scr, *, seq):
    g = pl.program_id(1)
    qi = pl.program_id(2)
    tq = NSA_TQ
    rows = HPG_NSA * tq
    nblk = seq // S_CMP
    n_slc = seq // L_SLC
    n_top = min(N_SEL, n_slc)
    q0 = qi * tq

    qall = q_ref[0]
    qs = jnp.concatenate([qall[:, p * HEAD_DIM:(p + 1) * HEAD_DIM] for p in range(HPG_NSA)], axis=0)
    qs = qs * (HEAD_DIM ** -0.5)
    qb = qs.astype(BF16)
    tpos = q0 + lax.broadcasted_iota(jnp.int32, (rows, 1), 0) % tq

    kc = kc_ref[0, 0]
    vc = vc_ref[0, 0]
    qh, ql = _split2(qs)
    kh, kl = _split2(kc)
    sc = lax.dot_general(jnp.concatenate([qh, qh, ql], axis=1), jnp.concatenate([kh, kl, kh], axis=1), NT,
                         preferred_element_type=F32)
    nidx = lax.broadcasted_iota(jnp.int32, (rows, nblk), 1)
    cmask = (nidx * S_CMP + (L_CMP - 1) <= tpos) & (nidx <= nblk - 2)
    scm = jnp.where(cmask, sc, NEG)
    mc = jnp.max(scm, -1, keepdims=True)
    ec = jnp.where(cmask, jnp.exp(scm - mc), 0.0)
    dc = jnp.sum(ec, -1, keepdims=True)
    p_cmp = ec / jnp.where(dc > 0, dc, 1.0)
    o_cmp = jnp.dot(p_cmp.astype(BF16), vc.astype(BF16), preferred_element_type=F32)

    psum = p_cmp[0:tq]
    for p in range(1, HPG_NSA):
        psum = psum + p_cmp[p * tq:(p + 1) * tq]
    cs = lax.broadcasted_iota(jnp.int32, (nblk, LANES), 0) * S_CMP
    ss = lax.broadcasted_iota(jnp.int32, (nblk, LANES), 1) * L_SLC
    overlap = ((cs < ss + L_SLC) & (cs + L_CMP > ss) & (cs <= seq - L_CMP) & (ss < seq)).astype(F32).astype(BF16)
    ph, pl_ = _split2(psum)
    imp = jnp.dot(jnp.concatenate([ph, pl_], axis=1), jnp.concatenate([overlap, overlap], axis=0),
                  preferred_element_type=F32)
    blk = lax.broadcasted_iota(jnp.int32, (tq, LANES), 1)
    cur = (q0 + lax.broadcasted_iota(jnp.int32, (tq, LANES), 0)) // L_SLC
    valid = blk <= cur
    forced = (blk == 0) | (blk == cur) | (blk == cur - 1)
    score = jnp.where(valid, jnp.where(forced, FORCE_SCORE, imp), NEG)
    rank = jnp.zeros((tq, LANES), jnp.int32)
    for mp in range(n_slc):
        colv = score[:, mp:mp + 1]
        rank = rank + ((colv > score) | ((colv == score) & (mp < blk))).astype(jnp.int32)
    selb = (valid & (rank < n_top)).astype(F32).astype(BF16)

    tk = ATT_TK
    qrow = q0 + lax.broadcasted_iota(jnp.int32, (tq, tk), 0)
    col = lax.broadcasted_iota(jnp.int32, (tq, tk), 1)

    def branch(k_ref, v_ref, first, nb, bias_fn):
        def scores(j, c):
            off = pl.multiple_of((first + j) * tk, tk)
            s = lax.dot_general(qb, k_ref[0, pl.ds(off, tk), :].astype(BF16), NT,
                                preferred_element_type=F32)
            bias = bias_fn(off)
            for p in range(HPG_NSA):
                s_scr[j, p * tq:(p + 1) * tq, :] = s[p * tq:(p + 1) * tq] + bias
            return c

        _for_blocks(nb, NSA_UNROLL, scores)

        def v_blk(j):
            return v_ref[0, pl.ds(pl.multiple_of((first + j) * tk, tk), tk), :].astype(BF16)

        return _softmax_pv(s_scr, nb, NSA_UNROLL, v_blk, m_scr, l_scr, acc_scr)

    def slc_bias(off):
        sel = jnp.dot(selb, exp_ref[:, pl.ds(off, tk)], preferred_element_type=F32)
        return jnp.where((sel > 0.5) & (off + col <= qrow), 0.0, NEG)

    last = (q0 + tq - 1) // tk
    o_slc = branch(ks_ref, vs_ref, 0, (last // NSA_UNROLL + 1) * NSA_UNROLL, slc_bias)

    def win_bias(off):
        d = qrow - (off + col)
        return jnp.where((d >= 0) & (d < WINDOW), 0.0, NEG)

    nwin = min(NSA_WIN_BLOCKS, seq // tk)
    o_win = branch(kw_ref, vw_ref, jnp.maximum(last - (nwin - 1), 0), nwin, win_bias)

    sg = jax.nn.sigmoid(sm_ref[0])
    for p in range(HPG_NSA):
        sl = slice(p * tq, (p + 1) * tq)
        base = 2 * N_HEADS + g * HPG_NSA + p
        o = (_lane_col(sg, blk, base) * o_cmp[sl] + _lane_col(sg, blk, base + N_HEADS) * o_slc[sl]
             + _lane_col(sg, blk, base + 2 * N_HEADS) * o_win[sl])
        o_ref[0, :, p * HEAD_DIM:(p + 1) * HEAD_DIM] = o.astype(BF16)


def nsa_attention(z3, kc, vc):
    bsz, seq, _ = z3.shape
    tq = NSA_TQ
    nblk = seq // S_CMP
    full = lambda cb: pl.BlockSpec((1, seq, LANES), lambda b, g, i: (b, 0, cb + g))
    cspec = pl.BlockSpec((1, 1, nblk, HEAD_DIM), lambda b, g, i: (b, g, 0, 0))
    qw = HPG_NSA * HEAD_DIM
    return pl.pallas_call(
        functools.partial(_nsa_attn_body, seq=seq),
        grid=(bsz, G_NSA, seq // tq),
        in_specs=[pl.BlockSpec((1, tq, qw), lambda b, g, i: (b, i, AB_NQ * LANES // qw + g)),
                  cspec, cspec,
                  full(AB_NKV + 2 * G_NSA), full(AB_NKV + 3 * G_NSA),
                  full(AB_NKV + 4 * G_NSA), full(AB_NKV + 5 * G_NSA),
                  pl.BlockSpec((1, tq, LANES), lambda b, g, i: (b, i, AB_SMALL)),
                  pl.BlockSpec((LANES, seq), lambda b, g, i: (0, 0))],
        out_specs=pl.BlockSpec((1, tq, qw), lambda b, g, i: (b, i, g)),
        out_shape=jax.ShapeDtypeStruct((bsz, seq, N_HEADS * HEAD_DIM), BF16),
        scratch_shapes=_softmax_scratch(HPG_NSA * tq, seq // ATT_TK, ATT_TK),
        compiler_params=_cparams(("parallel", "parallel", "arbitrary")),
        name="nsa_attention",
    )(z3, kc, vc, z3, z3, z3, z3, z3, _nsa_expand(seq))


def _repack_body(w_ref, o_ref, *, sections, pad_from):
    o_ref[:, pad_from:] = jnp.zeros((o_ref.shape[0], o_ref.shape[1] - pad_from), BF16)
    for dst, src, width in sections:
        o_ref[:, dst:dst + width] = w_ref[:, src:src + width].astype(BF16)


def _repack_weight(w, sections, n_out, pad_from, tr=256):
    k, n_in = w.shape
    return pl.pallas_call(
        functools.partial(_repack_body, sections=sections, pad_from=pad_from),
        grid=(k // tr,),
        in_specs=[pl.BlockSpec((tr, n_in), lambda i: (i, 0))],
        out_specs=pl.BlockSpec((tr, n_out), lambda i: (i, 0)),
        out_shape=jax.ShapeDtypeStruct((k, n_out), BF16),
        compiler_params=_cparams(("parallel",)),
        name="repack_weight",
    )(w)


def _ab_weight(w):
    small = AB_SMALL * LANES
    sections = ((0, 0, 6144), (6144, 6176, 2048), (8192, 8224, 2048), (10240, 10272, 3072),
                (small, 6144, 32), (small + 32, 13344, 48))
    return _repack_weight(w, sections, AB_N, small)


def _cd_weight(w):
    small = CD_SMALL * LANES
    return _repack_weight(w, ((0, 0, small), (small, small, 16)), CD_N, small)


def _row128(v):
    return jnp.pad(v.astype(F32), (0, LANES - v.shape[0])).reshape(1, LANES)


def kernel(x, p, ab_norm_pre, ab_norm_post, ab_w_in, gdn_conv_w, gdn_a_log, gdn_dt_bias, gdn_norm, nsa_pe_k, nsa_pe_v, nsa_cmp_k1, nsa_cmp_k2, nsa_cmp_v1, nsa_cmp_v2, ab_w_out, cd_norm_pre, cd_norm_post, cd_w_in, hgrn_lb_logits, hgrn_norm, fox_f_bias, cd_w_out, ffn_norm_pre, ffn_norm_post, ffn_w_up, ffn_conv_w, ffn_conv_b, ffn_w_down, ple_w_proj, ple_gate_norm, ple_w_gate, ple_norm_post):
    bsz, seq, dm = x.shape
    depth = p.shape[0]
    m = bsz * seq
    half = N_HEADS * HEAD_DIM
    sm_ = jax.nn.softmax(hgrn_lb_logits.astype(F32), axis=0)
    lb_table = jnp.cumsum(sm_, axis=0) - sm_[0]
    xf = x.reshape(m, dm)
    pf = p.reshape(depth, m, -1)
    w_up, w_down = ffn_w_up.astype(BF16), ffn_w_down.astype(BF16)
    w_gate, w_proj = ple_w_gate.astype(BF16), ple_w_proj.astype(BF16)
    for li in range(depth):
        j = li // 2
        if li % 2 == 0:
            z3 = norm_matmul(xf, ab_norm_pre[j], _ab_weight(ab_w_in[j])).reshape(bsz, seq, AB_N)
            prm = jnp.concatenate([_row128(gdn_a_log[j]), _row128(gdn_dt_bias[j]),
                                   jnp.zeros((6, LANES), F32)], axis=0)
            o_a = gated_deltanet(z3, gdn_conv_w[j], prm, gdn_norm[j])
            kc, vc = nsa_compress(z3, nsa_pe_k[j], nsa_pe_v[j], nsa_cmp_k1[j], nsa_cmp_k2[j],
                                  nsa_cmp_v1[j], nsa_cmp_v2[j])
            o_b = nsa_attention(z3, kc, vc)
            w_out, post = ab_w_out[j], ab_norm_post[j]
        else:
            z3 = norm_matmul(xf, cd_norm_pre[j], _cd_weight(cd_w_in[j])).reshape(bsz, seq, CD_N)
            o_a = hgrn2(z3, lb_table[li], hgrn_norm[j])
            o_b = fox_attention(z3, fox_gates(z3, _row128(fox_f_bias[j])))
            w_out, post = cd_w_out[j], cd_norm_post[j]
        xf = outproj(o_a.reshape(m, half), o_b.reshape(m, half), w_out.astype(BF16), xf, post)
        xf = conv_ffn(xf, li, ffn_norm_pre, w_up, ffn_conv_w, ffn_conv_b, w_down, ffn_norm_post, seq)
        xf = ple(xf, li, pf, ple_gate_norm, w_gate, w_proj, ple_norm_post)
    return xf.reshape(bsz, seq, dm)
```

```python
import functools

import numpy as np
import jax
import jax.numpy as jnp
from jax import lax
from jax.experimental import pallas as pl
from jax.experimental.pallas import tpu as pltpu

F32 = jnp.float32
BF16 = jnp.bfloat16
HI = lax.Precision.HIGHEST
NT = (((1,), (1,)), ((), ()))
TN = (((0,), (0,)), ((), ()))

D_MODEL = 4096
HEAD_DIM = 128
N_HEADS = 16
G_NSA = 4
HPG_NSA = 4
L_CMP = 32
S_CMP = 16
L_SLC = 64
N_SEL = 8
WINDOW = 512
FORCE_SCORE = 1e4
GDN_CHUNK = 64
GDN_GROUP = 256
GDN_LOCKSTEP = 8
HGRN_CHUNK = 256
HGRN_LOCKSTEP = 2
D_FF = 11008
EPS = 1e-6
NEG = -1e30
LANES = 128
SMALL_W = 512

AB_QKV, AB_GATE, AB_NQ, AB_NKV, AB_SMALL = 0, 48, 64, 80, 104
AB_N = 108 * LANES
CD_Q, CD_F, CD_I, CD_G, CD_FQKV, CD_SMALL = 0, 16, 32, 48, 64, 112
CD_N = 116 * LANES

VMEM_LIMIT = 56 * 1024 * 1024


def _cparams(sem):
    return pltpu.CompilerParams(dimension_semantics=sem, vmem_limit_bytes=VMEM_LIMIT)


def _rms(x, w):
    return x * lax.rsqrt(jnp.mean(x * x, axis=-1, keepdims=True) + EPS) * w


def _silu(x):
    return x * jax.nn.sigmoid(x)


def _lane_col(x, lane_idx, lane):
    return jnp.sum(jnp.where(lane_idx == lane, x, 0.0), axis=-1, keepdims=True)


def _split2(x):
    hi = x.astype(BF16)
    return hi, (x - hi.astype(F32)).astype(BF16)


def _dot_sel(sel, x):
    n = x.shape[1]
    hi, lo = _split2(x)
    y = jnp.dot(sel, jnp.concatenate([hi, lo], axis=1), preferred_element_type=F32)
    return y[:, :n] + y[:, n:]


def _norm_matmul_body(x_ref, nw_ref, w_ref, o_ref, h_ref):
    @pl.when(pl.program_id(1) == 0)
    def _():
        h_ref[...] = _rms(x_ref[...], nw_ref[...]).astype(BF16)

    o_ref[...] = lax.dot_general(h_ref[...], w_ref[...], NT, preferred_element_type=F32)


def norm_matmul(x, nw, wt, tm=512, tn=512):
    m, k = x.shape
    n = wt.shape[0]
    return pl.pallas_call(
        _norm_matmul_body,
        grid=(m // tm, n // tn),
        in_specs=[pl.BlockSpec((tm, k), lambda i, j: (i, 0)),
                  pl.BlockSpec((1, k), lambda i, j: (0, 0)),
                  pl.BlockSpec((tn, k), lambda i, j: (j, 0))],
        out_specs=pl.BlockSpec((tm, tn), lambda i, j: (i, j)),
        out_shape=jax.ShapeDtypeStruct((m, n), F32),
        scratch_shapes=[pltpu.VMEM((tm, k), BF16)],
        compiler_params=_cparams(("parallel", "arbitrary")),
        name="norm_matmul",
    )(x, nw.reshape(1, k), wt)


def _outproj_body(oa_ref, ob_ref, wa_ref, wb_ref, x_ref, nw_ref, o_ref, *, tn):
    j = pl.program_id(1)
    y = jnp.dot(oa_ref[...], wa_ref[...], preferred_element_type=F32)
    y = y + jnp.dot(ob_ref[...], wb_ref[...], preferred_element_type=F32)
    o_ref[:, pl.ds(pl.multiple_of(j * tn, tn), tn)] = y

    @pl.when(j == pl.num_programs(1) - 1)
    def _():
        o_ref[...] = x_ref[...] + _rms(o_ref[...], nw_ref[...])


def outproj(oa, ob, w, x, nw, tm=512, tn=512):
    m, ka = oa.shape
    n = w.shape[1]
    return pl.pallas_call(
        functools.partial(_outproj_body, tn=tn),
        grid=(m // tm, n // tn),
        in_specs=[pl.BlockSpec((tm, ka), lambda i, j: (i, 0)),
                  pl.BlockSpec((tm, ka), lambda i, j: (i, 0)),
                  pl.BlockSpec((ka, tn), lambda i, j: (0, j)),
                  pl.BlockSpec((ka, tn), lambda i, j: (1, j)),
                  pl.BlockSpec((tm, n), lambda i, j: (i, 0), pipeline_mode=pl.Buffered(1)),
                  pl.BlockSpec((1, n), lambda i, j: (0, 0))],
        out_specs=pl.BlockSpec((tm, n), lambda i, j: (i, 0)),
        out_shape=jax.ShapeDtypeStruct((m, n), F32),
        compiler_params=_cparams(("parallel", "arbitrary")),
        name="outproj",
    )(oa, ob, w, w, x, nw.reshape(1, n))


FFN_HALO = 16


def _ffn_body(x_ref, xh_ref, nw_ref, wg_ref, wu_ref, cwg_ref, cwu_ref, cbg_ref, cbu_ref,
              wd_ref, pw_ref, o_ref, h_s, *, tm, seq):
    i = pl.program_id(0)
    j = pl.program_id(1)

    @pl.when(j == 0)
    def _():
        keep = jnp.where((i * tm) % seq == 0, 0.0, 1.0)
        h_s[0:FFN_HALO, :] = (_rms(xh_ref[...], nw_ref[...]) * keep).astype(BF16)
        h_s[FFN_HALO:, :] = _rms(x_ref[...], nw_ref[...]).astype(BF16)
        o_ref[...] = jnp.zeros_like(o_ref)

    h = h_s[...]

    def branch(w_ref, cw_ref, cb_ref):
        u = jnp.dot(h, w_ref[...], preferred_element_type=F32)
        cw = cw_ref[...]
        y = u * cw[2:3] + pltpu.roll(u, 1, 0) * cw[1:2] + pltpu.roll(u, 2, 0) * cw[0:1]
        return y[FFN_HALO:] + cb_ref[...]

    g = branch(wg_ref, cwg_ref, cbg_ref)
    u = branch(wu_ref, cwu_ref, cbu_ref)
    act = (_silu(g) * u).astype(BF16)
    o_ref[...] += jnp.dot(act, wd_ref[...], preferred_element_type=F32)

    @pl.when(j == pl.num_programs(1) - 1)
    def _():
        o_ref[...] = x_ref[...] + _rms(o_ref[...], pw_ref[...])


def conv_ffn(x, li, nw, w_up, conv_w, conv_b, w_down, pw, seq, tm=512, tf=256):
    m, k = x.shape
    nl, f, _ = w_down.shape
    nf = f // tf
    hb = tm // FFN_HALO
    cb = conv_b.reshape(nl, 1, 2 * f)
    row = pl.BlockSpec((None, 1, k), lambda i, j: (li, 0, 0))
    return pl.pallas_call(
        functools.partial(_ffn_body, tm=tm, seq=seq),
        grid=(m // tm, nf),
        in_specs=[pl.BlockSpec((tm, k), lambda i, j: (i, 0), pipeline_mode=pl.Buffered(1)),
                  pl.BlockSpec((FFN_HALO, k), lambda i, j: (jnp.maximum(i * hb - 1, 0), 0)),
                  row,
                  pl.BlockSpec((None, k, tf), lambda i, j: (li, 0, j)),
                  pl.BlockSpec((None, k, tf), lambda i, j: (li, 0, nf + j)),
                  pl.BlockSpec((None, 3, tf), lambda i, j: (li, 0, j)),
                  pl.BlockSpec((None, 3, tf), lambda i, j: (li, 0, nf + j)),
                  pl.BlockSpec((None, 1, tf), lambda i, j: (li, 0, j)),
                  pl.BlockSpec((None, 1, tf), lambda i, j: (li, 0, nf + j)),
                  pl.BlockSpec((None, tf, k), lambda i, j: (li, j, 0)),
                  row],
        out_specs=pl.BlockSpec((tm, k), lambda i, j: (i, 0)),
        out_shape=jax.ShapeDtypeStruct((m, k), F32),
        scratch_shapes=[pltpu.VMEM((tm + FFN_HALO, k), BF16)],
        compiler_params=_cparams(("parallel", "arbitrary")),
        name="conv_ffn",
    )(x, x, nw.reshape(nl, 1, k), w_up, w_up, conv_w, conv_w, cb, cb, w_down, pw.reshape(nl, 1, k))


def _ple_body(x_ref, p_ref, gnw_ref, wg_ref, wp_ref, pnw_ref, o_ref, h_s, *, tn):
    j = pl.program_id(1)

    @pl.when(j == 0)
    def _():
        h_s[...] = _rms(x_ref[...], gnw_ref[...]).astype(BF16)

    gate = jax.nn.sigmoid(jnp.dot(h_s[...], wg_ref[...], preferred_element_type=F32))
    proj = jnp.dot(p_ref[...].astype(BF16), wp_ref[...], preferred_element_type=F32)
    o_ref[:, pl.ds(pl.multiple_of(j * tn, tn), tn)] = gate * proj

    @pl.when(j == pl.num_programs(1) - 1)
    def _():
        o_ref[...] = x_ref[...] + _rms(o_ref[...], pnw_ref[...])


def ple(x, li, p, gnw, wg, wp, pnw, tm=512, tn=512):
    m, k = x.shape
    nl, _, dp = p.shape
    row = pl.BlockSpec((None, 1, k), lambda i, j: (li, 0, 0))
    return pl.pallas_call(
        functools.partial(_ple_body, tn=tn),
        grid=(m // tm, k // tn),
        in_specs=[pl.BlockSpec((tm, k), lambda i, j: (i, 0), pipeline_mode=pl.Buffered(1)),
                  pl.BlockSpec((None, tm, dp), lambda i, j: (li, i, 0)),
                  row,
                  pl.BlockSpec((None, k, tn), lambda i, j: (li, 0, j)),
                  pl.BlockSpec((None, dp, tn), lambda i, j: (li, 0, j)),
                  row],
        out_specs=pl.BlockSpec((tm, k), lambda i, j: (i, 0)),
        out_shape=jax.ShapeDtypeStruct((m, k), F32),
        scratch_shapes=[pltpu.VMEM((tm, k), BF16)],
        compiler_params=_cparams(("parallel", "arbitrary")),
        name="ple",
    )(x, p, gnw.reshape(nl, 1, k), wg, wp, pnw.reshape(nl, 1, k))


def _gdn_ltri():
    r = np.arange(GDN_GROUP)[:, None]
    t = np.arange(GDN_GROUP)[None, :]
    return jnp.asarray((((r // GDN_CHUNK) == (t // GDN_CHUNK)) & (r >= t)).astype(np.float32), dtype=BF16)


def _gdn_body(zq_ref, zk_ref, zv_ref, zg_ref, sm_ref, cwq_ref, cwk_ref, cwv_ref, prm_ref, nw_ref, ltri_ref,
              o_ref, q_s, k_s, v_s, g_s, b_s, o0_s, n0_s, gl_s, qe_s, m_s, *, seq):
    hd = pl.program_id(1)
    c = GDN_CHUNK
    gs = GDN_GROUP
    row = lax.broadcasted_iota(jnp.int32, (seq, LANES), 0)
    lane = lax.broadcasted_iota(jnp.int32, (seq, LANES), 1)

    def conv_silu(z_ref, w_ref):
        z = z_ref[0]
        w = w_ref[...]
        y = z * w[3:4]
        for s in (1, 2, 3):
            y = y + jnp.where(row >= s, pltpu.roll(z, s, 0), 0.0) * w[3 - s:4 - s]
        return _silu(y)

    q = conv_silu(zq_ref, cwq_ref)
    k = conv_silu(zk_ref, cwk_ref)
    q_s[...] = q * lax.rsqrt(jnp.sum(q * q, -1, keepdims=True) + EPS) * (HEAD_DIM ** -0.5)
    k_s[...] = k * lax.rsqrt(jnp.sum(k * k, -1, keepdims=True) + EPS)
    v_s[...] = conv_silu(zv_ref, cwv_ref)

    sm = sm_ref[0]
    glog = -jnp.exp(prm_ref[0:1, :]) * jax.nn.softplus(sm + prm_ref[1:2, :])
    g_s[...] = jnp.broadcast_to(_lane_col(glog, lane, hd), (seq, LANES))
    b_s[...] = jnp.broadcast_to(_lane_col(jax.nn.sigmoid(sm), lane, N_HEADS + hd), (seq, LANES))

    nper = gs // c
    rs = lax.broadcasted_iota(jnp.int32, (c, gs), 0)
    cs = lax.broadcasted_iota(jnp.int32, (c, gs), 1)
    jj = cs % c
    cblk = cs // c
    incl = rs >= jj
    strict = rs > jj
    eye_side = (rs == jj).astype(F32)
    r2 = lax.broadcasted_iota(jnp.int32, (gs, gs), 0)
    c2 = lax.broadcasted_iota(jnp.int32, (gs, gs), 1)
    bdm = ((r2 // c) == (c2 // c)).astype(F32)
    bdm_b = bdm.astype(BF16)
    lane_c = lax.broadcasted_iota(jnp.int32, (c, LANES), 1)
    lane_g = lax.broadcasted_iota(jnp.int32, (gs, LANES), 1)
    ones3 = (lane_c < 3).astype(F32).astype(BF16)

    def side_col(xb):
        left = jnp.where(lane_c < c, xb[0:c], xb[c:2 * c])
        right = jnp.where(lane_c < c, xb[2 * c:3 * c], xb[3 * c:4 * c])
        return jnp.concatenate([left, right], axis=1)

    def to_side(full):
        out = full[0:c]
        for ci in range(1, nper):
            out = jnp.where(cblk == ci, full[ci * c:(ci + 1) * c], out)
        return out

    def bd(xs, mask):
        return jnp.concatenate([xs] * nper, axis=0) * mask

    def mm3(ph, pl_, xh, xl):
        bh = bd(xh, bdm_b)
        lhs = jnp.concatenate([ph, ph, pl_], axis=1)
        rhs = jnp.concatenate([bh, bd(xl, bdm_b), bh], axis=0)
        return jnp.dot(lhs, rhs, preferred_element_type=F32)

    def group_pre(gi):
        off = pl.multiple_of(gi * gs, gs)
        qg = q_s[pl.ds(off, gs), :]
        kg = k_s[pl.ds(off, gs), :]
        bg = b_s[pl.ds(off, gs), :]
        gam = _dot_sel(ltri_ref[...], g_s[pl.ds(off, gs), :])
        g_hi = gam.astype(BF16).astype(F32)
        g_mid = (gam - g_hi).astype(BF16).astype(F32)
        g_lo = gam - g_hi - g_mid
        gam3 = jnp.where(lane_g == 0, g_hi, jnp.where(lane_g == 1, g_mid, jnp.where(lane_g == 2, g_lo, 0.0)))
        gam_row = lax.dot_general(ones3, gam3.astype(BF16), NT, preferred_element_type=F32)
        dec = jnp.where(incl, jnp.exp(jnp.where(incl, side_col(gam) - gam_row, 0.0)), 0.0)
        qb = qg.astype(BF16)
        kb = kg.astype(BF16)
        kk = to_side(lax.dot_general(kb, kb, NT, preferred_element_type=F32))
        a = jnp.where(strict, side_col(bg) * dec * kk, 0.0)
        return dict(off=off, qg=qg, kg=kg, bg=bg, gam=gam, dec=dec, qb=qb, kb=kb, a=a)

    def group_post(gi, st, tinv):
        off, qg, kg, bg, gam, dec, qb, kb = (st[n] for n in ("off", "qg", "kg", "bg", "gam", "dec", "qb", "kb"))
        vg = v_s[pl.ds(off, gs), :]
        th, tl = _split2(tinv)
        eg = jnp.exp(gam)
        rh, rl = _split2(jnp.concatenate([vg * bg, kg * (bg * eg)], axis=1))
        bth = bd(th, bdm_b)
        sol = jnp.dot(jnp.concatenate([bth, bth, bd(tl, bdm_b)], axis=1),
                      jnp.concatenate([rh, rl, rh], axis=0), preferred_element_type=F32)
        solb = sol.astype(BF16)
        qk = bd(dec, bdm) * lax.dot_general(qb, kb, NT, preferred_element_type=F32)
        qkuw = jnp.dot(qk.astype(BF16), solb, preferred_element_type=F32)
        o0_s[pl.ds(off, gs), :] = qkuw[:, :HEAD_DIM]
        qe_s[pl.ds(off, gs), :] = (qg * eg - qkuw[:, HEAD_DIM:]).astype(BF16)
        for ci in range(nper):
            lo = ci * c
            gl = gam[lo + c - 1:lo + c, :]
            kd = (kg[lo:lo + c] * jnp.exp(gl - gam[lo:lo + c])).astype(BF16)
            kds = lax.dot_general(kd, solb[lo:lo + c], TN, preferred_element_type=F32)
            so = pl.multiple_of((gi * nper + ci) * HEAD_DIM, HEAD_DIM)
            n0_s[pl.ds(so, HEAD_DIM), :] = kds[:, :HEAD_DIM]
            m_s[pl.ds(so, HEAD_DIM), :] = (-kds[:, HEAD_DIM:]).astype(BF16)
            gl_s[pl.ds(pl.multiple_of((gi * nper + ci) * 8, 8), 8), :] = jnp.broadcast_to(jnp.exp(gl), (8, LANES))

    ngroups = seq // gs
    nlock = GDN_LOCKSTEP if ngroups % GDN_LOCKSTEP == 0 else 1

    def groups(it, carry):
        gis = [it + u * (ngroups // nlock) for u in range(nlock)]
        sts = [group_pre(gi) for gi in gis]
        tinvs = [eye_side - st["a"] for st in sts]
        xs = [_split2(st["a"]) for st in sts]
        for _ in range(5):
            xs = [_split2(mm3(xh, xl, xh, xl)) for xh, xl in xs]
            ts = [_split2(t) for t in tinvs]
            tinvs = [t + mm3(th, tl, xh, xl) for t, (th, tl), (xh, xl) in zip(tinvs, ts, xs)]
        for gi, st, t in zip(gis, sts, tinvs):
            group_post(gi, st, t)
        return carry

    lax.fori_loop(0, ngroups // nlock, groups, 0)

    def step(n, s):
        off = pl.multiple_of(n * c, c)
        so = pl.multiple_of(n * HEAD_DIM, HEAD_DIM)
        sb = s.astype(BF16)
        o0_s[pl.ds(off, c), :] = o0_s[pl.ds(off, c), :] + jnp.dot(
            qe_s[pl.ds(off, c), :], sb, preferred_element_type=F32)
        return (gl_s[pl.ds(pl.multiple_of(n * 8, 8), 1), :] * s + n0_s[pl.ds(so, HEAD_DIM), :]
                + jnp.dot(m_s[pl.ds(so, HEAD_DIM), :], sb, preferred_element_type=F32))

    lax.fori_loop(0, seq // c, step, jnp.zeros((HEAD_DIM, HEAD_DIM), F32))
    o_ref[0] = (_rms(o0_s[...], nw_ref[...]) * _silu(zg_ref[0])).astype(BF16)


def gated_deltanet(z3, conv_w, prm, norm_w):
    bsz, seq, _ = z3.shape
    zspec = lambda cb: pl.BlockSpec((1, seq, LANES), lambda b, h: (b, 0, cb + h))
    wspec = lambda cb: pl.BlockSpec((4, LANES), lambda b, h: (0, cb + h))
    nchunk = seq // GDN_CHUNK
    return pl.pallas_call(
        functools.partial(_gdn_body, seq=seq),
        grid=(bsz, N_HEADS),
        in_specs=[zspec(AB_QKV), zspec(AB_QKV + 16), zspec(AB_QKV + 32), zspec(AB_GATE),
                  pl.BlockSpec((1, seq, LANES), lambda b, h: (b, 0, AB_SMALL)),
                  wspec(0), wspec(16), wspec(32),
                  pl.BlockSpec((8, LANES), lambda b, h: (0, 0)),
                  pl.BlockSpec((1, LANES), lambda b, h: (0, 0)),
                  pl.BlockSpec((GDN_GROUP, GDN_GROUP), lambda b, h: (0, 0))],
        out_specs=pl.BlockSpec((1, seq, LANES), lambda b, h: (b, 0, h)),
        out_shape=jax.ShapeDtypeStruct((bsz, seq, N_HEADS * HEAD_DIM), BF16),
        scratch_shapes=[pltpu.VMEM((seq, LANES), F32) for _ in range(6)]
        + [pltpu.VMEM((nchunk * HEAD_DIM, HEAD_DIM), F32),
           pltpu.VMEM((nchunk * 8, LANES), F32),
           pltpu.VMEM((seq, LANES), BF16),
           pltpu.VMEM((nchunk * HEAD_DIM, HEAD_DIM), BF16)],
        compiler_params=_cparams(("parallel", "parallel")),
        name="gated_deltanet",
    )(z3, z3, z3, z3, z3, conv_w, conv_w, conv_w, prm, norm_w.reshape(1, HEAD_DIM), _gdn_ltri())


HGRN_LEVELS = tuple(HGRN_CHUNK >> (i + 1) for i in range(8))


def _hgrn_selectors():
    c = HGRN_CHUNK
    r = np.arange(c)[:, None]
    t = np.arange(c)[None, :]
    mats = [r >= t]
    for s in HGRN_LEVELS:
        isq = ((r // s) % 2) == 1
        mats.append(((r // s) == (t // s)) & ((isq & (t <= r)) | (~isq & (t > r))))
    return jnp.asarray(np.concatenate(mats, axis=0).astype(np.float32), dtype=BF16)


def _hgrn_body(zq_ref, zf_ref, zi_ref, zg_ref, lb_ref, nw_ref, sel_ref, o_ref, oi_s, ebl_s, qd_s, kd_s, *, seq):
    c = HGRN_CHUNK
    r = lax.broadcasted_iota(jnp.int32, (c, c), 0)
    cc = lax.broadcasted_iota(jnp.int32, (c, c), 1)
    diag = r == cc
    x = r ^ cc
    lev = jnp.zeros((c, c), jnp.int32)
    for bit in range(1, 8):
        lev = lev + (x >= (1 << bit)).astype(jnp.int32)
    lev = jnp.where(r > cc, lev, -1)
    rr = lax.broadcasted_iota(jnp.int32, (c, LANES), 0)
    lb = lb_ref[...]

    nchunk = seq // c
    nlock = HGRN_LOCKSTEP if nchunk % HGRN_LOCKSTEP == 0 else 1

    def intra(it, carry):
        ns = [it + u * (nchunk // nlock) for u in range(nlock)]
        offs = [pl.multiple_of(n * c, c) for n in ns]
        fxs = [zf_ref[0, pl.ds(off, c), :] for off in offs]
        qs_ = [_silu(zq_ref[0, pl.ds(off, c), :]) for off in offs]
        lfs = [jnp.log(lb + (1.0 - lb) * jax.nn.sigmoid(fx)) for fx in fxs]
        ks_ = [(1.0 - lb) * jax.nn.sigmoid(-fx) for fx in fxs]
        sums = [_dot_sel(sel_ref[...], lf) for lf in lfs]
        atts = [jnp.where(diag, jnp.sum(q * k, -1, keepdims=True), 0.0) for q, k in zip(qs_, ks_)]
        for i, s in enumerate(HGRN_LEVELS):
            isq = ((rr // s) % 2) == 1
            parts = []
            for q, k, sm in zip(qs_, ks_, sums):
                e = jnp.exp(sm[(i + 1) * c:(i + 2) * c])
                parts.append(lax.dot_general(jnp.where(isq, q * e, 0.0).astype(BF16),
                                             jnp.where(isq, 0.0, k * e).astype(BF16), NT,
                                             preferred_element_type=F32))
            atts = [jnp.where(lev == (7 - i), part, att) for part, att in zip(parts, atts)]
        for n, off, q, k, sm, att in zip(ns, offs, qs_, ks_, sums, atts):
            b = sm[0:c]
            bl = b[c - 1:c, :]
            vb = zi_ref[0, pl.ds(off, c), :].astype(BF16)
            oi_s[pl.ds(off, c), :] = jnp.dot(att.astype(BF16), vb, preferred_element_type=F32)
            qd_s[pl.ds(off, c), :] = (q * jnp.exp(b)).astype(BF16)
            kd_s[pl.ds(off, c), :] = (k * jnp.exp(bl - b)).astype(BF16)
            ebl_s[pl.ds(pl.multiple_of(n * 8, 8), 8), :] = jnp.broadcast_to(jnp.exp(bl), (8, LANES))
        return carry

    lax.fori_loop(0, nchunk // nlock, intra, 0)

    def scan(n, st):
        off = pl.multiple_of(n * c, c)
        oi_s[pl.ds(off, c), :] = oi_s[pl.ds(off, c), :] + lax.dot_general(
            qd_s[pl.ds(off, c), :], st.astype(BF16), NT, preferred_element_type=F32)
        vb = zi_ref[0, pl.ds(off, c), :].astype(BF16)
        return (st * ebl_s[pl.ds(pl.multiple_of(n * 8, 8), 1), :]
                + lax.dot_general(vb, kd_s[pl.ds(off, c), :], TN, preferred_element_type=F32))

    lax.fori_loop(0, nchunk, scan, jnp.zeros((HEAD_DIM, HEAD_DIM), F32))
    o_ref[0] = (_rms(oi_s[...], nw_ref[...]) * _silu(zg_ref[0])).astype(BF16)


def hgrn2(z3, lb, norm_w):
    bsz, seq, _ = z3.shape
    zspec = lambda cb: pl.BlockSpec((1, seq, LANES), lambda b, h: (b, 0, cb + h))
    sel = _hgrn_selectors()
    return pl.pallas_call(
        functools.partial(_hgrn_body, seq=seq),
        grid=(bsz, N_HEADS),
        in_specs=[zspec(CD_Q), zspec(CD_F), zspec(CD_I), zspec(CD_G),
                  pl.BlockSpec((1, LANES), lambda b, h: (0, h)),
                  pl.BlockSpec((1, LANES), lambda b, h: (0, 0)),
                  pl.BlockSpec(sel.shape, lambda b, h: (0, 0))],
        out_specs=pl.BlockSpec((1, seq, LANES), lambda b, h: (b, 0, h)),
        out_shape=jax.ShapeDtypeStruct((bsz, seq, N_HEADS * HEAD_DIM), BF16),
        scratch_shapes=[pltpu.VMEM((seq, LANES), F32), pltpu.VMEM((seq // HGRN_CHUNK * 8, LANES), F32),
                        pltpu.VMEM((seq, LANES), BF16), pltpu.VMEM((seq, LANES), BF16)],
        compiler_params=_cparams(("parallel", "parallel")),
        name="hgrn2",
    )(z3, z3, z3, z3, lb.reshape(1, N_HEADS * HEAD_DIM), norm_w.reshape(1, HEAD_DIM), sel)


FOX_BLK = 256


def _fox_gate_body(sm_ref, bias_ref, cumt_ref, *, seq):
    c = FOX_BLK
    r = lax.broadcasted_iota(jnp.int32, (c, c), 0)
    cc = lax.broadcasted_iota(jnp.int32, (c, c), 1)
    ltri = (r >= cc).astype(F32)
    r1 = lax.broadcasted_iota(jnp.int32, (LANES, LANES), 0)
    c1 = lax.broadcasted_iota(jnp.int32, (LANES, LANES), 1)
    eye = (r1 == c1).astype(F32)

    def blk(n, carry):
        off = pl.multiple_of(n * c, c)
        ls = jax.nn.log_sigmoid(sm_ref[0, pl.ds(off, c), :] + bias_ref[...])
        cum = carry + jnp.dot(ltri, ls, precision=HI, preferred_element_type=F32)
        cumt_ref[0, :, pl.ds(off, c)] = lax.dot_general(eye, cum, NT, precision=HI, preferred_element_type=F32)
        return cum[c - 1:c, :]

    lax.fori_loop(0, seq // c, blk, jnp.zeros((1, LANES), F32))


def fox_gates(z3, bias_row):
    bsz, seq, _ = z3.shape
    return pl.pallas_call(
        functools.partial(_fox_gate_body, seq=seq),
        grid=(bsz,),
        in_specs=[pl.BlockSpec((1, seq, LANES), lambda b: (b, 0, CD_SMALL)),
                  pl.BlockSpec((1, LANES), lambda b: (0, 0))],
        out_specs=pl.BlockSpec((1, LANES, seq), lambda b: (b, 0, 0)),
        out_shape=jax.ShapeDtypeStruct((bsz, LANES, seq), F32),
        compiler_params=_cparams(("parallel",)),
        name="fox_gates",
    )(z3, bias_row)


ATT_ROWS = 128
ATT_TK = 256


def _for_blocks(n, unroll, body, carry=None):
    if isinstance(n, int):
        for j in range(n):
            carry = body(j, carry)
        return carry

    def it(i, c):
        for u in range(unroll):
            c = body(i * unroll + u, c)
        return c

    return lax.fori_loop(0, n // unroll, it, carry)


def _softmax_pv(s_scr, nblk, unroll, v_fn, m_scr, l_scr, acc_scr):
    _, rows, bw = s_scr.shape
    nlb = bw // LANES
    for ch in range(rows // ATT_ROWS):
        rs = slice(ch * ATT_ROWS, (ch + 1) * ATT_ROWS)

        def pmax(j, mel):
            for b in range(nlb):
                mel = jnp.maximum(mel, s_scr[j, rs, b * LANES:(b + 1) * LANES])
            return mel

        mel = _for_blocks(nblk, unroll, pmax, jnp.full((ATT_ROWS, LANES), NEG, F32))
        m_scr[rs, :] = jnp.broadcast_to(jnp.max(mel, -1, keepdims=True), (ATT_ROWS, LANES))
    l_scr[...] = jnp.zeros_like(l_scr)
    acc_scr[...] = jnp.zeros_like(acc_scr)

    def ppv(j, c):
        pcs = []
        for ch in range(rows // ATT_ROWS):
            rs = slice(ch * ATT_ROWS, (ch + 1) * ATT_ROWS)
            mb = m_scr[rs, :]
            lacc = l_scr[rs, :]
            ps = []
            for b in range(nlb):
                pb = jnp.exp(s_scr[j, rs, b * LANES:(b + 1) * LANES] - mb)
                lacc = lacc + pb
                ps.append(pb.astype(BF16))
            l_scr[rs, :] = lacc
            pcs.append(jnp.concatenate(ps, axis=1))
        acc_scr[...] += jnp.dot(jnp.concatenate(pcs, axis=0), v_fn(j), preferred_element_type=F32)
        return c

    _for_blocks(nblk, unroll, ppv)
    return acc_scr[...] / jnp.sum(l_scr[...], -1, keepdims=True)


def _softmax_scratch(rows, nblk, bw):
    return [pltpu.VMEM((nblk, rows, bw), F32), pltpu.VMEM((rows, LANES), F32),
            pltpu.VMEM((rows, LANES), F32), pltpu.VMEM((rows, HEAD_DIM), F32)]


def _fox_attn_body(q_ref, k_ref, v_ref, cumt_ref, o_ref, s_scr, m_scr, l_scr, acc_scr, *, tq):
    hd = pl.program_id(1)
    qi = pl.program_id(2)
    qall = (q_ref[0] * (HEAD_DIM ** -0.5)).astype(BF16)

    def scores(j):
        off = pl.multiple_of(j * tq, tq)
        kj = k_ref[0, pl.ds(off, tq), :].astype(BF16)
        ck = cumt_ref[0, pl.ds(hd % 8, 1), pl.ds(off, tq)]
        return lax.dot_general(qall, kj, NT, preferred_element_type=F32) - ck

    row = lax.broadcasted_iota(jnp.int32, (tq, tq), 0)
    col = lax.broadcasted_iota(jnp.int32, (tq, tq), 1)

    def v_blk(j):
        return v_ref[0, pl.ds(pl.multiple_of(j * tq, tq), tq), :].astype(BF16)

    def tile(nfull):
        for j in range(nfull):
            s_scr[j] = scores(j)
        s_scr[nfull] = jnp.where(col <= row, scores(nfull), NEG)
        o_ref[0] = _softmax_pv(s_scr, nfull + 1, 1, v_blk, m_scr, l_scr, acc_scr).astype(BF16)

    lax.switch(qi, [functools.partial(tile, n) for n in range(s_scr.shape[0])])


def fox_attention(z3, cumt):
    bsz, seq, _ = z3.shape
    tq = min(512, seq)
    full = lambda cb: pl.BlockSpec((1, seq, LANES), lambda b, h, i: (b, 0, cb + h))
    return pl.pallas_call(
        functools.partial(_fox_attn_body, tq=tq),
        grid=(bsz, N_HEADS, seq // tq),
        in_specs=[pl.BlockSpec((1, tq, LANES), lambda b, h, i: (b, i, CD_FQKV + h)),
                  full(CD_FQKV + 16), full(CD_FQKV + 32),
                  pl.BlockSpec((1, 8, seq), lambda b, h, i: (b, h // 8, 0))],
        out_specs=pl.BlockSpec((1, tq, LANES), lambda b, h, i: (b, i, h)),
        out_shape=jax.ShapeDtypeStruct((bsz, seq, N_HEADS * HEAD_DIM), BF16),
        scratch_shapes=_softmax_scratch(tq, seq // tq, tq),
        compiler_params=_cparams(("parallel", "parallel", "arbitrary")),
        name="fox_attention",
    )(z3, z3, z3, cumt)


def _nsa_cmp_body(zk_ref, zv_ref, pek_ref, pev_ref, wk1_ref, wk2_ref, wv1_ref, wv2_ref, kc_ref, vc_ref, *, nblk):
    def compress(z_ref, pe_ref, w1_ref, w2_ref):
        u1 = jnp.zeros((nblk, HEAD_DIM), F32)
        u2 = jnp.zeros((nblk, HEAD_DIM), F32)
        for l in range(S_CMP):
            zl = z_ref[0, :, l, :]
            u1 = u1 + jnp.dot(zl, w1_ref[l * HEAD_DIM:(l + 1) * HEAD_DIM, :], precision=HI,
                              preferred_element_type=F32)
            u2 = u2 + jnp.dot(zl, w1_ref[(S_CMP + l) * HEAD_DIM:(S_CMP + l + 1) * HEAD_DIM, :], precision=HI,
                              preferred_element_type=F32)
        pe = jnp.dot(jnp.broadcast_to(pe_ref[...], (8, L_CMP * HEAD_DIM)), w1_ref[...], precision=HI,
                     preferred_element_type=F32)[0:1]
        hmid = u1 + pltpu.roll(u2, nblk - 1, 0) + pe
        return jnp.dot(_silu(hmid), w2_ref[...], precision=HI, preferred_element_type=F32)

    kc_ref[0, 0] = compress(zk_ref, pek_ref, wk1_ref, wk2_ref)
    vc_ref[0, 0] = compress(zv_ref, pev_ref, wv1_ref, wv2_ref)


def nsa_compress(z3, pe_k, pe_v, wk1, wk2, wv1, wv2):
    bsz, seq, n = z3.shape
    nblk = seq // S_CMP
    z4 = z3.reshape(bsz, nblk, S_CMP, n)
    zspec = lambda cb: pl.BlockSpec((1, nblk, S_CMP, LANES), lambda b, g: (b, 0, 0, cb + g))
    wfull = lambda a: pl.BlockSpec(a.shape, lambda b, g: (0, 0))
    pek = pe_k.reshape(1, L_CMP * HEAD_DIM)
    pev = pe_v.reshape(1, L_CMP * HEAD_DIM)
    out = jax.ShapeDtypeStruct((bsz, G_NSA, nblk, HEAD_DIM), F32)
    ospec = pl.BlockSpec((1, 1, nblk, HEAD_DIM), lambda b, g: (b, g, 0, 0))
    return pl.pallas_call(
        functools.partial(_nsa_cmp_body, nblk=nblk),
        grid=(bsz, G_NSA),
        in_specs=[zspec(AB_NKV), zspec(AB_NKV + G_NSA), wfull(pek), wfull(pev),
                  wfull(wk1), wfull(wk2), wfull(wv1), wfull(wv2)],
        out_specs=[ospec, ospec],
        out_shape=[out, out],
        compiler_params=_cparams(("parallel", "parallel")),
        name="nsa_compress",
    )(z4, z4, pek, pev, wk1, wk2, wv1, wv2)


NSA_TQ = 128


def _nsa_expand(seq):
    m = np.arange(LANES)[:, None]
    t = np.arange(seq)[None, :]
    return jnp.asarray(((t // L_SLC) == m).astype(np.float32), dtype=BF16)


NSA_UNROLL = 2
NSA_WIN_BLOCKS = max(e // ATT_TK - max(e - (WINDOW + NSA_TQ - 2), 0) // ATT_TK + 1
                     for e in range(NSA_TQ - 1, 8 * WINDOW, NSA_TQ))


def _nsa_attn_body(q_ref, kc_ref, vc_ref, ks_ref, vs_ref, kw_ref, vw_ref, sm_ref, exp_ref, o_ref,
                   s_scr, m_scr, l_scr, acc_scr, *, seq):
    g = pl.program_id(1)
    qi = pl.program_id(2)
    tq = NSA_TQ
    rows = HPG_NSA * tq
    nblk = seq // S_CMP
    n_slc = seq // L_SLC
    n_top = min(N_SEL, n_slc)
    q0 = qi * tq

    qall = q_ref[0]
    qs = jnp.concatenate([qall[:, p * HEAD_DIM:(p + 1) * HEAD_DIM] for p in range(HPG_NSA)], axis=0)
    qs = qs * (HEAD_DIM ** -0.5)
    qb = qs.astype(BF16)
    tpos = q0 + lax.broadcasted_iota(jnp.int32, (rows, 1), 0) % tq

    kc = kc_ref[0, 0]
    vc = vc_ref[0, 0]
    qh, ql = _split2(qs)
    kh, kl = _split2(kc)
    sc = lax.dot_general(jnp.concatenate([qh, qh, ql], axis=1), jnp.concatenate([kh, kl, kh], axis=1), NT,
                         preferred_element_type=F32)
    nidx = lax.broadcasted_iota(jnp.int32, (rows, nblk), 1)
    cmask = (nidx * S_CMP + (L_CMP - 1) <= tpos) & (nidx <= nblk - 2)
    scm = jnp.where(cmask, sc, NEG)
    mc = jnp.max(scm, -1, keepdims=True)
    ec = jnp.where(cmask, jnp.exp(scm - mc), 0.0)
    dc = jnp.sum(ec, -1, keepdims=True)
    p_cmp = ec / jnp.where(dc > 0, dc, 1.0)
    o_cmp = jnp.dot(p_cmp.astype(BF16), vc.astype(BF16), preferred_element_type=F32)

    psum = p_cmp[0:tq]
    for p in range(1, HPG_NSA):
        psum = psum + p_cmp[p * tq:(p + 1) * tq]
    cs = lax.broadcasted_iota(jnp.int32, (nblk, LANES), 0) * S_CMP
    ss = lax.broadcasted_iota(jnp.int32, (nblk, LANES), 1) * L_SLC
    overlap = ((cs < ss + L_SLC) & (cs + L_CMP > ss) & (cs <= seq - L_CMP) & (ss < seq)).astype(F32).astype(BF16)
    ph, pl_ = _split2(psum)
    imp = jnp.dot(jnp.concatenate([ph, pl_], axis=1), jnp.concatenate([overlap, overlap], axis=0),
                  preferred_element_type=F32)
    blk = lax.broadcasted_iota(jnp.int32, (tq, LANES), 1)
    cur = (q0 + lax.broadcasted_iota(jnp.int32, (tq, LANES), 0)) // L_SLC
    valid = blk <= cur
    forced = (blk == 0) | (blk == cur) | (blk == cur - 1)
    score = jnp.where(valid, jnp.where(forced, FORCE_SCORE, imp), NEG)
    rank = jnp.zeros((tq, LANES), jnp.int32)
    for mp in range(n_slc):
        colv = score[:, mp:mp + 1]
        rank = rank + ((colv > score) | ((colv == score) & (mp < blk))).astype(jnp.int32)
    selb = (valid & (rank < n_top)).astype(F32).astype(BF16)

    tk = ATT_TK
    qrow = q0 + lax.broadcasted_iota(jnp.int32, (tq, tk), 0)
    col = lax.broadcasted_iota(jnp.int32, (tq, tk), 1)

    def branch(k_ref, v_ref, first, nb, bias_fn):
        def scores(j, c):
            off = pl.multiple_of((first + j) * tk, tk)
            s = lax.dot_general(qb, k_ref[0, pl.ds(off, tk), :].astype(BF16), NT,
                                preferred_element_type=F32)
            bias = bias_fn(off)
            for p in range(HPG_NSA):
                s_scr[j, p * tq:(p + 1) * tq, :] = s[p * tq:(p + 1) * tq] + bias
            return c

        _for_blocks(nb, NSA_UNROLL, scores)

        def v_blk(j):
            return v_ref[0, pl.ds(pl.multiple_of((first + j) * tk, tk), tk), :].astype(BF16)

        return _softmax_pv(s_scr, nb, NSA_UNROLL, v_blk, m_scr, l_scr, acc_scr)

    def slc_bias(off):
        sel = jnp.dot(selb, exp_ref[:, pl.ds(off, tk)], preferred_element_type=F32)
        return jnp.where((sel > 0.5) & (off + col <= qrow), 0.0, NEG)

    last = (q0 + tq - 1) // tk
    o_slc = branch(ks_ref, vs_ref, 0, (last // NSA_UNROLL + 1) * NSA_UNROLL, slc_bias)

    def win_bias(off):
        d = qrow - (off + col)
        return jnp.where((d >= 0) & (d < WINDOW), 0.0, NEG)

    nwin = min(NSA_WIN_BLOCKS, seq // tk)
    o_win = branch(kw_ref, vw_ref, jnp.maximum(last - (nwin - 1), 0), nwin, win_bias)

    sg = jax.nn.sigmoid(sm_ref[0])
    for p in range(HPG_NSA):
        sl = slice(p * tq, (p + 1) * tq)
        base = 2 * N_HEADS + g * HPG_NSA + p
        o = (_lane_col(sg, blk, base) * o_cmp[sl] + _lane_col(sg, blk, base + N_HEADS) * o_slc[sl]
             + _lane_col(sg, blk, base + 2 * N_HEADS) * o_win[sl])
        o_ref[0, :, p * HEAD_DIM:(p + 1) * HEAD_DIM] = o.astype(BF16)


def nsa_attention(z3, kc, vc):
    bsz, seq, _ = z3.shape
    tq = NSA_TQ
    nblk = seq // S_CMP
    full = lambda cb: pl.BlockSpec((1, seq, LANES), lambda b, g, i: (b, 0, cb + g))
    cspec = pl.BlockSpec((1, 1, nblk, HEAD_DIM), lambda b, g, i: (b, g, 0, 0))
    qw = HPG_NSA * HEAD_DIM
    return pl.pallas_call(
        functools.partial(_nsa_attn_body, seq=seq),
        grid=(bsz, G_NSA, seq // tq),
        in_specs=[pl.BlockSpec((1, tq, qw), lambda b, g, i: (b, i, AB_NQ * LANES // qw + g)),
                  cspec, cspec,
                  full(AB_NKV + 2 * G_NSA), full(AB_NKV + 3 * G_NSA),
                  full(AB_NKV + 4 * G_NSA), full(AB_NKV + 5 * G_NSA),
                  pl.BlockSpec((1, tq, LANES), lambda b, g, i: (b, i, AB_SMALL)),
                  pl.BlockSpec((LANES, seq), lambda b, g, i: (0, 0))],
        out_specs=pl.BlockSpec((1, tq, qw), lambda b, g, i: (b, i, g)),
        out_shape=jax.ShapeDtypeStruct((bsz, seq, N_HEADS * HEAD_DIM), BF16),
        scratch_shapes=_softmax_scratch(HPG_NSA * tq, seq // ATT_TK, ATT_TK),
        compiler_params=_cparams(("parallel", "parallel", "arbitrary")),
        name="nsa_attention",
    )(z3, kc, vc, z3, z3, z3, z3, z3, _nsa_expand(seq))


def _repack_body(w_ref, o_ref, *, sections, pad_from):
    o_ref[pad_from:, :] = jnp.zeros((o_ref.shape[0] - pad_from, o_ref.shape[1]), BF16)
    for dst, src, width in sections:
        o_ref[dst:dst + width, :] = w_ref[src:src + width, :].astype(BF16)


def _repack_weight(w, sections, n_out, pad_from, tc=256):
    wt = jnp.swapaxes(w, 0, 1)
    n_in, k = wt.shape
    return pl.pallas_call(
        functools.partial(_repack_body, sections=sections, pad_from=pad_from),
        grid=(k // tc,),
        in_specs=[pl.BlockSpec((n_in, tc), lambda i: (0, i))],
        out_specs=pl.BlockSpec((n_out, tc), lambda i: (0, i)),
        out_shape=jax.ShapeDtypeStruct((n_out, k), BF16),
        compiler_params=_cparams(("parallel",)),
        name="repack_weight",
    )(wt)


def _ab_weight(w):
    small = AB_SMALL * LANES
    sections = ((0, 0, 6144), (6144, 6176, 2048), (8192, 8224, 2048), (10240, 10272, 3072),
                (small, 6144, 32), (small + 32, 13344, 48))
    return _repack_weight(w, sections, AB_N, small)


def _cd_weight(w):
    small = CD_SMALL * LANES
    return _repack_weight(w, ((0, 0, small), (small, small, 16)), CD_N, small)


def _row128(v):
    return jnp.pad(v.astype(F32), (0, LANES - v.shape[0])).reshape(1, LANES)


def kernel(x, p, ab_norm_pre, ab_norm_post, ab_w_in, gdn_conv_w, gdn_a_log, gdn_dt_bias, gdn_norm, nsa_pe_k, nsa_pe_v, nsa_cmp_k1, nsa_cmp_k2, nsa_cmp_v1, nsa_cmp_v2, ab_w_out, cd_norm_pre, cd_norm_post, cd_w_in, hgrn_lb_logits, hgrn_norm, fox_f_bias, cd_w_out, ffn_norm_pre, ffn_norm_post, ffn_w_up, ffn_conv_w, ffn_conv_b, ffn_w_down, ple_w_proj, ple_gate_norm, ple_w_gate, ple_norm_post):
    bsz, seq, dm = x.shape
    depth = p.shape[0]
    m = bsz * seq
    half = N_HEADS * HEAD_DIM
    sm_ = jax.nn.softmax(hgrn_lb_logits.astype(F32), axis=0)
    lb_table = jnp.cumsum(sm_, axis=0) - sm_[0]
    xf = x.reshape(m, dm)
    pf = p.reshape(depth, m, -1)
    w_up, w_down = ffn_w_up.astype(BF16), ffn_w_down.astype(BF16)
    w_gate, w_proj = ple_w_gate.astype(BF16), ple_w_proj.astype(BF16)
    for li in range(depth):
        j = li // 2
        if li % 2 == 0:
            z3 = norm_matmul(xf, ab_norm_pre[j], _ab_weight(ab_w_in[j])).reshape(bsz, seq, AB_N)
            prm = jnp.concatenate([_row128(gdn_a_log[j]), _row128(gdn_dt_bias[j]),
                                   jnp.zeros((6, LANES), F32)], axis=0)
            o_a = gated_deltanet(z3, gdn_conv_w[j], prm, gdn_norm[j])
            kc, vc = nsa_compress(z3, nsa_pe_k[j], nsa_pe_v[j], nsa_cmp_k1[j], nsa_cmp_k2[j],
                                  nsa_cmp_v1[j], nsa_cmp_v2[j])
            o_b = nsa_attention(z3, kc, vc)
            w_out, post = ab_w_out[j], ab_norm_post[j]
        else:
            z3 = norm_matmul(xf, cd_norm_pre[j], _cd_weight(cd_w_in[j])).reshape(bsz, seq, CD_N)
            o_a = hgrn2(z3, lb_table[li], hgrn_norm[j])
            o_b = fox_attention(z3, fox_gates(z3, _row128(fox_f_bias[j])))
            w_out, post = cd_w_out[j], cd_norm_post[j]
        xf = outproj(o_a.reshape(m, half), o_b.reshape(m, half), w_out.astype(BF16), xf, post)
        xf = conv_ffn(xf, li, ffn_norm_pre, w_up, ffn_conv_w, ffn_conv_b, w_down, ffn_norm_post, seq)
        xf = ple(xf, li, pf, ple_gate_norm, w_gate, w_proj, ple_norm_post)
    return xf.reshape(bsz, seq, dm)
```

```python
import functools

import numpy as np
import jax
import jax.numpy as jnp
from jax import lax
from jax.experimental import pallas as pl
from jax.experimental.pallas import tpu as pltpu

F32 = jnp.float32
BF16 = jnp.bfloat16
HI = lax.Precision.HIGHEST
NT = (((1,), (1,)), ((), ()))
TN = (((0,), (0,)), ((), ()))

D_MODEL = 4096
HEAD_DIM = 128
N_HEADS = 16
G_NSA = 4
HPG_NSA = 4
L_CMP = 32
S_CMP = 16
L_SLC = 64
N_SEL = 8
WINDOW = 512
FORCE_SCORE = 1e4
GDN_CHUNK = 64
GDN_GROUP = 256
GDN_LOCKSTEP = 8
GDN_HEADS = 2
HGRN_CHUNK = 256
HGRN_LOCKSTEP = 2
D_FF = 11008
EPS = 1e-6
NEG = -1e30
LANES = 128
SMALL_W = 512

AB_QKV, AB_GATE, AB_NQ, AB_NKV, AB_SMALL = 0, 48, 64, 80, 104
AB_N = 108 * LANES
CD_Q, CD_F, CD_I, CD_G, CD_FQKV, CD_SMALL = 0, 16, 32, 48, 64, 112
CD_N = 116 * LANES

VMEM_LIMIT = 56 * 1024 * 1024


def _cparams(sem):
    return pltpu.CompilerParams(dimension_semantics=sem, vmem_limit_bytes=VMEM_LIMIT)


def _rms(x, w):
    return x * lax.rsqrt(jnp.mean(x * x, axis=-1, keepdims=True) + EPS) * w


def _silu(x):
    return x * jax.nn.sigmoid(x)


def _lane_col(x, lane_idx, lane):
    return jnp.sum(jnp.where(lane_idx == lane, x, 0.0), axis=-1, keepdims=True)


def _split2(x):
    hi = x.astype(BF16)
    return hi, (x - hi.astype(F32)).astype(BF16)


def _dot_sel(sel, x):
    n = x.shape[1]
    hi, lo = _split2(x)
    y = jnp.dot(sel, jnp.concatenate([hi, lo], axis=1), preferred_element_type=F32)
    return y[:, :n] + y[:, n:]


def _norm_matmul_body(x_ref, nw_ref, w_ref, o_ref, h_ref):
    @pl.when(pl.program_id(1) == 0)
    def _():
        h_ref[...] = _rms(x_ref[...], nw_ref[...]).astype(BF16)

    o_ref[...] = lax.dot_general(h_ref[...], w_ref[...], NT, preferred_element_type=F32)


def norm_matmul(x, nw, wt, tm=512, tn=512):
    m, k = x.shape
    n = wt.shape[0]
    return pl.pallas_call(
        _norm_matmul_body,
        grid=(m // tm, n // tn),
        in_specs=[pl.BlockSpec((tm, k), lambda i, j: (i, 0)),
                  pl.BlockSpec((1, k), lambda i, j: (0, 0)),
                  pl.BlockSpec((tn, k), lambda i, j: (j, 0))],
        out_specs=pl.BlockSpec((tm, tn), lambda i, j: (i, j)),
        out_shape=jax.ShapeDtypeStruct((m, n), F32),
        scratch_shapes=[pltpu.VMEM((tm, k), BF16)],
        compiler_params=_cparams(("parallel", "arbitrary")),
        name="norm_matmul",
    )(x, nw.reshape(1, k), wt)


def _outproj_body(oa_ref, ob_ref, wa_ref, wb_ref, x_ref, nw_ref, o_ref, *, tn):
    j = pl.program_id(1)
    y = jnp.dot(oa_ref[...], wa_ref[...], preferred_element_type=F32)
    y = y + jnp.dot(ob_ref[...], wb_ref[...], preferred_element_type=F32)
    o_ref[:, pl.ds(pl.multiple_of(j * tn, tn), tn)] = y

    @pl.when(j == pl.num_programs(1) - 1)
    def _():
        o_ref[...] = x_ref[...] + _rms(o_ref[...], nw_ref[...])


def outproj(oa, ob, w, x, nw, tm=512, tn=512):
    m, ka = oa.shape
    n = w.shape[1]
    return pl.pallas_call(
        functools.partial(_outproj_body, tn=tn),
        grid=(m // tm, n // tn),
        in_specs=[pl.BlockSpec((tm, ka), lambda i, j: (i, 0)),
                  pl.BlockSpec((tm, ka), lambda i, j: (i, 0)),
                  pl.BlockSpec((ka, tn), lambda i, j: (0, j)),
                  pl.BlockSpec((ka, tn), lambda i, j: (1, j)),
                  pl.BlockSpec((tm, n), lambda i, j: (i, 0), pipeline_mode=pl.Buffered(1)),
                  pl.BlockSpec((1, n), lambda i, j: (0, 0))],
        out_specs=pl.BlockSpec((tm, n), lambda i, j: (i, 0)),
        out_shape=jax.ShapeDtypeStruct((m, n), F32),
        compiler_params=_cparams(("parallel", "arbitrary")),
        name="outproj",
    )(oa, ob, w, w, x, nw.reshape(1, n))


FFN_HALO = 16


def _ffn_body(x_ref, xh_ref, nw_ref, wg_ref, wu_ref, cwg_ref, cwu_ref, cbg_ref, cbu_ref,
              wd_ref, pw_ref, o_ref, h_s, *, tm, seq):
    i = pl.program_id(0)
    j = pl.program_id(1)

    @pl.when(j == 0)
    def _():
        keep = jnp.where((i * tm) % seq == 0, 0.0, 1.0)
        h_s[0:FFN_HALO, :] = (_rms(xh_ref[...], nw_ref[...]) * keep).astype(BF16)
        h_s[FFN_HALO:, :] = _rms(x_ref[...], nw_ref[...]).astype(BF16)
        o_ref[...] = jnp.zeros_like(o_ref)

    h = h_s[...]

    def branch(w_ref, cw_ref, cb_ref):
        u = jnp.dot(h, w_ref[...], preferred_element_type=F32)
        cw = cw_ref[...]
        y = u * cw[2:3] + pltpu.roll(u, 1, 0) * cw[1:2] + pltpu.roll(u, 2, 0) * cw[0:1]
        return y[FFN_HALO:] + cb_ref[...]

    g = branch(wg_ref, cwg_ref, cbg_ref)
    u = branch(wu_ref, cwu_ref, cbu_ref)
    act = (_silu(g) * u).astype(BF16)
    o_ref[...] += jnp.dot(act, wd_ref[...], preferred_element_type=F32)

    @pl.when(j == pl.num_programs(1) - 1)
    def _():
        o_ref[...] = x_ref[...] + _rms(o_ref[...], pw_ref[...])


def conv_ffn(x, li, nw, w_up, conv_w, conv_b, w_down, pw, seq, tm=512, tf=256):
    m, k = x.shape
    nl, f, _ = w_down.shape
    nf = f // tf
    hb = tm // FFN_HALO
    cb = conv_b.reshape(nl, 1, 2 * f)
    row = pl.BlockSpec((None, 1, k), lambda i, j: (li, 0, 0))
    return pl.pallas_call(
        functools.partial(_ffn_body, tm=tm, seq=seq),
        grid=(m // tm, nf),
        in_specs=[pl.BlockSpec((tm, k), lambda i, j: (i, 0), pipeline_mode=pl.Buffered(1)),
                  pl.BlockSpec((FFN_HALO, k), lambda i, j: (jnp.maximum(i * hb - 1, 0), 0)),
                  row,
                  pl.BlockSpec((None, k, tf), lambda i, j: (li, 0, j)),
                  pl.BlockSpec((None, k, tf), lambda i, j: (li, 0, nf + j)),
                  pl.BlockSpec((None, 3, tf), lambda i, j: (li, 0, j)),
                  pl.BlockSpec((None, 3, tf), lambda i, j: (li, 0, nf + j)),
                  pl.BlockSpec((None, 1, tf), lambda i, j: (li, 0, j)),
                  pl.BlockSpec((None, 1, tf), lambda i, j: (li, 0, nf + j)),
                  pl.BlockSpec((None, tf, k), lambda i, j: (li, j, 0)),
                  row],
        out_specs=pl.BlockSpec((tm, k), lambda i, j: (i, 0)),
        out_shape=jax.ShapeDtypeStruct((m, k), F32),
        scratch_shapes=[pltpu.VMEM((tm + FFN_HALO, k), BF16)],
        compiler_params=_cparams(("parallel", "arbitrary")),
        name="conv_ffn",
    )(x, x, nw.reshape(nl, 1, k), w_up, w_up, conv_w, conv_w, cb, cb, w_down, pw.reshape(nl, 1, k))


def _ple_body(x_ref, p_ref, gnw_ref, wg_ref, wp_ref, pnw_ref, o_ref, h_s, *, tn):
    j = pl.program_id(1)

    @pl.when(j == 0)
    def _():
        h_s[...] = _rms(x_ref[...], gnw_ref[...]).astype(BF16)

    gate = jax.nn.sigmoid(jnp.dot(h_s[...], wg_ref[...], preferred_element_type=F32))
    proj = jnp.dot(p_ref[...].astype(BF16), wp_ref[...], preferred_element_type=F32)
    o_ref[:, pl.ds(pl.multiple_of(j * tn, tn), tn)] = gate * proj

    @pl.when(j == pl.num_programs(1) - 1)
    def _():
        o_ref[...] = x_ref[...] + _rms(o_ref[...], pnw_ref[...])


def ple(x, li, p, gnw, wg, wp, pnw, tm=512, tn=512):
    m, k = x.shape
    nl, _, dp = p.shape
    row = pl.BlockSpec((None, 1, k), lambda i, j: (li, 0, 0))
    return pl.pallas_call(
        functools.partial(_ple_body, tn=tn),
        grid=(m // tm, k // tn),
        in_specs=[pl.BlockSpec((tm, k), lambda i, j: (i, 0), pipeline_mode=pl.Buffered(1)),
                  pl.BlockSpec((None, tm, dp), lambda i, j: (li, i, 0)),
                  row,
                  pl.BlockSpec((None, k, tn), lambda i, j: (li, 0, j)),
                  pl.BlockSpec((None, dp, tn), lambda i, j: (li, 0, j)),
                  row],
        out_specs=pl.BlockSpec((tm, k), lambda i, j: (i, 0)),
        out_shape=jax.ShapeDtypeStruct((m, k), F32),
        scratch_shapes=[pltpu.VMEM((tm, k), BF16)],
        compiler_params=_cparams(("parallel", "arbitrary")),
        name="ple",
    )(x, p, gnw.reshape(nl, 1, k), wg, wp, pnw.reshape(nl, 1, k))


def _gdn_ltri():
    r = np.arange(GDN_GROUP)[:, None]
    t = np.arange(GDN_GROUP)[None, :]
    return jnp.asarray((((r // GDN_CHUNK) == (t // GDN_CHUNK)) & (r >= t)).astype(np.float32), dtype=BF16)


def _gdn_head_setup(hd, lanes, zq_ref, zk_ref, zv_ref, sm_ref, cwq_ref, cwk_ref, cwv_ref, prm_ref, ltri_ref,
                    q_s, k_s, v_s, g_s, b_s, o0_s, n0_s, gl_s, qe_s, m_s, seq):
    c = GDN_CHUNK
    gs = GDN_GROUP
    row = lax.broadcasted_iota(jnp.int32, (seq, LANES), 0)
    lane = lax.broadcasted_iota(jnp.int32, (seq, LANES), 1)

    def conv_silu(z_ref, w_ref):
        z = z_ref[0, :, lanes]
        w = w_ref[:, lanes]
        y = z * w[3:4]
        for s in (1, 2, 3):
            y = y + jnp.where(row >= s, pltpu.roll(z, s, 0), 0.0) * w[3 - s:4 - s]
        return _silu(y)

    q = conv_silu(zq_ref, cwq_ref)
    k = conv_silu(zk_ref, cwk_ref)
    q_s[...] = q * lax.rsqrt(jnp.sum(q * q, -1, keepdims=True) + EPS) * (HEAD_DIM ** -0.5)
    k_s[...] = k * lax.rsqrt(jnp.sum(k * k, -1, keepdims=True) + EPS)
    v_s[...] = conv_silu(zv_ref, cwv_ref)

    sm = sm_ref[0]
    glog = -jnp.exp(prm_ref[0:1, :]) * jax.nn.softplus(sm + prm_ref[1:2, :])
    g_s[...] = jnp.broadcast_to(_lane_col(glog, lane, hd), (seq, LANES))
    b_s[...] = jnp.broadcast_to(_lane_col(jax.nn.sigmoid(sm), lane, N_HEADS + hd), (seq, LANES))

    nper = gs // c
    rs = lax.broadcasted_iota(jnp.int32, (c, gs), 0)
    cs = lax.broadcasted_iota(jnp.int32, (c, gs), 1)
    jj = cs % c
    cblk = cs // c
    incl = rs >= jj
    strict = rs > jj
    eye_side = (rs == jj).astype(F32)
    r2 = lax.broadcasted_iota(jnp.int32, (gs, gs), 0)
    c2 = lax.broadcasted_iota(jnp.int32, (gs, gs), 1)
    bdm = ((r2 // c) == (c2 // c)).astype(F32)
    bdm_b = bdm.astype(BF16)
    lane_c = lax.broadcasted_iota(jnp.int32, (c, LANES), 1)
    lane_g = lax.broadcasted_iota(jnp.int32, (gs, LANES), 1)
    ones3 = (lane_c < 3).astype(F32).astype(BF16)

    def side_col(xb):
        left = jnp.where(lane_c < c, xb[0:c], xb[c:2 * c])
        right = jnp.where(lane_c < c, xb[2 * c:3 * c], xb[3 * c:4 * c])
        return jnp.concatenate([left, right], axis=1)

    def to_side(full):
        out = full[0:c]
        for ci in range(1, nper):
            out = jnp.where(cblk == ci, full[ci * c:(ci + 1) * c], out)
        return out

    def bd(xs, mask):
        return jnp.concatenate([xs] * nper, axis=0) * mask

    def mm3(ph, pl_, xh, xl):
        bh = bd(xh, bdm_b)
        lhs = jnp.concatenate([ph, ph, pl_], axis=1)
        rhs = jnp.concatenate([bh, bd(xl, bdm_b), bh], axis=0)
        return jnp.dot(lhs, rhs, preferred_element_type=F32)

    def group_pre(gi):
        off = pl.multiple_of(gi * gs, gs)
        qg = q_s[pl.ds(off, gs), :]
        kg = k_s[pl.ds(off, gs), :]
        bg = b_s[pl.ds(off, gs), :]
        gam = _dot_sel(ltri_ref[...], g_s[pl.ds(off, gs), :])
        g_hi = gam.astype(BF16).astype(F32)
        g_mid = (gam - g_hi).astype(BF16).astype(F32)
        g_lo = gam - g_hi - g_mid
        gam3 = jnp.where(lane_g == 0, g_hi, jnp.where(lane_g == 1, g_mid, jnp.where(lane_g == 2, g_lo, 0.0)))
        gam_row = lax.dot_general(ones3, gam3.astype(BF16), NT, preferred_element_type=F32)
        dec = jnp.where(incl, jnp.exp(jnp.where(incl, side_col(gam) - gam_row, 0.0)), 0.0)
        qb = qg.astype(BF16)
        kb = kg.astype(BF16)
        kk = to_side(lax.dot_general(kb, kb, NT, preferred_element_type=F32))
        a = jnp.where(strict, side_col(bg) * dec * kk, 0.0)
        return dict(off=off, qg=qg, kg=kg, bg=bg, gam=gam, dec=dec, qb=qb, kb=kb, a=a)

    def group_post(gi, st, tinv):
        off, qg, kg, bg, gam, dec, qb, kb = (st[n] for n in ("off", "qg", "kg", "bg", "gam", "dec", "qb", "kb"))
        vg = v_s[pl.ds(off, gs), :]
        th, tl = _split2(tinv)
        eg = jnp.exp(gam)
        rh, rl = _split2(jnp.concatenate([vg * bg, kg * (bg * eg)], axis=1))
        bth = bd(th, bdm_b)
        sol = jnp.dot(jnp.concatenate([bth, bth, bd(tl, bdm_b)], axis=1),
                      jnp.concatenate([rh, rl, rh], axis=0), preferred_element_type=F32)
        solb = sol.astype(BF16)
        qk = bd(dec, bdm) * lax.dot_general(qb, kb, NT, preferred_element_type=F32)
        qkuw = jnp.dot(qk.astype(BF16), solb, preferred_element_type=F32)
        o0_s[pl.ds(off, gs), :] = qkuw[:, :HEAD_DIM]
        qe_s[pl.ds(off, gs), :] = (qg * eg - qkuw[:, HEAD_DIM:]).astype(BF16)
        for ci in range(nper):
            lo = ci * c
            gl = gam[lo + c - 1:lo + c, :]
            kd = (kg[lo:lo + c] * jnp.exp(gl - gam[lo:lo + c])).astype(BF16)
            kds = lax.dot_general(kd, solb[lo:lo + c], TN, preferred_element_type=F32)
            so = pl.multiple_of((gi * nper + ci) * HEAD_DIM, HEAD_DIM)
            n0_s[pl.ds(so, HEAD_DIM), :] = kds[:, :HEAD_DIM]
            m_s[pl.ds(so, HEAD_DIM), :] = (-kds[:, HEAD_DIM:]).astype(BF16)
            gl_s[pl.ds(pl.multiple_of((gi * nper + ci) * 8, 8), 8), :] = jnp.broadcast_to(jnp.exp(gl), (8, LANES))

    ngroups = seq // gs
    nlock = GDN_LOCKSTEP if ngroups % GDN_LOCKSTEP == 0 else 1

    def groups(it, carry):
        gis = [it + u * (ngroups // nlock) for u in range(nlock)]
        sts = [group_pre(gi) for gi in gis]
        tinvs = [eye_side - st["a"] for st in sts]
        xs = [_split2(st["a"]) for st in sts]
        for _ in range(5):
            xs = [_split2(mm3(xh, xl, xh, xl)) for xh, xl in xs]
            ts = [_split2(t) for t in tinvs]
            tinvs = [t + mm3(th, tl, xh, xl) for t, (th, tl), (xh, xl) in zip(tinvs, ts, xs)]
        for gi, st, t in zip(gis, sts, tinvs):
            group_post(gi, st, t)
        return carry

    lax.fori_loop(0, ngroups // nlock, groups, 0)


def _gdn_body(zq_ref, zk_ref, zv_ref, zg_ref, sm_ref, cwq_ref, cwk_ref, cwv_ref, prm_ref, nw_ref, ltri_ref,
              o_ref, *scratch, seq):
    c = GDN_CHUNK
    nh = GDN_HEADS
    o0_s, n0_s, gl_s, qe_s, m_s = scratch[5:]
    for hh in range(nh):
        _gdn_head_setup(pl.program_id(1) * nh + hh, slice(hh * LANES, (hh + 1) * LANES), zq_ref, zk_ref, zv_ref,
                        sm_ref, cwq_ref, cwk_ref, cwv_ref, prm_ref, ltri_ref, *(r.at[hh] for r in scratch), seq)

    def step(n, states):
        off = pl.multiple_of(n * c, c)
        so = pl.multiple_of(n * HEAD_DIM, HEAD_DIM)
        sbs = [s.astype(BF16) for s in states]
        for hh in range(nh):
            o0_s[hh, pl.ds(off, c), :] = o0_s[hh, pl.ds(off, c), :] + jnp.dot(
                qe_s[hh, pl.ds(off, c), :], sbs[hh], preferred_element_type=F32)
        ms = [jnp.dot(m_s[hh, pl.ds(so, HEAD_DIM), :], sbs[hh], preferred_element_type=F32) for hh in range(nh)]
        return tuple(gl_s[hh, pl.ds(pl.multiple_of(n * 8, 8), 1), :] * states[hh] + n0_s[hh, pl.ds(so, HEAD_DIM), :]
                     + ms[hh] for hh in range(nh))

    lax.fori_loop(0, seq // c, step, tuple(jnp.zeros((HEAD_DIM, HEAD_DIM), F32) for _ in range(nh)))
    for hh in range(nh):
        lanes = slice(hh * LANES, (hh + 1) * LANES)
        o_ref[0, :, lanes] = (_rms(o0_s[hh], nw_ref[...]) * _silu(zg_ref[0, :, lanes])).astype(BF16)


def gated_deltanet(z3, conv_w, prm, norm_w):
    bsz, seq, _ = z3.shape
    nh = GDN_HEADS
    zspec = lambda cb: pl.BlockSpec((1, seq, nh * LANES), lambda b, h: (b, 0, cb // nh + h))
    wspec = lambda cb: pl.BlockSpec((4, nh * LANES), lambda b, h: (0, cb // nh + h))
    nchunk = seq // GDN_CHUNK
    return pl.pallas_call(
        functools.partial(_gdn_body, seq=seq),
        grid=(bsz, N_HEADS // nh),
        in_specs=[zspec(AB_QKV), zspec(AB_QKV + 16), zspec(AB_QKV + 32), zspec(AB_GATE),
                  pl.BlockSpec((1, seq, LANES), lambda b, h: (b, 0, AB_SMALL)),
                  wspec(0), wspec(16), wspec(32),
                  pl.BlockSpec((8, LANES), lambda b, h: (0, 0)),
                  pl.BlockSpec((1, LANES), lambda b, h: (0, 0)),
                  pl.BlockSpec((GDN_GROUP, GDN_GROUP), lambda b, h: (0, 0))],
        out_specs=pl.BlockSpec((1, seq, nh * LANES), lambda b, h: (b, 0, h)),
        out_shape=jax.ShapeDtypeStruct((bsz, seq, N_HEADS * HEAD_DIM), BF16),
        scratch_shapes=[pltpu.VMEM((nh, seq, LANES), F32) for _ in range(6)]
        + [pltpu.VMEM((nh, nchunk * HEAD_DIM, HEAD_DIM), F32),
           pltpu.VMEM((nh, nchunk * 8, LANES), F32),
           pltpu.VMEM((nh, seq, LANES), BF16),
           pltpu.VMEM((nh, nchunk * HEAD_DIM, HEAD_DIM), BF16)],
        compiler_params=_cparams(("parallel", "parallel")),
        name="gated_deltanet",
    )(z3, z3, z3, z3, z3, conv_w, conv_w, conv_w, prm, norm_w.reshape(1, HEAD_DIM), _gdn_ltri())


HGRN_LEVELS = tuple(HGRN_CHUNK >> (i + 1) for i in range(8))


def _hgrn_selectors():
    c = HGRN_CHUNK
    r = np.arange(c)[:, None]
    t = np.arange(c)[None, :]
    mats = [r >= t]
    for s in HGRN_LEVELS:
        isq = ((r // s) % 2) == 1
        mats.append(((r // s) == (t // s)) & ((isq & (t <= r)) | (~isq & (t > r))))
    return jnp.asarray(np.concatenate(mats, axis=0).astype(np.float32), dtype=BF16)


def _hgrn_body(zq_ref, zf_ref, zi_ref, zg_ref, lb_ref, nw_ref, sel_ref, o_ref, oi_s, ebl_s, qd_s, kd_s, *, seq):
    c = HGRN_CHUNK
    r = lax.broadcasted_iota(jnp.int32, (c, c), 0)
    cc = lax.broadcasted_iota(jnp.int32, (c, c), 1)
    diag = r == cc
    x = r ^ cc
    lev = jnp.zeros((c, c), jnp.int32)
    for bit in range(1, 8):
        lev = lev + (x >= (1 << bit)).astype(jnp.int32)
    lev = jnp.where(r > cc, lev, -1)
    rr = lax.broadcasted_iota(jnp.int32, (c, LANES), 0)
    lb = lb_ref[...]

    nchunk = seq // c
    nlock = HGRN_LOCKSTEP if nchunk % HGRN_LOCKSTEP == 0 else 1

    def intra(it, carry):
        ns = [it + u * (nchunk // nlock) for u in range(nlock)]
        offs = [pl.multiple_of(n * c, c) for n in ns]
        fxs = [zf_ref[0, pl.ds(off, c), :] for off in offs]
        qs_ = [_silu(zq_ref[0, pl.ds(off, c), :]) for off in offs]
        lfs = [jnp.log(lb + (1.0 - lb) * jax.nn.sigmoid(fx)) for fx in fxs]
        ks_ = [(1.0 - lb) * jax.nn.sigmoid(-fx) for fx in fxs]
        sums = [_dot_sel(sel_ref[...], lf) for lf in lfs]
        atts = [jnp.where(diag, jnp.sum(q * k, -1, keepdims=True), 0.0) for q, k in zip(qs_, ks_)]
        for i, s in enumerate(HGRN_LEVELS):
            isq = ((rr // s) % 2) == 1
            parts = []
            for q, k, sm in zip(qs_, ks_, sums):
                e = jnp.exp(sm[(i + 1) * c:(i + 2) * c])
                parts.append(lax.dot_general(jnp.where(isq, q * e, 0.0).astype(BF16),
                                             jnp.where(isq, 0.0, k * e).astype(BF16), NT,
                                             preferred_element_type=F32))
            atts = [jnp.where(lev == (7 - i), part, att) for part, att in zip(parts, atts)]
        for n, off, q, k, sm, att in zip(ns, offs, qs_, ks_, sums, atts):
            b = sm[0:c]
            bl = b[c - 1:c, :]
            vb = zi_ref[0, pl.ds(off, c), :].astype(BF16)
            oi_s[pl.ds(off, c), :] = jnp.dot(att.astype(BF16), vb, preferred_element_type=F32)
            qd_s[pl.ds(off, c), :] = (q * jnp.exp(b)).astype(BF16)
            kd_s[pl.ds(off, c), :] = (k * jnp.exp(bl - b)).astype(BF16)
            ebl_s[pl.ds(pl.multiple_of(n * 8, 8), 8), :] = jnp.broadcast_to(jnp.exp(bl), (8, LANES))
        return carry

    lax.fori_loop(0, nchunk // nlock, intra, 0)

    def scan(n, st):
        off = pl.multiple_of(n * c, c)
        oi_s[pl.ds(off, c), :] = oi_s[pl.ds(off, c), :] + lax.dot_general(
            qd_s[pl.ds(off, c), :], st.astype(BF16), NT, preferred_element_type=F32)
        vb = zi_ref[0, pl.ds(off, c), :].astype(BF16)
        return (st * ebl_s[pl.ds(pl.multiple_of(n * 8, 8), 1), :]
                + lax.dot_general(vb, kd_s[pl.ds(off, c), :], TN, preferred_element_type=F32))

    lax.fori_loop(0, nchunk, scan, jnp.zeros((HEAD_DIM, HEAD_DIM), F32))
    o_ref[0] = (_rms(oi_s[...], nw_ref[...]) * _silu(zg_ref[0])).astype(BF16)


def hgrn2(z3, lb, norm_w):
    bsz, seq, _ = z3.shape
    zspec = lambda cb: pl.BlockSpec((1, seq, LANES), lambda b, h: (b, 0, cb + h))
    sel = _hgrn_selectors()
    return pl.pallas_call(
        functools.partial(_hgrn_body, seq=seq),
        grid=(bsz, N_HEADS),
        in_specs=[zspec(CD_Q), zspec(CD_F), zspec(CD_I), zspec(CD_G),
                  pl.BlockSpec((1, LANES), lambda b, h: (0, h)),
                  pl.BlockSpec((1, LANES), lambda b, h: (0, 0)),
                  pl.BlockSpec(sel.shape, lambda b, h: (0, 0))],
        out_specs=pl.BlockSpec((1, seq, LANES), lambda b, h: (b, 0, h)),
        out_shape=jax.ShapeDtypeStruct((bsz, seq, N_HEADS * HEAD_DIM), BF16),
        scratch_shapes=[pltpu.VMEM((seq, LANES), F32), pltpu.VMEM((seq // HGRN_CHUNK * 8, LANES), F32),
                        pltpu.VMEM((seq, LANES), BF16), pltpu.VMEM((seq, LANES), BF16)],
        compiler_params=_cparams(("parallel", "parallel")),
        name="hgrn2",
    )(z3, z3, z3, z3, lb.reshape(1, N_HEADS * HEAD_DIM), norm_w.reshape(1, HEAD_DIM), sel)


FOX_BLK = 256


def _fox_gate_body(sm_ref, bias_ref, cumt_ref, *, seq):
    c = FOX_BLK
    r = lax.broadcasted_iota(jnp.int32, (c, c), 0)
    cc = lax.broadcasted_iota(jnp.int32, (c, c), 1)
    ltri = (r >= cc).astype(F32)
    r1 = lax.broadcasted_iota(jnp.int32, (LANES, LANES), 0)
    c1 = lax.broadcasted_iota(jnp.int32, (LANES, LANES), 1)
    eye = (r1 == c1).astype(F32)

    def blk(n, carry):
        off = pl.multiple_of(n * c, c)
        ls = jax.nn.log_sigmoid(sm_ref[0, pl.ds(off, c), :] + bias_ref[...])
        cum = carry + jnp.dot(ltri, ls, precision=HI, preferred_element_type=F32)
        cumt_ref[0, :, pl.ds(off, c)] = lax.dot_general(eye, cum, NT, precision=HI, preferred_element_type=F32)
        return cum[c - 1:c, :]

    lax.fori_loop(0, seq // c, blk, jnp.zeros((1, LANES), F32))


def fox_gates(z3, bias_row):
    bsz, seq, _ = z3.shape
    return pl.pallas_call(
        functools.partial(_fox_gate_body, seq=seq),
        grid=(bsz,),
        in_specs=[pl.BlockSpec((1, seq, LANES), lambda b: (b, 0, CD_SMALL)),
                  pl.BlockSpec((1, LANES), lambda b: (0, 0))],
        out_specs=pl.BlockSpec((1, LANES, seq), lambda b: (b, 0, 0)),
        out_shape=jax.ShapeDtypeStruct((bsz, LANES, seq), F32),
        compiler_params=_cparams(("parallel",)),
        name="fox_gates",
    )(z3, bias_row)


ATT_ROWS = 128
ATT_TK = 256


def _for_blocks(n, unroll, body, carry=None):
    if isinstance(n, int):
        for j in range(n):
            carry = body(j, carry)
        return carry

    def it(i, c):
        for u in range(unroll):
            c = body(i * unroll + u, c)
        return c

    return lax.fori_loop(0, n // unroll, it, carry)


def _softmax_pv(s_scr, nblk, unroll, v_fn, m_scr, l_scr, acc_scr):
    _, rows, bw = s_scr.shape
    nlb = bw // LANES
    for ch in range(rows // ATT_ROWS):
        rs = slice(ch * ATT_ROWS, (ch + 1) * ATT_ROWS)

        def pmax(j, mel):
            for b in range(nlb):
                mel = jnp.maximum(mel, s_scr[j, rs, b * LANES:(b + 1) * LANES])
            return mel

        mel = _for_blocks(nblk, unroll, pmax, jnp.full((ATT_ROWS, LANES), NEG, F32))
        m_scr[rs, :] = jnp.broadcast_to(jnp.max(mel, -1, keepdims=True), (ATT_ROWS, LANES))
    l_scr[...] = jnp.zeros_like(l_scr)
    acc_scr[...] = jnp.zeros_like(acc_scr)

    def ppv(j, c):
        pcs = []
        for ch in range(rows // ATT_ROWS):
            rs = slice(ch * ATT_ROWS, (ch + 1) * ATT_ROWS)
            mb = m_scr[rs, :]
            lacc = l_scr[rs, :]
            ps = []
            for b in range(nlb):
                pb = jnp.exp(s_scr[j, rs, b * LANES:(b + 1) * LANES] - mb)
                lacc = lacc + pb
                ps.append(pb.astype(BF16))
            l_scr[rs, :] = lacc
            pcs.append(jnp.concatenate(ps, axis=1))
        acc_scr[...] += jnp.dot(jnp.concatenate(pcs, axis=0), v_fn(j), preferred_element_type=F32)
        return c

    _for_blocks(nblk, unroll, ppv)
    return acc_scr[...] / jnp.sum(l_scr[...], -1, keepdims=True)


def _softmax_scratch(rows, nblk, bw):
    return [pltpu.VMEM((nblk, rows, bw), F32), pltpu.VMEM((rows, LANES), F32),
            pltpu.VMEM((rows, LANES), F32), pltpu.VMEM((rows, HEAD_DIM), F32)]


def _fox_attn_body(q_ref, k_ref, v_ref, cumt_ref, o_ref, s_scr, m_scr, l_scr, acc_scr, *, tq):
    hd = pl.program_id(1)
    qi = pl.program_id(2)
    qall = (q_ref[0] * (HEAD_DIM ** -0.5)).astype(BF16)

    def scores(j):
        off = pl.multiple_of(j * tq, tq)
        kj = k_ref[0, pl.ds(off, tq), :].astype(BF16)
        ck = cumt_ref[0, pl.ds(hd % 8, 1), pl.ds(off, tq)]
        return lax.dot_general(qall, kj, NT, preferred_element_type=F32) - ck

    row = lax.broadcasted_iota(jnp.int32, (tq, tq), 0)
    col = lax.broadcasted_iota(jnp.int32, (tq, tq), 1)

    def v_blk(j):
        return v_ref[0, pl.ds(pl.multiple_of(j * tq, tq), tq), :].astype(BF16)

    def tile(nfull):
        for j in range(nfull):
            s_scr[j] = scores(j)
        s_scr[nfull] = jnp.where(col <= row, scores(nfull), NEG)
        o_ref[0] = _softmax_pv(s_scr, nfull + 1, 1, v_blk, m_scr, l_scr, acc_scr).astype(BF16)

    lax.switch(qi, [functools.partial(tile, n) for n in range(s_scr.shape[0])])


def fox_attention(z3, cumt):
    bsz, seq, _ = z3.shape
    tq = min(512, seq)
    full = lambda cb: pl.BlockSpec((1, seq, LANES), lambda b, h, i: (b, 0, cb + h))
    return pl.pallas_call(
        functools.partial(_fox_attn_body, tq=tq),
        grid=(bsz, N_HEADS, seq // tq),
        in_specs=[pl.BlockSpec((1, tq, LANES), lambda b, h, i: (b, i, CD_FQKV + h)),
                  full(CD_FQKV + 16), full(CD_FQKV + 32),
                  pl.BlockSpec((1, 8, seq), lambda b, h, i: (b, h // 8, 0))],
        out_specs=pl.BlockSpec((1, tq, LANES), lambda b, h, i: (b, i, h)),
        out_shape=jax.ShapeDtypeStruct((bsz, seq, N_HEADS * HEAD_DIM), BF16),
        scratch_shapes=_softmax_scratch(tq, seq // tq, tq),
        compiler_params=_cparams(("parallel", "parallel", "arbitrary")),
        name="fox_attention",
    )(z3, z3, z3, cumt)


def _nsa_cmp_body(zk_ref, zv_ref, pek_ref, pev_ref, wk1_ref, wk2_ref, wv1_ref, wv2_ref, kc_ref, vc_ref, *, nblk):
    def compress(z_ref, pe_ref, w1_ref, w2_ref):
        u1 = jnp.zeros((nblk, HEAD_DIM), F32)
        u2 = jnp.zeros((nblk, HEAD_DIM), F32)
        for l in range(S_CMP):
            zl = z_ref[0, :, l, :]
            u1 = u1 + jnp.dot(zl, w1_ref[l * HEAD_DIM:(l + 1) * HEAD_DIM, :], precision=HI,
                              preferred_element_type=F32)
            u2 = u2 + jnp.dot(zl, w1_ref[(S_CMP + l) * HEAD_DIM:(S_CMP + l + 1) * HEAD_DIM, :], precision=HI,
                              preferred_element_type=F32)
        pe = jnp.dot(jnp.broadcast_to(pe_ref[...], (8, L_CMP * HEAD_DIM)), w1_ref[...], precision=HI,
                     preferred_element_type=F32)[0:1]
        hmid = u1 + pltpu.roll(u2, nblk - 1, 0) + pe
        return jnp.dot(_silu(hmid), w2_ref[...], precision=HI, preferred_element_type=F32)

    kc_ref[0, 0] = compress(zk_ref, pek_ref, wk1_ref, wk2_ref)
    vc_ref[0, 0] = compress(zv_ref, pev_ref, wv1_ref, wv2_ref)


def nsa_compress(z3, pe_k, pe_v, wk1, wk2, wv1, wv2):
    bsz, seq, n = z3.shape
    nblk = seq // S_CMP
    z4 = z3.reshape(bsz, nblk, S_CMP, n)
    zspec = lambda cb: pl.BlockSpec((1, nblk, S_CMP, LANES), lambda b, g: (b, 0, 0, cb + g))
    wfull = lambda a: pl.BlockSpec(a.shape, lambda b, g: (0, 0))
    pek = pe_k.reshape(1, L_CMP * HEAD_DIM)
    pev = pe_v.reshape(1, L_CMP * HEAD_DIM)
    out = jax.ShapeDtypeStruct((bsz, G_NSA, nblk, HEAD_DIM), F32)
    ospec = pl.BlockSpec((1, 1, nblk, HEAD_DIM), lambda b, g: (b, g, 0, 0))
    return pl.pallas_call(
        functools.partial(_nsa_cmp_body, nblk=nblk),
        grid=(bsz, G_NSA),
        in_specs=[zspec(AB_NKV), zspec(AB_NKV + G_NSA), wfull(pek), wfull(pev),
                  wfull(wk1), wfull(wk2), wfull(wv1), wfull(wv2)],
        out_specs=[ospec, ospec],
        out_shape=[out, out],
        compiler_params=_cparams(("parallel", "parallel")),
        name="nsa_compress",
    )(z4, z4, pek, pev, wk1, wk2, wv1, wv2)


NSA_TQ = 128


def _nsa_expand(seq):
    m = np.arange(LANES)[:, None]
    t = np.arange(seq)[None, :]
    return jnp.asarray(((t // L_SLC) == m).astype(np.float32), dtype=BF16)


def _nsa_overlap(seq):
    cs = np.arange(seq // S_CMP)[:, None] * S_CMP
    ss = np.arange(LANES)[None, :] * L_SLC
    ov = (cs < ss + L_SLC) & (cs + L_CMP > ss) & (cs <= seq - L_CMP) & (ss < seq)
    return jnp.asarray(ov.astype(np.float32), dtype=BF16)


NSA_UNROLL = 2
NSA_WIN_BLOCKS = max(e // ATT_TK - max(e - (WINDOW + NSA_TQ - 2), 0) // ATT_TK + 1
                     for e in range(NSA_TQ - 1, 8 * WINDOW, NSA_TQ))


def _nsa_attn_body(q_ref, kc_ref, vc_ref, ks_ref, vs_ref, kw_ref, vw_ref, sm_ref, exp_ref, ovl_ref, o_ref,
                   s_scr, m_scr, l_scr, acc_scr, *, seq):
    g = pl.program_id(1)
    qi = pl.program_id(2)
    tq = NSA_TQ
    rows = HPG_NSA * tq
    nblk = seq // S_CMP
    n_slc = seq // L_SLC
    n_top = min(N_SEL, n_slc)
    q0 = qi * tq

    qall = q_ref[0]
    qs = jnp.concatenate([qall[:, p * HEAD_DIM:(p + 1) * HEAD_DIM] for p in range(HPG_NSA)], axis=0)
    qs = qs * (HEAD_DIM ** -0.5)
    qb = qs.astype(BF16)
    tpos = q0 + lax.broadcasted_iota(jnp.int32, (rows, 1), 0) % tq

    kc = kc_ref[0, 0]
    vc = vc_ref[0, 0]
    qh, ql = _split2(qs)
    kh, kl = _split2(kc)
    sc = lax.dot_general(jnp.concatenate([qh, qh, ql], axis=1), jnp.concatenate([kh, kl, kh], axis=1), NT,
                         preferred_element_type=F32)
    nidx = lax.broadcasted_iota(jnp.int32, (rows, nblk), 1)
    cmask = (nidx * S_CMP + (L_CMP - 1) <= tpos) & (nidx <= nblk - 2)
    scm = jnp.where(cmask, sc, NEG)
    mc = jnp.max(scm, -1, keepdims=True)
    ec = jnp.where(cmask, jnp.exp(scm - mc), 0.0)
    dc = jnp.sum(ec, -1, keepdims=True)
    p_cmp = ec / jnp.where(dc > 0, dc, 1.0)
    o_cmp = jnp.dot(p_cmp.astype(BF16), vc.astype(BF16), preferred_element_type=F32)

    psum = p_cmp[0:tq]
    for p in range(1, HPG_NSA):
        psum = psum + p_cmp[p * tq:(p + 1) * tq]
    ph, pl_ = _split2(psum)
    imp = jnp.dot(jnp.concatenate([ph, pl_], axis=1), jnp.concatenate([ovl_ref[...], ovl_ref[...]], axis=0),
                  preferred_element_type=F32)
    blk = lax.broadcasted_iota(jnp.int32, (tq, LANES), 1)
    cur = (q0 + lax.broadcasted_iota(jnp.int32, (tq, LANES), 0)) // L_SLC
    valid = blk <= cur
    forced = (blk == 0) | (blk == cur) | (blk == cur - 1)
    score = jnp.where(valid, jnp.where(forced, FORCE_SCORE, imp), NEG)
    rank = jnp.zeros((tq, LANES), jnp.int32)
    for mp in range(n_slc):
        colv = score[:, mp:mp + 1]
        rank = rank + ((colv > score) | ((colv == score) & (mp < blk))).astype(jnp.int32)
    selb = (valid & (rank < n_top)).astype(F32).astype(BF16)

    tk = ATT_TK
    qrow = q0 + lax.broadcasted_iota(jnp.int32, (tq, tk), 0)
    col = lax.broadcasted_iota(jnp.int32, (tq, tk), 1)

    def branch(k_ref, v_ref, first, nb, bias_fn):
        def scores(j, c):
            off = pl.multiple_of((first + j) * tk, tk)
            s = lax.dot_general(qb, k_ref[0, pl.ds(off, tk), :].astype(BF16), NT,
                                preferred_element_type=F32)
            bias = bias_fn(off)
            for p in range(HPG_NSA):
                s_scr[j, p * tq:(p + 1) * tq, :] = s[p * tq:(p + 1) * tq] + bias
            return c

        _for_blocks(nb, NSA_UNROLL, scores)

        def v_blk(j):
            return v_ref[0, pl.ds(pl.multiple_of((first + j) * tk, tk), tk), :].astype(BF16)

        return _softmax_pv(s_scr, nb, NSA_UNROLL, v_blk, m_scr, l_scr, acc_scr)

    def slc_bias(off):
        sel = jnp.dot(selb, exp_ref[:, pl.ds(off, tk)], preferred_element_type=F32)
        return jnp.where((sel > 0.5) & (off + col <= qrow), 0.0, NEG)

    last = (q0 + tq - 1) // tk
    o_slc = branch(ks_ref, vs_ref, 0, (last // NSA_UNROLL + 1) * NSA_UNROLL, slc_bias)

    def win_bias(off):
        d = qrow - (off + col)
        return jnp.where((d >= 0) & (d < WINDOW), 0.0, NEG)

    nwin = min(NSA_WIN_BLOCKS, seq // tk)
    o_win = branch(kw_ref, vw_ref, jnp.maximum(last - (nwin - 1), 0), nwin, win_bias)

    sg = jax.nn.sigmoid(sm_ref[0])
    for p in range(HPG_NSA):
        sl = slice(p * tq, (p + 1) * tq)
        base = 2 * N_HEADS + g * HPG_NSA + p
        o = (_lane_col(sg, blk, base) * o_cmp[sl] + _lane_col(sg, blk, base + N_HEADS) * o_slc[sl]
             + _lane_col(sg, blk, base + 2 * N_HEADS) * o_win[sl])
        o_ref[0, :, p * HEAD_DIM:(p + 1) * HEAD_DIM] = o.astype(BF16)


def nsa_attention(z3, kc, vc):
    bsz, seq, _ = z3.shape
    tq = NSA_TQ
    nblk = seq // S_CMP
    full = lambda cb: pl.BlockSpec((1, seq, LANES), lambda b, g, i: (b, 0, cb + g))
    cspec = pl.BlockSpec((1, 1, nblk, HEAD_DIM), lambda b, g, i: (b, g, 0, 0))
    qw = HPG_NSA * HEAD_DIM
    return pl.pallas_call(
        functools.partial(_nsa_attn_body, seq=seq),
        grid=(bsz, G_NSA, seq // tq),
        in_specs=[pl.BlockSpec((1, tq, qw), lambda b, g, i: (b, i, AB_NQ * LANES // qw + g)),
                  cspec, cspec,
                  full(AB_NKV + 2 * G_NSA), full(AB_NKV + 3 * G_NSA),
                  full(AB_NKV + 4 * G_NSA), full(AB_NKV + 5 * G_NSA),
                  pl.BlockSpec((1, tq, LANES), lambda b, g, i: (b, i, AB_SMALL)),
                  pl.BlockSpec((LANES, seq), lambda b, g, i: (0, 0)),
                  pl.BlockSpec((nblk, LANES), lambda b, g, i: (0, 0))],
        out_specs=pl.BlockSpec((1, tq, qw), lambda b, g, i: (b, i, g)),
        out_shape=jax.ShapeDtypeStruct((bsz, seq, N_HEADS * HEAD_DIM), BF16),
        scratch_shapes=_softmax_scratch(HPG_NSA * tq, seq // ATT_TK, ATT_TK),
        compiler_params=_cparams(("parallel", "parallel", "arbitrary")),
        name="nsa_attention",
    )(z3, kc, vc, z3, z3, z3, z3, z3, _nsa_expand(seq), _nsa_overlap(seq))


def _repack_body(w_ref, o_ref, *, sections, pad_from):
    o_ref[pad_from:, :] = jnp.zeros((o_ref.shape[0] - pad_from, o_ref.shape[1]), BF16)
    for dst, src, width in sections:
        o_ref[dst:dst + width, :] = w_ref[src:src + width, :].astype(BF16)


def _repack_weight(w, sections, n_out, pad_from, tc=256):
    wt = jnp.swapaxes(w, 0, 1)
    n_in, k = wt.shape
    return pl.pallas_call(
        functools.partial(_repack_body, sections=sections, pad_from=pad_from),
        grid=(k // tc,),
        in_specs=[pl.BlockSpec((n_in, tc), lambda i: (0, i))],
        out_specs=pl.BlockSpec((n_out, tc), lambda i: (0, i)),
        out_shape=jax.ShapeDtypeStruct((n_out, k), BF16),
        compiler_params=_cparams(("parallel",)),
        name="repack_weight",
    )(wt)


def _ab_weight(w):
    small = AB_SMALL * LANES
    sections = ((0, 0, 6144), (6144, 6176, 2048), (8192, 8224, 2048), (10240, 10272, 3072),
                (small, 6144, 32), (small + 32, 13344, 48))
    return _repack_weight(w, sections, AB_N, small)


def _cd_weight(w):
    small = CD_SMALL * LANES
    return _repack_weight(w, ((0, 0, small), (small, small, 16)), CD_N, small)


def _row128(v):
    return jnp.pad(v.astype(F32), (0, LANES - v.shape[0])).reshape(1, LANES)


def kernel(x, p, ab_norm_pre, ab_norm_post, ab_w_in, gdn_conv_w, gdn_a_log, gdn_dt_bias, gdn_norm, nsa_pe_k, nsa_pe_v, nsa_cmp_k1, nsa_cmp_k2, nsa_cmp_v1, nsa_cmp_v2, ab_w_out, cd_norm_pre, cd_norm_post, cd_w_in, hgrn_lb_logits, hgrn_norm, fox_f_bias, cd_w_out, ffn_norm_pre, ffn_norm_post, ffn_w_up, ffn_conv_w, ffn_conv_b, ffn_w_down, ple_w_proj, ple_gate_norm, ple_w_gate, ple_norm_post):
    bsz, seq, dm = x.shape
    depth = p.shape[0]
    m = bsz * seq
    half = N_HEADS * HEAD_DIM
    sm_ = jax.nn.softmax(hgrn_lb_logits.astype(F32), axis=0)
    lb_table = jnp.cumsum(sm_, axis=0) - sm_[0]
    xf = x.reshape(m, dm)
    pf = p.reshape(depth, m, -1)
    w_up, w_down = ffn_w_up.astype(BF16), ffn_w_down.astype(BF16)
    w_gate, w_proj = ple_w_gate.astype(BF16), ple_w_proj.astype(BF16)
    for li in range(depth):
        j = li // 2
        if li % 2 == 0:
            z3 = norm_matmul(xf, ab_norm_pre[j], _ab_weight(ab_w_in[j])).reshape(bsz, seq, AB_N)
            prm = jnp.concatenate([_row128(gdn_a_log[j]), _row128(gdn_dt_bias[j]),
                                   jnp.zeros((6, LANES), F32)], axis=0)
            o_a = gated_deltanet(z3, gdn_conv_w[j], prm, gdn_norm[j])
            kc, vc = nsa_compress(z3, nsa_pe_k[j], nsa_pe_v[j], nsa_cmp_k1[j], nsa_cmp_k2[j],
                                  nsa_cmp_v1[j], nsa_cmp_v2[j])
            o_b = nsa_attention(z3, kc, vc)
            w_out, post = ab_w_out[j], ab_norm_post[j]
        else:
            z3 = norm_matmul(xf, cd_norm_pre[j], _cd_weight(cd_w_in[j])).reshape(bsz, seq, CD_N)
            o_a = hgrn2(z3, lb_table[li], hgrn_norm[j])
            o_b = fox_attention(z3, fox_gates(z3, _row128(fox_f_bias[j])))
            w_out, post = cd_w_out[j], cd_norm_post[j]
        xf = outproj(o_a.reshape(m, half), o_b.reshape(m, half), w_out.astype(BF16), xf, post)
        xf = conv_ffn(xf, li, ffn_norm_pre, w_up, ffn_conv_w, ffn_conv_b, w_down, ffn_norm_post, seq)
        xf = ple(xf, li, pf, ple_gate_norm, w_gate, w_proj, ple_norm_post)
    return xf.reshape(bsz, seq, dm)
```

```python
import functools

import numpy as np
import jax
import jax.numpy as jnp
from jax import lax
from jax.experimental import pallas as pl
from jax.experimental.pallas import tpu as pltpu

F32 = jnp.float32
BF16 = jnp.bfloat16
HI = lax.Precision.HIGHEST
NT = (((1,), (1,)), ((), ()))
TN = (((0,), (0,)), ((), ()))

D_MODEL = 4096
HEAD_DIM = 128
N_HEADS = 16
G_NSA = 4
HPG_NSA = 4
L_CMP = 32
S_CMP = 16
L_SLC = 64
N_SEL = 8
WINDOW = 512
FORCE_SCORE = 1e4
GDN_CHUNK = 64
GDN_GROUP = 256
GDN_LOCKSTEP = 8
GDN_HEADS = 2
HGRN_CHUNK = 256
HGRN_LOCKSTEP = 2
D_FF = 11008
EPS = 1e-6
NEG = -1e30
LANES = 128
SMALL_W = 512

AB_QKV, AB_GATE, AB_NQ, AB_NKV, AB_SMALL = 0, 48, 64, 80, 104
AB_N = 108 * LANES
CD_Q, CD_F, CD_I, CD_G, CD_FQKV, CD_SMALL = 0, 16, 32, 48, 64, 112
CD_N = 116 * LANES

VMEM_LIMIT = 56 * 1024 * 1024


def _cparams(sem):
    return pltpu.CompilerParams(dimension_semantics=sem, vmem_limit_bytes=VMEM_LIMIT)


def _rms(x, w):
    return x * lax.rsqrt(jnp.mean(x * x, axis=-1, keepdims=True) + EPS) * w


def _silu(x):
    return x * jax.nn.sigmoid(x)


def _lane_col(x, lane_idx, lane):
    return jnp.sum(jnp.where(lane_idx == lane, x, 0.0), axis=-1, keepdims=True)


def _split2(x):
    hi = x.astype(BF16)
    return hi, (x - hi.astype(F32)).astype(BF16)


def _dot_sel(sel, x):
    n = x.shape[1]
    hi, lo = _split2(x)
    y = jnp.dot(sel, jnp.concatenate([hi, lo], axis=1), preferred_element_type=F32)
    return y[:, :n] + y[:, n:]


def _norm_matmul_body(x_ref, nw_ref, w_ref, o_ref, h_ref):
    @pl.when(pl.program_id(1) == 0)
    def _():
        h_ref[...] = _rms(x_ref[...], nw_ref[...]).astype(BF16)

    o_ref[...] = lax.dot_general(h_ref[...], w_ref[...], NT, preferred_element_type=F32)


def norm_matmul(x, nw, wt, tm=512, tn=512):
    m, k = x.shape
    n = wt.shape[0]
    return pl.pallas_call(
        _norm_matmul_body,
        grid=(m // tm, n // tn),
        in_specs=[pl.BlockSpec((tm, k), lambda i, j: (i, 0)),
                  pl.BlockSpec((1, k), lambda i, j: (0, 0)),
                  pl.BlockSpec((tn, k), lambda i, j: (j, 0))],
        out_specs=pl.BlockSpec((tm, tn), lambda i, j: (i, j)),
        out_shape=jax.ShapeDtypeStruct((m, n), F32),
        scratch_shapes=[pltpu.VMEM((tm, k), BF16)],
        compiler_params=_cparams(("parallel", "arbitrary")),
        name="norm_matmul",
    )(x, nw.reshape(1, k), wt)


def _outproj_body(oa_ref, ob_ref, wa_ref, wb_ref, x_ref, nw_ref, o_ref, *, tn):
    j = pl.program_id(1)
    y = jnp.dot(oa_ref[...], wa_ref[...], preferred_element_type=F32)
    y = y + jnp.dot(ob_ref[...], wb_ref[...], preferred_element_type=F32)
    o_ref[:, pl.ds(pl.multiple_of(j * tn, tn), tn)] = y

    @pl.when(j == pl.num_programs(1) - 1)
    def _():
        o_ref[...] = x_ref[...] + _rms(o_ref[...], nw_ref[...])


def outproj(oa, ob, w, x, nw, tm=512, tn=512):
    m, ka = oa.shape
    n = w.shape[1]
    return pl.pallas_call(
        functools.partial(_outproj_body, tn=tn),
        grid=(m // tm, n // tn),
        in_specs=[pl.BlockSpec((tm, ka), lambda i, j: (i, 0)),
                  pl.BlockSpec((tm, ka), lambda i, j: (i, 0)),
                  pl.BlockSpec((ka, tn), lambda i, j: (0, j)),
                  pl.BlockSpec((ka, tn), lambda i, j: (1, j)),
                  pl.BlockSpec((tm, n), lambda i, j: (i, 0), pipeline_mode=pl.Buffered(1)),
                  pl.BlockSpec((1, n), lambda i, j: (0, 0))],
        out_specs=pl.BlockSpec((tm, n), lambda i, j: (i, 0)),
        out_shape=jax.ShapeDtypeStruct((m, n), F32),
        compiler_params=_cparams(("parallel", "arbitrary")),
        name="outproj",
    )(oa, ob, w, w, x, nw.reshape(1, n))


FFN_HALO = 16


def _ffn_body(x_ref, xh_ref, nw_ref, wg_ref, wu_ref, cwg_ref, cwu_ref, cbg_ref, cbu_ref,
              wd_ref, pw_ref, o_ref, h_s, *, tm, seq):
    i = pl.program_id(0)
    j = pl.program_id(1)

    @pl.when(j == 0)
    def _():
        keep = jnp.where((i * tm) % seq == 0, 0.0, 1.0)
        h_s[0:FFN_HALO, :] = (_rms(xh_ref[...], nw_ref[...]) * keep).astype(BF16)
        h_s[FFN_HALO:, :] = _rms(x_ref[...], nw_ref[...]).astype(BF16)
        o_ref[...] = jnp.zeros_like(o_ref)

    h = h_s[...]

    def branch(w_ref, cw_ref, cb_ref):
        u = jnp.dot(h, w_ref[...], preferred_element_type=F32)
        cw = cw_ref[...]
        y = u * cw[2:3] + pltpu.roll(u, 1, 0) * cw[1:2] + pltpu.roll(u, 2, 0) * cw[0:1]
        return y[FFN_HALO:] + cb_ref[...]

    g = branch(wg_ref, cwg_ref, cbg_ref)
    u = branch(wu_ref, cwu_ref, cbu_ref)
    act = (_silu(g) * u).astype(BF16)
    o_ref[...] += jnp.dot(act, wd_ref[...], preferred_element_type=F32)

    @pl.when(j == pl.num_programs(1) - 1)
    def _():
        o_ref[...] = x_ref[...] + _rms(o_ref[...], pw_ref[...])


def conv_ffn(x, li, nw, w_up, conv_w, conv_b, w_down, pw, seq, tm=512, tf=256):
    m, k = x.shape
    nl, f, _ = w_down.shape
    nf = f // tf
    hb = tm // FFN_HALO
    cb = conv_b.reshape(nl, 1, 2 * f)
    row = pl.BlockSpec((None, 1, k), lambda i, j: (li, 0, 0))
    return pl.pallas_call(
        functools.partial(_ffn_body, tm=tm, seq=seq),
        grid=(m // tm, nf),
        in_specs=[pl.BlockSpec((tm, k), lambda i, j: (i, 0), pipeline_mode=pl.Buffered(1)),
                  pl.BlockSpec((FFN_HALO, k), lambda i, j: (jnp.maximum(i * hb - 1, 0), 0)),
                  row,
                  pl.BlockSpec((None, k, tf), lambda i, j: (li, 0, j)),
                  pl.BlockSpec((None, k, tf), lambda i, j: (li, 0, nf + j)),
                  pl.BlockSpec((None, 3, tf), lambda i, j: (li, 0, j)),
                  pl.BlockSpec((None, 3, tf), lambda i, j: (li, 0, nf + j)),
                  pl.BlockSpec((None, 1, tf), lambda i, j: (li, 0, j)),
                  pl.BlockSpec((None, 1, tf), lambda i, j: (li, 0, nf + j)),
                  pl.BlockSpec((None, tf, k), lambda i, j: (li, j, 0)),
                  row],
        out_specs=pl.BlockSpec((tm, k), lambda i, j: (i, 0)),
        out_shape=jax.ShapeDtypeStruct((m, k), F32),
        scratch_shapes=[pltpu.VMEM((tm + FFN_HALO, k), BF16)],
        compiler_params=_cparams(("parallel", "arbitrary")),
        name="conv_ffn",
    )(x, x, nw.reshape(nl, 1, k), w_up, w_up, conv_w, conv_w, cb, cb, w_down, pw.reshape(nl, 1, k))


def _ple_body(x_ref, p_ref, gnw_ref, wg_ref, wp_ref, pnw_ref, o_ref, h_s, *, tn):
    j = pl.program_id(1)

    @pl.when(j == 0)
    def _():
        h_s[...] = _rms(x_ref[...], gnw_ref[...]).astype(BF16)

    gate = jax.nn.sigmoid(jnp.dot(h_s[...], wg_ref[...], preferred_element_type=F32))
    proj = jnp.dot(p_ref[...].astype(BF16), wp_ref[...], preferred_element_type=F32)
    o_ref[:, pl.ds(pl.multiple_of(j * tn, tn), tn)] = gate * proj

    @pl.when(j == pl.num_programs(1) - 1)
    def _():
        o_ref[...] = x_ref[...] + _rms(o_ref[...], pnw_ref[...])


def ple(x, li, p, gnw, wg, wp, pnw, tm=512, tn=512):
    m, k = x.shape
    nl, _, dp = p.shape
    row = pl.BlockSpec((None, 1, k), lambda i, j: (li, 0, 0))
    return pl.pallas_call(
        functools.partial(_ple_body, tn=tn),
        grid=(m // tm, k // tn),
        in_specs=[pl.BlockSpec((tm, k), lambda i, j: (i, 0), pipeline_mode=pl.Buffered(1)),
                  pl.BlockSpec((None, tm, dp), lambda i, j: (li, i, 0)),
                  row,
                  pl.BlockSpec((None, k, tn), lambda i, j: (li, 0, j)),
                  pl.BlockSpec((None, dp, tn), lambda i, j: (li, 0, j)),
                  row],
        out_specs=pl.BlockSpec((tm, k), lambda i, j: (i, 0)),
        out_shape=jax.ShapeDtypeStruct((m, k), F32),
        scratch_shapes=[pltpu.VMEM((tm, k), BF16)],
        compiler_params=_cparams(("parallel", "arbitrary")),
        name="ple",
    )(x, p, gnw.reshape(nl, 1, k), wg, wp, pnw.reshape(nl, 1, k))


def _gdn_ltri():
    r = np.arange(GDN_GROUP)[:, None]
    t = np.arange(GDN_GROUP)[None, :]
    return jnp.asarray((((r // GDN_CHUNK) == (t // GDN_CHUNK)) & (r >= t)).astype(np.float32), dtype=BF16)


def _gdn_head_setup(hd, lanes, zq_ref, zk_ref, zv_ref, sm_ref, cwq_ref, cwk_ref, cwv_ref, prm_ref, ltri_ref,
                    q_s, k_s, v_s, g_s, b_s, o0_s, n0_s, gl_s, qe_s, m_s, seq):
    c = GDN_CHUNK
    gs = GDN_GROUP
    row = lax.broadcasted_iota(jnp.int32, (seq, LANES), 0)
    lane = lax.broadcasted_iota(jnp.int32, (seq, LANES), 1)

    def conv_silu(z_ref, w_ref):
        z = z_ref[0, :, lanes]
        w = w_ref[:, lanes]
        y = z * w[3:4]
        for s in (1, 2, 3):
            y = y + jnp.where(row >= s, pltpu.roll(z, s, 0), 0.0) * w[3 - s:4 - s]
        return _silu(y)

    q = conv_silu(zq_ref, cwq_ref)
    k = conv_silu(zk_ref, cwk_ref)
    q_s[...] = q * lax.rsqrt(jnp.sum(q * q, -1, keepdims=True) + EPS) * (HEAD_DIM ** -0.5)
    k_s[...] = k * lax.rsqrt(jnp.sum(k * k, -1, keepdims=True) + EPS)
    v_s[...] = conv_silu(zv_ref, cwv_ref)

    sm = sm_ref[0]
    glog = -jnp.exp(prm_ref[0:1, :]) * jax.nn.softplus(sm + prm_ref[1:2, :])
    g_s[...] = jnp.broadcast_to(_lane_col(glog, lane, hd), (seq, LANES))
    b_s[...] = jnp.broadcast_to(_lane_col(jax.nn.sigmoid(sm), lane, N_HEADS + hd), (seq, LANES))

    nper = gs // c
    rs = lax.broadcasted_iota(jnp.int32, (c, gs), 0)
    cs = lax.broadcasted_iota(jnp.int32, (c, gs), 1)
    jj = cs % c
    cblk = cs // c
    incl = rs >= jj
    strict = rs > jj
    eye_side = (rs == jj).astype(F32)
    r2 = lax.broadcasted_iota(jnp.int32, (gs, gs), 0)
    c2 = lax.broadcasted_iota(jnp.int32, (gs, gs), 1)
    bdm = ((r2 // c) == (c2 // c)).astype(F32)
    bdm_b = bdm.astype(BF16)
    lane_c = lax.broadcasted_iota(jnp.int32, (c, LANES), 1)
    lane_g = lax.broadcasted_iota(jnp.int32, (gs, LANES), 1)
    ones3 = (lane_c < 3).astype(F32).astype(BF16)

    def side_col(xb):
        left = jnp.where(lane_c < c, xb[0:c], xb[c:2 * c])
        right = jnp.where(lane_c < c, xb[2 * c:3 * c], xb[3 * c:4 * c])
        return jnp.concatenate([left, right], axis=1)

    def to_side(full):
        out = full[0:c]
        for ci in range(1, nper):
            out = jnp.where(cblk == ci, full[ci * c:(ci + 1) * c], out)
        return out

    def bd(xs, mask):
        return jnp.concatenate([xs] * nper, axis=0) * mask

    def mm3(ph, pl_, xh, xl):
        bh = bd(xh, bdm_b)
        lhs = jnp.concatenate([ph, ph, pl_], axis=1)
        rhs = jnp.concatenate([bh, bd(xl, bdm_b), bh], axis=0)
        return jnp.dot(lhs, rhs, preferred_element_type=F32)

    def group_pre(gi):
        off = pl.multiple_of(gi * gs, gs)
        qg = q_s[pl.ds(off, gs), :]
        kg = k_s[pl.ds(off, gs), :]
        bg = b_s[pl.ds(off, gs), :]
        gam = _dot_sel(ltri_ref[...], g_s[pl.ds(off, gs), :])
        g_hi = gam.astype(BF16).astype(F32)
        g_mid = (gam - g_hi).astype(BF16).astype(F32)
        g_lo = gam - g_hi - g_mid
        gam3 = jnp.where(lane_g == 0, g_hi, jnp.where(lane_g == 1, g_mid, jnp.where(lane_g == 2, g_lo, 0.0)))
        gam_row = lax.dot_general(ones3, gam3.astype(BF16), NT, preferred_element_type=F32)
        dec = jnp.where(incl, jnp.exp(jnp.where(incl, side_col(gam) - gam_row, 0.0)), 0.0)
        qb = qg.astype(BF16)
        kb = kg.astype(BF16)
        kk = to_side(lax.dot_general(kb, kb, NT, preferred_element_type=F32))
        a = jnp.where(strict, side_col(bg) * dec * kk, 0.0)
        return dict(off=off, qg=qg, kg=kg, bg=bg, gam=gam, dec=dec, qb=qb, kb=kb, a=a)

    def group_post(gi, st, tinv):
        off, qg, kg, bg, gam, dec, qb, kb = (st[n] for n in ("off", "qg", "kg", "bg", "gam", "dec", "qb", "kb"))
        vg = v_s[pl.ds(off, gs), :]
        th, tl = _split2(tinv)
        eg = jnp.exp(gam)
        rh, rl = _split2(jnp.concatenate([vg * bg, kg * (bg * eg)], axis=1))
        bth = bd(th, bdm_b)
        sol = jnp.dot(jnp.concatenate([bth, bth, bd(tl, bdm_b)], axis=1),
                      jnp.concatenate([rh, rl, rh], axis=0), preferred_element_type=F32)
        solb = sol.astype(BF16)
        qk = bd(dec, bdm) * lax.dot_general(qb, kb, NT, preferred_element_type=F32)
        qkuw = jnp.dot(qk.astype(BF16), solb, preferred_element_type=F32)
        o0_s[pl.ds(off, gs), :] = qkuw[:, :HEAD_DIM]
        qe_s[pl.ds(off, gs), :] = (qg * eg - qkuw[:, HEAD_DIM:]).astype(BF16)
        for ci in range(nper):
            lo = ci * c
            gl = gam[lo + c - 1:lo + c, :]
            kd = (kg[lo:lo + c] * jnp.exp(gl - gam[lo:lo + c])).astype(BF16)
            kds = lax.dot_general(kd, solb[lo:lo + c], TN, preferred_element_type=F32)
            so = pl.multiple_of((gi * nper + ci) * HEAD_DIM, HEAD_DIM)
            n0_s[pl.ds(so, HEAD_DIM), :] = kds[:, :HEAD_DIM]
            m_s[pl.ds(so, HEAD_DIM), :] = (-kds[:, HEAD_DIM:]).astype(BF16)
            gl_s[pl.ds(pl.multiple_of((gi * nper + ci) * 8, 8), 8), :] = jnp.broadcast_to(jnp.exp(gl), (8, LANES))

    ngroups = seq // gs
    nlock = GDN_LOCKSTEP if ngroups % GDN_LOCKSTEP == 0 else 1

    def groups(it, carry):
        gis = [it + u * (ngroups // nlock) for u in range(nlock)]
        sts = [group_pre(gi) for gi in gis]
        tinvs = [eye_side - st["a"] for st in sts]
        xs = [_split2(st["a"]) for st in sts]
        for _ in range(5):
            xs = [_split2(mm3(xh, xl, xh, xl)) for xh, xl in xs]
            ts = [_split2(t) for t in tinvs]
            tinvs = [t + mm3(th, tl, xh, xl) for t, (th, tl), (xh, xl) in zip(tinvs, ts, xs)]
        for gi, st, t in zip(gis, sts, tinvs):
            group_post(gi, st, t)
        return carry

    lax.fori_loop(0, ngroups // nlock, groups, 0)


def _gdn_body(zq_ref, zk_ref, zv_ref, zg_ref, sm_ref, cwq_ref, cwk_ref, cwv_ref, prm_ref, nw_ref, ltri_ref,
              o_ref, *scratch, seq):
    c = GDN_CHUNK
    nh = GDN_HEADS
    o0_s, n0_s, gl_s, qe_s, m_s = scratch[5:]
    for hh in range(nh):
        _gdn_head_setup(pl.program_id(1) * nh + hh, slice(hh * LANES, (hh + 1) * LANES), zq_ref, zk_ref, zv_ref,
                        sm_ref, cwq_ref, cwk_ref, cwv_ref, prm_ref, ltri_ref, *(r.at[hh] for r in scratch), seq)

    def step(n, states):
        off = pl.multiple_of(n * c, c)
        so = pl.multiple_of(n * HEAD_DIM, HEAD_DIM)
        sbs = [s.astype(BF16) for s in states]
        for hh in range(nh):
            o0_s[hh, pl.ds(off, c), :] = o0_s[hh, pl.ds(off, c), :] + jnp.dot(
                qe_s[hh, pl.ds(off, c), :], sbs[hh], preferred_element_type=F32)
        ms = [jnp.dot(m_s[hh, pl.ds(so, HEAD_DIM), :], sbs[hh], preferred_element_type=F32) for hh in range(nh)]
        return tuple(gl_s[hh, pl.ds(pl.multiple_of(n * 8, 8), 1), :] * states[hh] + n0_s[hh, pl.ds(so, HEAD_DIM), :]
                     + ms[hh] for hh in range(nh))

    lax.fori_loop(0, seq // c, step, tuple(jnp.zeros((HEAD_DIM, HEAD_DIM), F32) for _ in range(nh)))
    for hh in range(nh):
        lanes = slice(hh * LANES, (hh + 1) * LANES)
        o_ref[0, :, lanes] = (_rms(o0_s[hh], nw_ref[...]) * _silu(zg_ref[0, :, lanes])).astype(BF16)


def gated_deltanet(z3, conv_w, prm, norm_w):
    bsz, seq, _ = z3.shape
    nh = GDN_HEADS
    zspec = lambda cb: pl.BlockSpec((1, seq, nh * LANES), lambda b, h: (b, 0, cb // nh + h))
    wspec = lambda cb: pl.BlockSpec((4, nh * LANES), lambda b, h: (0, cb // nh + h))
    nchunk = seq // GDN_CHUNK
    return pl.pallas_call(
        functools.partial(_gdn_body, seq=seq),
        grid=(bsz, N_HEADS // nh),
        in_specs=[zspec(AB_QKV), zspec(AB_QKV + 16), zspec(AB_QKV + 32), zspec(AB_GATE),
                  pl.BlockSpec((1, seq, LANES), lambda b, h: (b, 0, AB_SMALL)),
                  wspec(0), wspec(16), wspec(32),
                  pl.BlockSpec((8, LANES), lambda b, h: (0, 0)),
                  pl.BlockSpec((1, LANES), lambda b, h: (0, 0)),
                  pl.BlockSpec((GDN_GROUP, GDN_GROUP), lambda b, h: (0, 0))],
        out_specs=pl.BlockSpec((1, seq, nh * LANES), lambda b, h: (b, 0, h)),
        out_shape=jax.ShapeDtypeStruct((bsz, seq, N_HEADS * HEAD_DIM), BF16),
        scratch_shapes=[pltpu.VMEM((nh, seq, LANES), F32) for _ in range(6)]
        + [pltpu.VMEM((nh, nchunk * HEAD_DIM, HEAD_DIM), F32),
           pltpu.VMEM((nh, nchunk * 8, LANES), F32),
           pltpu.VMEM((nh, seq, LANES), BF16),
           pltpu.VMEM((nh, nchunk * HEAD_DIM, HEAD_DIM), BF16)],
        compiler_params=_cparams(("parallel", "parallel")),
        name="gated_deltanet",
    )(z3, z3, z3, z3, z3, conv_w, conv_w, conv_w, prm, norm_w.reshape(1, HEAD_DIM), _gdn_ltri())


HGRN_LEVELS = tuple(HGRN_CHUNK >> (i + 1) for i in range(8))


def _hgrn_selectors():
    c = HGRN_CHUNK
    r = np.arange(c)[:, None]
    t = np.arange(c)[None, :]
    mats = [r >= t]
    for s in HGRN_LEVELS:
        isq = ((r // s) % 2) == 1
        mats.append(((r // s) == (t // s)) & ((isq & (t <= r)) | (~isq & (t > r))))
    return jnp.asarray(np.concatenate(mats, axis=0).astype(np.float32), dtype=BF16)


def _hgrn_body(zq_ref, zf_ref, zi_ref, zg_ref, lb_ref, nw_ref, sel_ref, o_ref, oi_s, ebl_s, qd_s, kd_s, *, seq):
    c = HGRN_CHUNK
    r = lax.broadcasted_iota(jnp.int32, (c, c), 0)
    cc = lax.broadcasted_iota(jnp.int32, (c, c), 1)
    diag = r == cc
    x = r ^ cc
    lev = jnp.zeros((c, c), jnp.int32)
    for bit in range(1, 8):
        lev = lev + (x >= (1 << bit)).astype(jnp.int32)
    lev = jnp.where(r > cc, lev, -1)
    rr = lax.broadcasted_iota(jnp.int32, (c, LANES), 0)
    lb = lb_ref[...]
    log_lb = jnp.log(lb)
    log_1m_lb = jnp.log1p(-lb)

    nchunk = seq // c
    nlock = HGRN_LOCKSTEP if nchunk % HGRN_LOCKSTEP == 0 else 1

    def intra(it, carry):
        ns = [it + u * (nchunk // nlock) for u in range(nlock)]
        offs = [pl.multiple_of(n * c, c) for n in ns]
        fxs = [zf_ref[0, pl.ds(off, c), :] for off in offs]
        qs_ = [_silu(zq_ref[0, pl.ds(off, c), :]) for off in offs]
        lfs = [jnp.logaddexp(log_lb, log_1m_lb + jax.nn.log_sigmoid(fx)) for fx in fxs]
        ks_ = [(1.0 - lb) * jax.nn.sigmoid(-fx) for fx in fxs]
        sums = [_dot_sel(sel_ref[...], lf) for lf in lfs]
        atts = [jnp.where(diag, jnp.sum(q * k, -1, keepdims=True), 0.0) for q, k in zip(qs_, ks_)]
        for i, s in enumerate(HGRN_LEVELS):
            isq = ((rr // s) % 2) == 1
            parts = []
            for q, k, sm in zip(qs_, ks_, sums):
                e = jnp.exp(sm[(i + 1) * c:(i + 2) * c])
                parts.append(lax.dot_general(jnp.where(isq, q * e, 0.0).astype(BF16),
                                             jnp.where(isq, 0.0, k * e).astype(BF16), NT,
                                             preferred_element_type=F32))
            atts = [jnp.where(lev == (7 - i), part, att) for part, att in zip(parts, atts)]
        for n, off, q, k, sm, att in zip(ns, offs, qs_, ks_, sums, atts):
            b = sm[0:c]
            bl = b[c - 1:c, :]
            vb = zi_ref[0, pl.ds(off, c), :].astype(BF16)
            oi_s[pl.ds(off, c), :] = jnp.dot(att.astype(BF16), vb, preferred_element_type=F32)
            qd_s[pl.ds(off, c), :] = (q * jnp.exp(b)).astype(BF16)
            kd_s[pl.ds(off, c), :] = (k * jnp.exp(bl - b)).astype(BF16)
            ebl_s[pl.ds(pl.multiple_of(n * 8, 8), 8), :] = jnp.broadcast_to(jnp.exp(bl), (8, LANES))
        return carry

    lax.fori_loop(0, nchunk // nlock, intra, 0)

    def scan(n, st):
        off = pl.multiple_of(n * c, c)
        oi_s[pl.ds(off, c), :] = oi_s[pl.ds(off, c), :] + lax.dot_general(
            qd_s[pl.ds(off, c), :], st.astype(BF16), NT, preferred_element_type=F32)
        vb = zi_ref[0, pl.ds(off, c), :].astype(BF16)
        return (st * ebl_s[pl.ds(pl.multiple_of(n * 8, 8), 1), :]
                + lax.dot_general(vb, kd_s[pl.ds(off, c), :], TN, preferred_element_type=F32))

    lax.fori_loop(0, nchunk, scan, jnp.zeros((HEAD_DIM, HEAD_DIM), F32))
    o_ref[0] = (_rms(oi_s[...], nw_ref[...]) * _silu(zg_ref[0])).astype(BF16)


def hgrn2(z3, lb, norm_w):
    bsz, seq, _ = z3.shape
    zspec = lambda cb: pl.BlockSpec((1, seq, LANES), lambda b, h: (b, 0, cb + h))
    sel = _hgrn_selectors()
    return pl.pallas_call(
        functools.partial(_hgrn_body, seq=seq),
        grid=(bsz, N_HEADS),
        in_specs=[zspec(CD_Q), zspec(CD_F), zspec(CD_I), zspec(CD_G),
                  pl.BlockSpec((1, LANES), lambda b, h: (0, h)),
                  pl.BlockSpec((1, LANES), lambda b, h: (0, 0)),
                  pl.BlockSpec(sel.shape, lambda b, h: (0, 0))],
        out_specs=pl.BlockSpec((1, seq, LANES), lambda b, h: (b, 0, h)),
        out_shape=jax.ShapeDtypeStruct((bsz, seq, N_HEADS * HEAD_DIM), BF16),
        scratch_shapes=[pltpu.VMEM((seq, LANES), F32), pltpu.VMEM((seq // HGRN_CHUNK * 8, LANES), F32),
                        pltpu.VMEM((seq, LANES), BF16), pltpu.VMEM((seq, LANES), BF16)],
        compiler_params=_cparams(("parallel", "parallel")),
        name="hgrn2",
    )(z3, z3, z3, z3, lb.reshape(1, N_HEADS * HEAD_DIM), norm_w.reshape(1, HEAD_DIM), sel)


FOX_BLK = 256


def _fox_gate_body(sm_ref, bias_ref, cumt_ref, *, seq):
    c = FOX_BLK
    r = lax.broadcasted_iota(jnp.int32, (c, c), 0)
    cc = lax.broadcasted_iota(jnp.int32, (c, c), 1)
    ltri = (r >= cc).astype(F32)
    r1 = lax.broadcasted_iota(jnp.int32, (LANES, LANES), 0)
    c1 = lax.broadcasted_iota(jnp.int32, (LANES, LANES), 1)
    eye = (r1 == c1).astype(F32)

    def blk(n, carry):
        off = pl.multiple_of(n * c, c)
        ls = jax.nn.log_sigmoid(sm_ref[0, pl.ds(off, c), :] + bias_ref[...])
        cum = carry + jnp.dot(ltri, ls, precision=HI, preferred_element_type=F32)
        cumt_ref[0, :, pl.ds(off, c)] = lax.dot_general(eye, cum, NT, precision=HI, preferred_element_type=F32)
        return cum[c - 1:c, :]

    lax.fori_loop(0, seq // c, blk, jnp.zeros((1, LANES), F32))


def fox_gates(z3, bias_row):
    bsz, seq, _ = z3.shape
    return pl.pallas_call(
        functools.partial(_fox_gate_body, seq=seq),
        grid=(bsz,),
        in_specs=[pl.BlockSpec((1, seq, LANES), lambda b: (b, 0, CD_SMALL)),
                  pl.BlockSpec((1, LANES), lambda b: (0, 0))],
        out_specs=pl.BlockSpec((1, LANES, seq), lambda b: (b, 0, 0)),
        out_shape=jax.ShapeDtypeStruct((bsz, LANES, seq), F32),
        compiler_params=_cparams(("parallel",)),
        name="fox_gates",
    )(z3, bias_row)


ATT_ROWS = 128
ATT_TK = 256


def _for_blocks(n, unroll, body, carry=None):
    if isinstance(n, int):
        for j in range(n):
            carry = body(j, carry)
        return carry

    def it(i, c):
        for u in range(unroll):
            c = body(i * unroll + u, c)
        return c

    return lax.fori_loop(0, n // unroll, it, carry)


def _softmax_pv(s_scr, nblk, unroll, v_fn, m_scr, l_scr, acc_scr):
    _, rows, bw = s_scr.shape
    nlb = bw // LANES
    for ch in range(rows // ATT_ROWS):
        rs = slice(ch * ATT_ROWS, (ch + 1) * ATT_ROWS)

        def pmax(j, mel):
            for b in range(nlb):
                mel = jnp.maximum(mel, s_scr[j, rs, b * LANES:(b + 1) * LANES])
            return mel

        mel = _for_blocks(nblk, unroll, pmax, jnp.full((ATT_ROWS, LANES), NEG, F32))
        m_scr[rs, :] = jnp.broadcast_to(jnp.max(mel, -1, keepdims=True), (ATT_ROWS, LANES))
    l_scr[...] = jnp.zeros_like(l_scr)
    acc_scr[...] = jnp.zeros_like(acc_scr)

    def ppv(j, c):
        pcs = []
        for ch in range(rows // ATT_ROWS):
            rs = slice(ch * ATT_ROWS, (ch + 1) * ATT_ROWS)
            mb = m_scr[rs, :]
            lacc = l_scr[rs, :]
            ps = []
            for b in range(nlb):
                pb = jnp.exp(s_scr[j, rs, b * LANES:(b + 1) * LANES] - mb)
                lacc = lacc + pb
                ps.append(pb.astype(BF16))
            l_scr[rs, :] = lacc
            pcs.append(jnp.concatenate(ps, axis=1))
        acc_scr[...] += jnp.dot(jnp.concatenate(pcs, axis=0), v_fn(j), preferred_element_type=F32)
        return c

    _for_blocks(nblk, unroll, ppv)
    return acc_scr[...] / jnp.sum(l_scr[...], -1, keepdims=True)


def _softmax_scratch(rows, nblk, bw):
    return [pltpu.VMEM((nblk, rows, bw), F32), pltpu.VMEM((rows, LANES), F32),
            pltpu.VMEM((rows, LANES), F32), pltpu.VMEM((rows, HEAD_DIM), F32)]


def _fox_attn_body(q_ref, k_ref, v_ref, cumt_ref, o_ref, s_scr, m_scr, l_scr, acc_scr, *, tq):
    hd = pl.program_id(1)
    qi = pl.program_id(2)
    qall = (q_ref[0] * (HEAD_DIM ** -0.5)).astype(BF16)

    def scores(j):
        off = pl.multiple_of(j * tq, tq)
        kj = k_ref[0, pl.ds(off, tq), :].astype(BF16)
        ck = cumt_ref[0, pl.ds(hd % 8, 1), pl.ds(off, tq)]
        return lax.dot_general(qall, kj, NT, preferred_element_type=F32) - ck

    row = lax.broadcasted_iota(jnp.int32, (tq, tq), 0)
    col = lax.broadcasted_iota(jnp.int32, (tq, tq), 1)

    def v_blk(j):
        return v_ref[0, pl.ds(pl.multiple_of(j * tq, tq), tq), :].astype(BF16)

    def tile(nfull):
        for j in range(nfull):
            s_scr[j] = scores(j)
        s_scr[nfull] = jnp.where(col <= row, scores(nfull), NEG)
        o_ref[0] = _softmax_pv(s_scr, nfull + 1, 1, v_blk, m_scr, l_scr, acc_scr).astype(BF16)

    lax.switch(qi, [functools.partial(tile, n) for n in range(s_scr.shape[0])])


def fox_attention(z3, cumt):
    bsz, seq, _ = z3.shape
    tq = min(512, seq)
    full = lambda cb: pl.BlockSpec((1, seq, LANES), lambda b, h, i: (b, 0, cb + h))
    return pl.pallas_call(
        functools.partial(_fox_attn_body, tq=tq),
        grid=(bsz, N_HEADS, seq // tq),
        in_specs=[pl.BlockSpec((1, tq, LANES), lambda b, h, i: (b, i, CD_FQKV + h)),
                  full(CD_FQKV + 16), full(CD_FQKV + 32),
                  pl.BlockSpec((1, 8, seq), lambda b, h, i: (b, h // 8, 0))],
        out_specs=pl.BlockSpec((1, tq, LANES), lambda b, h, i: (b, i, h)),
        out_shape=jax.ShapeDtypeStruct((bsz, seq, N_HEADS * HEAD_DIM), BF16),
        scratch_shapes=_softmax_scratch(tq, seq // tq, tq),
        compiler_params=_cparams(("parallel", "parallel", "arbitrary")),
        name="fox_attention",
    )(z3, z3, z3, cumt)


def _nsa_cmp_body(zk_ref, zv_ref, pek_ref, pev_ref, wk1_ref, wk2_ref, wv1_ref, wv2_ref, kc_ref, vc_ref, *, nblk):
    def compress(z_ref, pe_ref, w1_ref, w2_ref):
        u1 = jnp.zeros((nblk, HEAD_DIM), F32)
        u2 = jnp.zeros((nblk, HEAD_DIM), F32)
        for l in range(S_CMP):
            zl = z_ref[0, :, l, :]
            u1 = u1 + jnp.dot(zl, w1_ref[l * HEAD_DIM:(l + 1) * HEAD_DIM, :], precision=HI,
                              preferred_element_type=F32)
            u2 = u2 + jnp.dot(zl, w1_ref[(S_CMP + l) * HEAD_DIM:(S_CMP + l + 1) * HEAD_DIM, :], precision=HI,
                              preferred_element_type=F32)
        pe = jnp.dot(jnp.broadcast_to(pe_ref[...], (8, L_CMP * HEAD_DIM)), w1_ref[...], precision=HI,
                     preferred_element_type=F32)[0:1]
        hmid = u1 + pltpu.roll(u2, nblk - 1, 0) + pe
        return jnp.dot(_silu(hmid), w2_ref[...], precision=HI, preferred_element_type=F32)

    kc_ref[0, 0] = compress(zk_ref, pek_ref, wk1_ref, wk2_ref)
    vc_ref[0, 0] = compress(zv_ref, pev_ref, wv1_ref, wv2_ref)


def nsa_compress(z3, pe_k, pe_v, wk1, wk2, wv1, wv2):
    bsz, seq, n = z3.shape
    nblk = seq // S_CMP
    z4 = z3.reshape(bsz, nblk, S_CMP, n)
    zspec = lambda cb: pl.BlockSpec((1, nblk, S_CMP, LANES), lambda b, g: (b, 0, 0, cb + g))
    wfull = lambda a: pl.BlockSpec(a.shape, lambda b, g: (0, 0))
    pek = pe_k.reshape(1, L_CMP * HEAD_DIM)
    pev = pe_v.reshape(1, L_CMP * HEAD_DIM)
    out = jax.ShapeDtypeStruct((bsz, G_NSA, nblk, HEAD_DIM), F32)
    ospec = pl.BlockSpec((1, 1, nblk, HEAD_DIM), lambda b, g: (b, g, 0, 0))
    return pl.pallas_call(
        functools.partial(_nsa_cmp_body, nblk=nblk),
        grid=(bsz, G_NSA),
        in_specs=[zspec(AB_NKV), zspec(AB_NKV + G_NSA), wfull(pek), wfull(pev),
                  wfull(wk1), wfull(wk2), wfull(wv1), wfull(wv2)],
        out_specs=[ospec, ospec],
        out_shape=[out, out],
        compiler_params=_cparams(("parallel", "parallel")),
        name="nsa_compress",
    )(z4, z4, pek, pev, wk1, wk2, wv1, wv2)


NSA_TQ = 128


def _nsa_expand(seq):
    m = np.arange(LANES)[:, None]
    t = np.arange(seq)[None, :]
    return jnp.asarray(((t // L_SLC) == m).astype(np.float32), dtype=BF16)


def _nsa_overlap(seq):
    cs = np.arange(seq // S_CMP)[:, None] * S_CMP
    ss = np.arange(LANES)[None, :] * L_SLC
    ov = (cs < ss + L_SLC) & (cs + L_CMP > ss) & (cs <= seq - L_CMP) & (ss < seq)
    return jnp.asarray(ov.astype(np.float32), dtype=BF16)


NSA_UNROLL = 2
NSA_WIN_BLOCKS = max(e // ATT_TK - max(e - (WINDOW + NSA_TQ - 2), 0) // ATT_TK + 1
                     for e in range(NSA_TQ - 1, 8 * WINDOW, NSA_TQ))


def _nsa_attn_body(q_ref, kc_ref, vc_ref, ks_ref, vs_ref, kw_ref, vw_ref, sm_ref, exp_ref, ovl_ref, o_ref,
                   s_scr, m_scr, l_scr, acc_scr, *, seq):
    g = pl.program_id(1)
    qi = pl.program_id(2)
    tq = NSA_TQ
    rows = HPG_NSA * tq
    nblk = seq // S_CMP
    n_slc = seq // L_SLC
    n_top = min(N_SEL, n_slc)
    q0 = qi * tq

    qall = q_ref[0]
    qs = jnp.concatenate([qall[:, p * HEAD_DIM:(p + 1) * HEAD_DIM] for p in range(HPG_NSA)], axis=0)
    qs = qs * (HEAD_DIM ** -0.5)
    qb = qs.astype(BF16)
    tpos = q0 + lax.broadcasted_iota(jnp.int32, (rows, 1), 0) % tq

    kc = kc_ref[0, 0]
    vc = vc_ref[0, 0]
    qh, ql = _split2(qs)
    kh, kl = _split2(kc)
    sc = lax.dot_general(jnp.concatenate([qh, qh, ql], axis=1), jnp.concatenate([kh, kl, kh], axis=1), NT,
                         preferred_element_type=F32)
    nidx = lax.broadcasted_iota(jnp.int32, (rows, nblk), 1)
    cmask = (nidx * S_CMP + (L_CMP - 1) <= tpos) & (nidx <= nblk - 2)
    scm = jnp.where(cmask, sc, NEG)
    mc = jnp.max(scm, -1, keepdims=True)
    ec = jnp.where(cmask, jnp.exp(scm - mc), 0.0)
    dc = jnp.sum(ec, -1, keepdims=True)
    p_cmp = ec / jnp.where(dc > 0, dc, 1.0)
    o_cmp = jnp.dot(p_cmp.astype(BF16), vc.astype(BF16), preferred_element_type=F32)

    psum = p_cmp[0:tq]
    for p in range(1, HPG_NSA):
        psum = psum + p_cmp[p * tq:(p + 1) * tq]
    ph, pl_ = _split2(psum)
    imp = jnp.dot(jnp.concatenate([ph, pl_], axis=1), jnp.concatenate([ovl_ref[...], ovl_ref[...]], axis=0),
                  preferred_element_type=F32)
    blk = lax.broadcasted_iota(jnp.int32, (tq, LANES), 1)
    cur = (q0 + lax.broadcasted_iota(jnp.int32, (tq, LANES), 0)) // L_SLC
    valid = blk <= cur
    forced = (blk == 0) | (blk == cur) | (blk == cur - 1)
    score = jnp.where(valid, jnp.where(forced, FORCE_SCORE, imp), NEG)
    rank = jnp.zeros((tq, LANES), jnp.int32)
    for mp in range(n_slc):
        colv = score[:, mp:mp + 1]
        rank = rank + ((colv > score) | ((colv == score) & (mp < blk))).astype(jnp.int32)
    selb = (valid & (rank < n_top)).astype(F32).astype(BF16)

    tk = ATT_TK
    qrow = q0 + lax.broadcasted_iota(jnp.int32, (tq, tk), 0)
    col = lax.broadcasted_iota(jnp.int32, (tq, tk), 1)

    def branch(k_ref, v_ref, first, nb, bias_fn):
        def scores(j, c):
            off = pl.multiple_of((first + j) * tk, tk)
            s = lax.dot_general(qb, k_ref[0, pl.ds(off, tk), :].astype(BF16), NT,
                                preferred_element_type=F32)
            bias = bias_fn(off)
            for p in range(HPG_NSA):
                s_scr[j, p * tq:(p + 1) * tq, :] = s[p * tq:(p + 1) * tq] + bias
            return c

        _for_blocks(nb, NSA_UNROLL, scores)

        def v_blk(j):
            return v_ref[0, pl.ds(pl.multiple_of((first + j) * tk, tk), tk), :].astype(BF16)

        return _softmax_pv(s_scr, nb, NSA_UNROLL, v_blk, m_scr, l_scr, acc_scr)

    def slc_bias(off):
        sel = jnp.dot(selb, exp_ref[:, pl.ds(off, tk)], preferred_element_type=F32)
        return jnp.where((sel > 0.5) & (off + col <= qrow), 0.0, NEG)

    last = (q0 + tq - 1) // tk
    o_slc = branch(ks_ref, vs_ref, 0, (last // NSA_UNROLL + 1) * NSA_UNROLL, slc_bias)

    def win_bias(off):
        d = qrow - (off + col)
        return jnp.where((d >= 0) & (d < WINDOW), 0.0, NEG)

    nwin = min(NSA_WIN_BLOCKS, seq // tk)
    o_win = branch(kw_ref, vw_ref, jnp.maximum(last - (nwin - 1), 0), nwin, win_bias)

    sg = jax.nn.sigmoid(sm_ref[0])
    for p in range(HPG_NSA):
        sl = slice(p * tq, (p + 1) * tq)
        base = 2 * N_HEADS + g * HPG_NSA + p
        o = (_lane_col(sg, blk, base) * o_cmp[sl] + _lane_col(sg, blk, base + N_HEADS) * o_slc[sl]
             + _lane_col(sg, blk, base + 2 * N_HEADS) * o_win[sl])
        o_ref[0, :, p * HEAD_DIM:(p + 1) * HEAD_DIM] = o.astype(BF16)


def nsa_attention(z3, kc, vc):
    bsz, seq, _ = z3.shape
    tq = NSA_TQ
    assert seq % tq == 0 and (seq // ATT_TK) % NSA_UNROLL == 0 and seq // L_SLC <= LANES
    nblk = seq // S_CMP
    full = lambda cb: pl.BlockSpec((1, seq, LANES), lambda b, g, i: (b, 0, cb + g))
    cspec = pl.BlockSpec((1, 1, nblk, HEAD_DIM), lambda b, g, i: (b, g, 0, 0))
    qw = HPG_NSA * HEAD_DIM
    return pl.pallas_call(
        functools.partial(_nsa_attn_body, seq=seq),
        grid=(bsz, G_NSA, seq // tq),
        in_specs=[pl.BlockSpec((1, tq, qw), lambda b, g, i: (b, i, AB_NQ * LANES // qw + g)),
                  cspec, cspec,
                  full(AB_NKV + 2 * G_NSA), full(AB_NKV + 3 * G_NSA),
                  full(AB_NKV + 4 * G_NSA), full(AB_NKV + 5 * G_NSA),
                  pl.BlockSpec((1, tq, LANES), lambda b, g, i: (b, i, AB_SMALL)),
                  pl.BlockSpec((LANES, seq), lambda b, g, i: (0, 0)),
                  pl.BlockSpec((nblk, LANES), lambda b, g, i: (0, 0))],
        out_specs=pl.BlockSpec((1, tq, qw), lambda b, g, i: (b, i, g)),
        out_shape=jax.ShapeDtypeStruct((bsz, seq, N_HEADS * HEAD_DIM), BF16),
        scratch_shapes=_softmax_scratch(HPG_NSA * tq, seq // ATT_TK, ATT_TK),
        compiler_params=_cparams(("parallel", "parallel", "arbitrary")),
        name="nsa_attention",
    )(z3, kc, vc, z3, z3, z3, z3, z3, _nsa_expand(seq), _nsa_overlap(seq))


def _repack_body(w_ref, o_ref, *, sections, pad_from):
    o_ref[pad_from:, :] = jnp.zeros((o_ref.shape[0] - pad_from, o_ref.shape[1]), BF16)
    for dst, src, width in sections:
        o_ref[dst:dst + width, :] = w_ref[src:src + width, :].astype(BF16)


def _repack_weight(w, sections, n_out, pad_from, tc=256):
    wt = jnp.swapaxes(w, 0, 1)
    n_in, k = wt.shape
    return pl.pallas_call(
        functools.partial(_repack_body, sections=sections, pad_from=pad_from),
        grid=(k // tc,),
        in_specs=[pl.BlockSpec((n_in, tc), lambda i: (0, i))],
        out_specs=pl.BlockSpec((n_out, tc), lambda i: (0, i)),
        out_shape=jax.ShapeDtypeStruct((n_out, k), BF16),
        compiler_params=_cparams(("parallel",)),
        name="repack_weight",
    )(wt)


def _ab_weight(w):
    small = AB_SMALL * LANES
    sections = ((0, 0, 6144), (6144, 6176, 2048), (8192, 8224, 2048), (10240, 10272, 3072),
                (small, 6144, 32), (small + 32, 13344, 48))
    return _repack_weight(w, sections, AB_N, small)


def _cd_weight(w):
    small = CD_SMALL * LANES
    return _repack_weight(w, ((0, 0, small), (small, small, 16)), CD_N, small)


def _row128(v):
    return jnp.pad(v.astype(F32), (0, LANES - v.shape[0])).reshape(1, LANES)


def kernel(x, p, ab_norm_pre, ab_norm_post, ab_w_in, gdn_conv_w, gdn_a_log, gdn_dt_bias, gdn_norm, nsa_pe_k, nsa_pe_v, nsa_cmp_k1, nsa_cmp_k2, nsa_cmp_v1, nsa_cmp_v2, ab_w_out, cd_norm_pre, cd_norm_post, cd_w_in, hgrn_lb_logits, hgrn_norm, fox_f_bias, cd_w_out, ffn_norm_pre, ffn_norm_post, ffn_w_up, ffn_conv_w, ffn_conv_b, ffn_w_down, ple_w_proj, ple_gate_norm, ple_w_gate, ple_norm_post):
    bsz, seq, dm = x.shape
    depth = p.shape[0]
    m = bsz * seq
    half = N_HEADS * HEAD_DIM
    sm_ = jax.nn.softmax(hgrn_lb_logits.astype(F32), axis=0)
    lb_table = jnp.cumsum(sm_, axis=0) - sm_[0]
    xf = x.reshape(m, dm)
    pf = p.reshape(depth, m, -1)
    w_up, w_down = ffn_w_up.astype(BF16), ffn_w_down.astype(BF16)
    w_gate, w_proj = ple_w_gate.astype(BF16), ple_w_proj.astype(BF16)
    for li in range(depth):
        j = li // 2
        if li % 2 == 0:
            z3 = norm_matmul(xf, ab_norm_pre[j], _ab_weight(ab_w_in[j])).reshape(bsz, seq, AB_N)
            prm = jnp.concatenate([_row128(gdn_a_log[j]), _row128(gdn_dt_bias[j]),
                                   jnp.zeros((6, LANES), F32)], axis=0)
            o_a = gated_deltanet(z3, gdn_conv_w[j], prm, gdn_norm[j])
            kc, vc = nsa_compress(z3, nsa_pe_k[j], nsa_pe_v[j], nsa_cmp_k1[j], nsa_cmp_k2[j],
                                  nsa_cmp_v1[j], nsa_cmp_v2[j])
            o_b = nsa_attention(z3, kc, vc)
            w_out, post = ab_w_out[j], ab_norm_post[j]
        else:
            z3 = norm_matmul(xf, cd_norm_pre[j], _cd_weight(cd_w_in[j])).reshape(bsz, seq, CD_N)
            o_a = hgrn2(z3, lb_table[li], hgrn_norm[j])
            o_b = fox_attention(z3, fox_gates(z3, _row128(fox_f_bias[j])))
            w_out, post = cd_w_out[j], cd_norm_post[j]
        xf = outproj(o_a.reshape(m, half), o_b.reshape(m, half), w_out.astype(BF16), xf, post)
        xf = conv_ffn(xf, li, ffn_norm_pre, w_up, ffn_conv_w, ffn_conv_b, w_down, ffn_norm_post, seq)
        xf = ple(xf, li, pf, ple_gate_norm, w_gate, w_proj, ple_norm_post)
    return xf.reshape(bsz, seq, dm)
```

```python
import functools

import numpy as np
import jax
import jax.numpy as jnp
from jax import lax
from jax.experimental import pallas as pl
from jax.experimental.pallas import tpu as pltpu

F32 = jnp.float32
BF16 = jnp.bfloat16
HI = lax.Precision.HIGHEST
NT = (((1,), (1,)), ((), ()))
TN = (((0,), (0,)), ((), ()))

D_MODEL = 4096
HEAD_DIM = 128
N_HEADS = 16
G_NSA = 4
HPG_NSA = 4
L_CMP = 32
S_CMP = 16
L_SLC = 64
N_SEL = 8
WINDOW = 512
FORCE_SCORE = 1e4
GDN_CHUNK = 64
GDN_GROUP = 256
GDN_LOCKSTEP = 8
GDN_HEADS = 2
HGRN_CHUNK = 256
HGRN_LOCKSTEP = 2
D_FF = 11008
EPS = 1e-6
NEG = -1e30
LANES = 128
SMALL_W = 512

AB_QKV, AB_GATE, AB_NQ, AB_NKV, AB_SMALL = 0, 48, 64, 80, 104
AB_N = 108 * LANES
CD_Q, CD_F, CD_I, CD_G, CD_FQKV, CD_SMALL = 0, 16, 32, 48, 64, 112
CD_N = 116 * LANES

VMEM_LIMIT = 56 * 1024 * 1024


def _cparams(sem):
    return pltpu.CompilerParams(dimension_semantics=sem, vmem_limit_bytes=VMEM_LIMIT)


def _rms(x, w):
    return x * lax.rsqrt(jnp.mean(x * x, axis=-1, keepdims=True) + EPS) * w


def _silu(x):
    return x * jax.nn.sigmoid(x)


def _lane_col(x, lane_idx, lane):
    return jnp.sum(jnp.where(lane_idx == lane, x, 0.0), axis=-1, keepdims=True)


def _split2(x):
    hi = x.astype(BF16)
    return hi, (x - hi.astype(F32)).astype(BF16)


def _dot_sel(sel, x):
    n = x.shape[1]
    hi, lo = _split2(x)
    y = jnp.dot(sel, jnp.concatenate([hi, lo], axis=1), preferred_element_type=F32)
    return y[:, :n] + y[:, n:]


def _norm_matmul_body(x_ref, nw_ref, w_ref, o_ref, h_ref):
    @pl.when(pl.program_id(1) == 0)
    def _():
        h_ref[...] = _rms(x_ref[...], nw_ref[...]).astype(BF16)

    o_ref[...] = lax.dot_general(h_ref[...], w_ref[...], NT, preferred_element_type=F32)


def norm_matmul(x, nw, wt, tm=512, tn=512):
    m, k = x.shape
    n = wt.shape[0]
    return pl.pallas_call(
        _norm_matmul_body,
        grid=(m // tm, n // tn),
        in_specs=[pl.BlockSpec((tm, k), lambda i, j: (i, 0)),
                  pl.BlockSpec((1, k), lambda i, j: (0, 0)),
                  pl.BlockSpec((tn, k), lambda i, j: (j, 0))],
        out_specs=pl.BlockSpec((tm, tn), lambda i, j: (i, j)),
        out_shape=jax.ShapeDtypeStruct((m, n), F32),
        scratch_shapes=[pltpu.VMEM((tm, k), BF16)],
        compiler_params=_cparams(("parallel", "arbitrary")),
        name="norm_matmul",
    )(x, nw.reshape(1, k), wt)


def _outproj_body(oa_ref, ob_ref, wa_ref, wb_ref, x_ref, nw_ref, o_ref, *, tn):
    j = pl.program_id(1)
    y = jnp.dot(oa_ref[...], wa_ref[...], preferred_element_type=F32)
    y = y + jnp.dot(ob_ref[...], wb_ref[...], preferred_element_type=F32)
    o_ref[:, pl.ds(pl.multiple_of(j * tn, tn), tn)] = y

    @pl.when(j == pl.num_programs(1) - 1)
    def _():
        o_ref[...] = x_ref[...] + _rms(o_ref[...], nw_ref[...])


def outproj(oa, ob, w, x, nw, tm=512, tn=512):
    m, ka = oa.shape
    n = w.shape[1]
    return pl.pallas_call(
        functools.partial(_outproj_body, tn=tn),
        grid=(m // tm, n // tn),
        in_specs=[pl.BlockSpec((tm, ka), lambda i, j: (i, 0)),
                  pl.BlockSpec((tm, ka), lambda i, j: (i, 0)),
                  pl.BlockSpec((ka, tn), lambda i, j: (0, j)),
                  pl.BlockSpec((ka, tn), lambda i, j: (1, j)),
                  pl.BlockSpec((tm, n), lambda i, j: (i, 0), pipeline_mode=pl.Buffered(1)),
                  pl.BlockSpec((1, n), lambda i, j: (0, 0))],
        out_specs=pl.BlockSpec((tm, n), lambda i, j: (i, 0)),
        out_shape=jax.ShapeDtypeStruct((m, n), F32),
        compiler_params=_cparams(("parallel", "arbitrary")),
        name="outproj",
    )(oa, ob, w, w, x, nw.reshape(1, n))


FFN_HALO = 16


def _ffn_body(x_ref, xh_ref, nw_ref, wg_ref, wu_ref, cwg_ref, cwu_ref, cbg_ref, cbu_ref,
              wd_ref, pw_ref, o_ref, h_s, *, tm, seq):
    i = pl.program_id(0)
    j = pl.program_id(1)

    @pl.when(j == 0)
    def _():
        keep = jnp.where((i * tm) % seq == 0, 0.0, 1.0)
        h_s[0:FFN_HALO, :] = (_rms(xh_ref[...], nw_ref[...]) * keep).astype(BF16)
        h_s[FFN_HALO:, :] = _rms(x_ref[...], nw_ref[...]).astype(BF16)
        o_ref[...] = jnp.zeros_like(o_ref)

    h = h_s[...]

    def branch(w_ref, cw_ref, cb_ref):
        u = jnp.dot(h, w_ref[...], preferred_element_type=F32)
        cw = cw_ref[...]
        y = u * cw[2:3] + pltpu.roll(u, 1, 0) * cw[1:2] + pltpu.roll(u, 2, 0) * cw[0:1]
        return y[FFN_HALO:] + cb_ref[...]

    g = branch(wg_ref, cwg_ref, cbg_ref)
    u = branch(wu_ref, cwu_ref, cbu_ref)
    act = (_silu(g) * u).astype(BF16)
    o_ref[...] += jnp.dot(act, wd_ref[...], preferred_element_type=F32)

    @pl.when(j == pl.num_programs(1) - 1)
    def _():
        o_ref[...] = x_ref[...] + _rms(o_ref[...], pw_ref[...])


def conv_ffn(x, li, nw, w_up, conv_w, conv_b, w_down, pw, seq, tm=512, tf=256):
    m, k = x.shape
    nl, f, _ = w_down.shape
    nf = f // tf
    hb = tm // FFN_HALO
    cb = conv_b.reshape(nl, 1, 2 * f)
    row = pl.BlockSpec((None, 1, k), lambda i, j: (li, 0, 0))
    return pl.pallas_call(
        functools.partial(_ffn_body, tm=tm, seq=seq),
        grid=(m // tm, nf),
        in_specs=[pl.BlockSpec((tm, k), lambda i, j: (i, 0), pipeline_mode=pl.Buffered(1)),
                  pl.BlockSpec((FFN_HALO, k), lambda i, j: (jnp.maximum(i * hb - 1, 0), 0)),
                  row,
                  pl.BlockSpec((None, k, tf), lambda i, j: (li, 0, j)),
                  pl.BlockSpec((None, k, tf), lambda i, j: (li, 0, nf + j)),
                  pl.BlockSpec((None, 3, tf), lambda i, j: (li, 0, j)),
                  pl.BlockSpec((None, 3, tf), lambda i, j: (li, 0, nf + j)),
                  pl.BlockSpec((None, 1, tf), lambda i, j: (li, 0, j)),
                  pl.BlockSpec((None, 1, tf), lambda i, j: (li, 0, nf + j)),
                  pl.BlockSpec((None, tf, k), lambda i, j: (li, j, 0)),
                  row],
        out_specs=pl.BlockSpec((tm, k), lambda i, j: (i, 0)),
        out_shape=jax.ShapeDtypeStruct((m, k), F32),
        scratch_shapes=[pltpu.VMEM((tm + FFN_HALO, k), BF16)],
        compiler_params=_cparams(("parallel", "arbitrary")),
        name="conv_ffn",
    )(x, x, nw.reshape(nl, 1, k), w_up, w_up, conv_w, conv_w, cb, cb, w_down, pw.reshape(nl, 1, k))


def _ple_body(x_ref, p_ref, gnw_ref, wg_ref, wp_ref, pnw_ref, o_ref, h_s, *, tn):
    j = pl.program_id(1)

    @pl.when(j == 0)
    def _():
        h_s[...] = _rms(x_ref[...], gnw_ref[...]).astype(BF16)

    gate = jax.nn.sigmoid(jnp.dot(h_s[...], wg_ref[...], preferred_element_type=F32))
    proj = jnp.dot(p_ref[...].astype(BF16), wp_ref[...], preferred_element_type=F32)
    o_ref[:, pl.ds(pl.multiple_of(j * tn, tn), tn)] = gate * proj

    @pl.when(j == pl.num_programs(1) - 1)
    def _():
        o_ref[...] = x_ref[...] + _rms(o_ref[...], pnw_ref[...])


def ple(x, li, p, gnw, wg, wp, pnw, tm=512, tn=512):
    m, k = x.shape
    nl, _, dp = p.shape
    row = pl.BlockSpec((None, 1, k), lambda i, j: (li, 0, 0))
    return pl.pallas_call(
        functools.partial(_ple_body, tn=tn),
        grid=(m // tm, k // tn),
        in_specs=[pl.BlockSpec((tm, k), lambda i, j: (i, 0), pipeline_mode=pl.Buffered(1)),
                  pl.BlockSpec((None, tm, dp), lambda i, j: (li, i, 0)),
                  row,
                  pl.BlockSpec((None, k, tn), lambda i, j: (li, 0, j)),
                  pl.BlockSpec((None, dp, tn), lambda i, j: (li, 0, j)),
                  row],
        out_specs=pl.BlockSpec((tm, k), lambda i, j: (i, 0)),
        out_shape=jax.ShapeDtypeStruct((m, k), F32),
        scratch_shapes=[pltpu.VMEM((tm, k), BF16)],
        compiler_params=_cparams(("parallel", "arbitrary")),
        name="ple",
    )(x, p, gnw.reshape(nl, 1, k), wg, wp, pnw.reshape(nl, 1, k))


def _gdn_ltri():
    r = np.arange(GDN_GROUP)[:, None]
    t = np.arange(GDN_GROUP)[None, :]
    return jnp.asarray((((r // GDN_CHUNK) == (t // GDN_CHUNK)) & (r >= t)).astype(np.float32), dtype=BF16)


def _gdn_head_setup(hd, lanes, zq_ref, zk_ref, zv_ref, sm_ref, cwq_ref, cwk_ref, cwv_ref, prm_ref, ltri_ref,
                    q_s, k_s, v_s, g_s, b_s, o0_s, n0_s, gl_s, qe_s, m_s, seq):
    c = GDN_CHUNK
    gs = GDN_GROUP
    row = lax.broadcasted_iota(jnp.int32, (seq, LANES), 0)
    lane = lax.broadcasted_iota(jnp.int32, (seq, LANES), 1)

    def conv_silu(z_ref, w_ref):
        z = z_ref[0, :, lanes]
        w = w_ref[:, lanes]
        y = z * w[3:4]
        for s in (1, 2, 3):
            y = y + jnp.where(row >= s, pltpu.roll(z, s, 0), 0.0) * w[3 - s:4 - s]
        return _silu(y)

    q = conv_silu(zq_ref, cwq_ref)
    k = conv_silu(zk_ref, cwk_ref)
    q_s[...] = q * lax.rsqrt(jnp.sum(q * q, -1, keepdims=True) + EPS) * (HEAD_DIM ** -0.5)
    k_s[...] = k * lax.rsqrt(jnp.sum(k * k, -1, keepdims=True) + EPS)
    v_s[...] = conv_silu(zv_ref, cwv_ref)

    sm = sm_ref[0]
    glog = -jnp.exp(prm_ref[0:1, :]) * jax.nn.softplus(sm + prm_ref[1:2, :])
    g_s[...] = jnp.broadcast_to(_lane_col(glog, lane, hd), (seq, LANES))
    b_s[...] = jnp.broadcast_to(_lane_col(jax.nn.sigmoid(sm), lane, N_HEADS + hd), (seq, LANES))

    nper = gs // c
    rs = lax.broadcasted_iota(jnp.int32, (c, gs), 0)
    cs = lax.broadcasted_iota(jnp.int32, (c, gs), 1)
    jj = cs % c
    cblk = cs // c
    incl = rs >= jj
    strict = rs > jj
    eye_side = (rs == jj).astype(F32)
    r2 = lax.broadcasted_iota(jnp.int32, (gs, gs), 0)
    c2 = lax.broadcasted_iota(jnp.int32, (gs, gs), 1)
    bdm = ((r2 // c) == (c2 // c)).astype(F32)
    bdm_b = bdm.astype(BF16)
    lane_c = lax.broadcasted_iota(jnp.int32, (c, LANES), 1)
    lane_g = lax.broadcasted_iota(jnp.int32, (gs, LANES), 1)
    ones3 = (lane_c < 3).astype(F32).astype(BF16)

    def side_col(xb):
        left = jnp.where(lane_c < c, xb[0:c], xb[c:2 * c])
        right = jnp.where(lane_c < c, xb[2 * c:3 * c], xb[3 * c:4 * c])
        return jnp.concatenate([left, right], axis=1)

    def to_side(full):
        out = full[0:c]
        for ci in range(1, nper):
            out = jnp.where(cblk == ci, full[ci * c:(ci + 1) * c], out)
        return out

    def bd(xs, mask):
        return jnp.concatenate([xs] * nper, axis=0) * mask

    def mm3(ph, pl_, xh, xl):
        bh = bd(xh, bdm_b)
        lhs = jnp.concatenate([ph, ph, pl_], axis=1)
        rhs = jnp.concatenate([bh, bd(xl, bdm_b), bh], axis=0)
        return jnp.dot(lhs, rhs, preferred_element_type=F32)

    def group_pre(gi):
        off = pl.multiple_of(gi * gs, gs)
        qg = q_s[pl.ds(off, gs), :]
        kg = k_s[pl.ds(off, gs), :]
        bg = b_s[pl.ds(off, gs), :]
        gam = _dot_sel(ltri_ref[...], g_s[pl.ds(off, gs), :])
        g_hi = gam.astype(BF16).astype(F32)
        g_mid = (gam - g_hi).astype(BF16).astype(F32)
        g_lo = gam - g_hi - g_mid
        gam3 = jnp.where(lane_g == 0, g_hi, jnp.where(lane_g == 1, g_mid, jnp.where(lane_g == 2, g_lo, 0.0)))
        gam_row = lax.dot_general(ones3, gam3.astype(BF16), NT, preferred_element_type=F32)
        dec = jnp.where(incl, jnp.exp(jnp.where(incl, side_col(gam) - gam_row, 0.0)), 0.0)
        qb = qg.astype(BF16)
        kb = kg.astype(BF16)
        kk = to_side(lax.dot_general(kb, kb, NT, preferred_element_type=F32))
        a = jnp.where(strict, side_col(bg) * dec * kk, 0.0)
        return dict(off=off, qg=qg, kg=kg, bg=bg, gam=gam, dec=dec, qb=qb, kb=kb, a=a)

    def group_post(gi, st, tinv):
        off, qg, kg, bg, gam, dec, qb, kb = (st[n] for n in ("off", "qg", "kg", "bg", "gam", "dec", "qb", "kb"))
        vg = v_s[pl.ds(off, gs), :]
        th, tl = _split2(tinv)
        eg = jnp.exp(gam)
        rh, rl = _split2(jnp.concatenate([vg * bg, kg * (bg * eg)], axis=1))
        bth = bd(th, bdm_b)
        sol = jnp.dot(jnp.concatenate([bth, bth, bd(tl, bdm_b)], axis=1),
                      jnp.concatenate([rh, rl, rh], axis=0), preferred_element_type=F32)
        solb = sol.astype(BF16)
        qk = bd(dec, bdm) * lax.dot_general(qb, kb, NT, preferred_element_type=F32)
        qkuw = jnp.dot(qk.astype(BF16), solb, preferred_element_type=F32)
        o0_s[pl.ds(off, gs), :] = qkuw[:, :HEAD_DIM]
        qe_s[pl.ds(off, gs), :] = (qg * eg - qkuw[:, HEAD_DIM:]).astype(BF16)
        for ci in range(nper):
            lo = ci * c
            gl = gam[lo + c - 1:lo + c, :]
            kd = (kg[lo:lo + c] * jnp.exp(gl - gam[lo:lo + c])).astype(BF16)
            kds = lax.dot_general(kd, solb[lo:lo + c], TN, preferred_element_type=F32)
            so = pl.multiple_of((gi * nper + ci) * HEAD_DIM, HEAD_DIM)
            n0_s[pl.ds(so, HEAD_DIM), :] = kds[:, :HEAD_DIM]
            m_s[pl.ds(so, HEAD_DIM), :] = (-kds[:, HEAD_DIM:]).astype(BF16)
            gl_s[pl.ds(pl.multiple_of((gi * nper + ci) * 8, 8), 8), :] = jnp.broadcast_to(jnp.exp(gl), (8, LANES))

    ngroups = seq // gs
    nlock = GDN_LOCKSTEP if ngroups % GDN_LOCKSTEP == 0 else 1

    def groups(it, carry):
        gis = [it + u * (ngroups // nlock) for u in range(nlock)]
        sts = [group_pre(gi) for gi in gis]
        tinvs = [eye_side - st["a"] for st in sts]
        xs = [_split2(st["a"]) for st in sts]
        for _ in range(5):
            xs = [_split2(mm3(xh, xl, xh, xl)) for xh, xl in xs]
            ts = [_split2(t) for t in tinvs]
            tinvs = [t + mm3(th, tl, xh, xl) for t, (th, tl), (xh, xl) in zip(tinvs, ts, xs)]
        for gi, st, t in zip(gis, sts, tinvs):
            group_post(gi, st, t)
        return carry

    lax.fori_loop(0, ngroups // nlock, groups, 0)


def _gdn_body(zq_ref, zk_ref, zv_ref, zg_ref, sm_ref, cwq_ref, cwk_ref, cwv_ref, prm_ref, nw_ref, ltri_ref,
              o_ref, *scratch, seq):
    c = GDN_CHUNK
    nh = GDN_HEADS
    o0_s, n0_s, gl_s, qe_s, m_s = scratch[5:]
    for hh in range(nh):
        _gdn_head_setup(pl.program_id(1) * nh + hh, slice(hh * LANES, (hh + 1) * LANES), zq_ref, zk_ref, zv_ref,
                        sm_ref, cwq_ref, cwk_ref, cwv_ref, prm_ref, ltri_ref, *(r.at[hh] for r in scratch), seq)

    def step(n, states):
        off = pl.multiple_of(n * c, c)
        so = pl.multiple_of(n * HEAD_DIM, HEAD_DIM)
        sbs = [s.astype(BF16) for s in states]
        for hh in range(nh):
            o0_s[hh, pl.ds(off, c), :] = o0_s[hh, pl.ds(off, c), :] + jnp.dot(
                qe_s[hh, pl.ds(off, c), :], sbs[hh], preferred_element_type=F32)
        ms = [jnp.dot(m_s[hh, pl.ds(so, HEAD_DIM), :], sbs[hh], preferred_element_type=F32) for hh in range(nh)]
        return tuple(gl_s[hh, pl.ds(pl.multiple_of(n * 8, 8), 1), :] * states[hh] + n0_s[hh, pl.ds(so, HEAD_DIM), :]
                     + ms[hh] for hh in range(nh))

    lax.fori_loop(0, seq // c, step, tuple(jnp.zeros((HEAD_DIM, HEAD_DIM), F32) for _ in range(nh)))
    for hh in range(nh):
        lanes = slice(hh * LANES, (hh + 1) * LANES)
        o_ref[0, :, lanes] = (_rms(o0_s[hh], nw_ref[...]) * _silu(zg_ref[0, :, lanes])).astype(BF16)


def gated_deltanet(z3, conv_w, prm, norm_w):
    bsz, seq, _ = z3.shape
    nh = GDN_HEADS
    zspec = lambda cb: pl.BlockSpec((1, seq, nh * LANES), lambda b, h: (b, 0, cb // nh + h))
    wspec = lambda cb: pl.BlockSpec((4, nh * LANES), lambda b, h: (0, cb // nh + h))
    nchunk = seq // GDN_CHUNK
    return pl.pallas_call(
        functools.partial(_gdn_body, seq=seq),
        grid=(bsz, N_HEADS // nh),
        in_specs=[zspec(AB_QKV), zspec(AB_QKV + 16), zspec(AB_QKV + 32), zspec(AB_GATE),
                  pl.BlockSpec((1, seq, LANES), lambda b, h: (b, 0, AB_SMALL)),
                  wspec(0), wspec(16), wspec(32),
                  pl.BlockSpec((8, LANES), lambda b, h: (0, 0)),
                  pl.BlockSpec((1, LANES), lambda b, h: (0, 0)),
                  pl.BlockSpec((GDN_GROUP, GDN_GROUP), lambda b, h: (0, 0))],
        out_specs=pl.BlockSpec((1, seq, nh * LANES), lambda b, h: (b, 0, h)),
        out_shape=jax.ShapeDtypeStruct((bsz, seq, N_HEADS * HEAD_DIM), BF16),
        scratch_shapes=[pltpu.VMEM((nh, seq, LANES), F32) for _ in range(6)]
        + [pltpu.VMEM((nh, nchunk * HEAD_DIM, HEAD_DIM), F32),
           pltpu.VMEM((nh, nchunk * 8, LANES), F32),
           pltpu.VMEM((nh, seq, LANES), BF16),
           pltpu.VMEM((nh, nchunk * HEAD_DIM, HEAD_DIM), BF16)],
        compiler_params=_cparams(("parallel", "parallel")),
        name="gated_deltanet",
    )(z3, z3, z3, z3, z3, conv_w, conv_w, conv_w, prm, norm_w.reshape(1, HEAD_DIM), _gdn_ltri())


HGRN_LEVELS = tuple(HGRN_CHUNK >> (i + 1) for i in range(8))


def _hgrn_selectors():
    c = HGRN_CHUNK
    r = np.arange(c)[:, None]
    t = np.arange(c)[None, :]
    mats = [r >= t]
    for s in HGRN_LEVELS:
        isq = ((r // s) % 2) == 1
        mats.append(((r // s) == (t // s)) & ((isq & (t <= r)) | (~isq & (t > r))))
    return jnp.asarray(np.concatenate(mats, axis=0).astype(np.float32), dtype=BF16)


def _hgrn_body(zq_ref, zf_ref, zi_ref, zg_ref, lb_ref, nw_ref, sel_ref, o_ref, oi_s, ebl_s, qd_s, kd_s, *, seq):
    c = HGRN_CHUNK
    r = lax.broadcasted_iota(jnp.int32, (c, c), 0)
    cc = lax.broadcasted_iota(jnp.int32, (c, c), 1)
    diag = r == cc
    x = r ^ cc
    lev = jnp.zeros((c, c), jnp.int32)
    for bit in range(1, 8):
        lev = lev + (x >= (1 << bit)).astype(jnp.int32)
    lev = jnp.where(r > cc, lev, -1)
    rr = lax.broadcasted_iota(jnp.int32, (c, LANES), 0)
    lb = lb_ref[...]
    log_lb = jnp.log(lb)
    log_1m_lb = jnp.log1p(-lb)

    nchunk = seq // c
    nlock = HGRN_LOCKSTEP if nchunk % HGRN_LOCKSTEP == 0 else 1

    def intra(it, carry):
        ns = [it + u * (nchunk // nlock) for u in range(nlock)]
        offs = [pl.multiple_of(n * c, c) for n in ns]
        fxs = [zf_ref[0, pl.ds(off, c), :] for off in offs]
        qs_ = [_silu(zq_ref[0, pl.ds(off, c), :]) for off in offs]
        lfs, ks_ = [], []
        for fx in fxs:
            t = jnp.exp(-jnp.abs(fx))
            gate = log_1m_lb + (jnp.minimum(fx, 0.0) - jnp.log(1.0 + t))
            lfs.append(jnp.maximum(log_lb, gate) + jnp.log(1.0 + jnp.exp(-jnp.abs(log_lb - gate))))
            ks_.append((1.0 - lb) * (jnp.where(fx >= 0.0, t, 1.0) / (1.0 + t)))
        sums = [_dot_sel(sel_ref[...], lf) for lf in lfs]
        atts = [jnp.where(diag, jnp.sum(q * k, -1, keepdims=True), 0.0) for q, k in zip(qs_, ks_)]
        for i, s in enumerate(HGRN_LEVELS):
            isq = ((rr // s) % 2) == 1
            parts = []
            for q, k, sm in zip(qs_, ks_, sums):
                e = jnp.exp(sm[(i + 1) * c:(i + 2) * c])
                parts.append(lax.dot_general(jnp.where(isq, q * e, 0.0).astype(BF16),
                                             jnp.where(isq, 0.0, k * e).astype(BF16), NT,
                                             preferred_element_type=F32))
            atts = [jnp.where(lev == (7 - i), part, att) for part, att in zip(parts, atts)]
        for n, off, q, k, sm, att in zip(ns, offs, qs_, ks_, sums, atts):
            b = sm[0:c]
            bl = b[c - 1:c, :]
            vb = zi_ref[0, pl.ds(off, c), :].astype(BF16)
            oi_s[pl.ds(off, c), :] = jnp.dot(att.astype(BF16), vb, preferred_element_type=F32)
            qd_s[pl.ds(off, c), :] = (q * jnp.exp(b)).astype(BF16)
            kd_s[pl.ds(off, c), :] = (k * jnp.exp(bl - b)).astype(BF16)
            ebl_s[pl.ds(pl.multiple_of(n * 8, 8), 8), :] = jnp.broadcast_to(jnp.exp(bl), (8, LANES))
        return carry

    lax.fori_loop(0, nchunk // nlock, intra, 0)

    def scan(n, st):
        off = pl.multiple_of(n * c, c)
        oi_s[pl.ds(off, c), :] = oi_s[pl.ds(off, c), :] + lax.dot_general(
            qd_s[pl.ds(off, c), :], st.astype(BF16), NT, preferred_element_type=F32)
        vb = zi_ref[0, pl.ds(off, c), :].astype(BF16)
        return (st * ebl_s[pl.ds(pl.multiple_of(n * 8, 8), 1), :]
                + lax.dot_general(vb, kd_s[pl.ds(off, c), :], TN, preferred_element_type=F32))

    lax.fori_loop(0, nchunk, scan, jnp.zeros((HEAD_DIM, HEAD_DIM), F32))
    o_ref[0] = (_rms(oi_s[...], nw_ref[...]) * _silu(zg_ref[0])).astype(BF16)


def hgrn2(z3, lb, norm_w):
    bsz, seq, _ = z3.shape
    zspec = lambda cb: pl.BlockSpec((1, seq, LANES), lambda b, h: (b, 0, cb + h))
    sel = _hgrn_selectors()
    return pl.pallas_call(
        functools.partial(_hgrn_body, seq=seq),
        grid=(bsz, N_HEADS),
        in_specs=[zspec(CD_Q), zspec(CD_F), zspec(CD_I), zspec(CD_G),
                  pl.BlockSpec((1, LANES), lambda b, h: (0, h)),
                  pl.BlockSpec((1, LANES), lambda b, h: (0, 0)),
                  pl.BlockSpec(sel.shape, lambda b, h: (0, 0))],
        out_specs=pl.BlockSpec((1, seq, LANES), lambda b, h: (b, 0, h)),
        out_shape=jax.ShapeDtypeStruct((bsz, seq, N_HEADS * HEAD_DIM), BF16),
        scratch_shapes=[pltpu.VMEM((seq, LANES), F32), pltpu.VMEM((seq // HGRN_CHUNK * 8, LANES), F32),
                        pltpu.VMEM((seq, LANES), BF16), pltpu.VMEM((seq, LANES), BF16)],
        compiler_params=_cparams(("parallel", "parallel")),
        name="hgrn2",
    )(z3, z3, z3, z3, lb.reshape(1, N_HEADS * HEAD_DIM), norm_w.reshape(1, HEAD_DIM), sel)


FOX_BLK = 256


def _fox_gate_body(sm_ref, bias_ref, cumt_ref, *, seq):
    c = FOX_BLK
    r = lax.broadcasted_iota(jnp.int32, (c, c), 0)
    cc = lax.broadcasted_iota(jnp.int32, (c, c), 1)
    ltri = (r >= cc).astype(F32)
    r1 = lax.broadcasted_iota(jnp.int32, (LANES, LANES), 0)
    c1 = lax.broadcasted_iota(jnp.int32, (LANES, LANES), 1)
    eye = (r1 == c1).astype(F32)

    def blk(n, carry):
        off = pl.multiple_of(n * c, c)
        ls = jax.nn.log_sigmoid(sm_ref[0, pl.ds(off, c), :] + bias_ref[...])
        cum = carry + jnp.dot(ltri, ls, precision=HI, preferred_element_type=F32)
        cumt_ref[0, :, pl.ds(off, c)] = lax.dot_general(eye, cum, NT, precision=HI, preferred_element_type=F32)
        return cum[c - 1:c, :]

    lax.fori_loop(0, seq // c, blk, jnp.zeros((1, LANES), F32))


def fox_gates(z3, bias_row):
    bsz, seq, _ = z3.shape
    return pl.pallas_call(
        functools.partial(_fox_gate_body, seq=seq),
        grid=(bsz,),
        in_specs=[pl.BlockSpec((1, seq, LANES), lambda b: (b, 0, CD_SMALL)),
                  pl.BlockSpec((1, LANES), lambda b: (0, 0))],
        out_specs=pl.BlockSpec((1, LANES, seq), lambda b: (b, 0, 0)),
        out_shape=jax.ShapeDtypeStruct((bsz, LANES, seq), F32),
        compiler_params=_cparams(("parallel",)),
        name="fox_gates",
    )(z3, bias_row)


ATT_ROWS = 128
ATT_TK = 256


def _for_blocks(n, unroll, body, carry=None):
    if isinstance(n, int):
        for j in range(n):
            carry = body(j, carry)
        return carry

    def it(i, c):
        for u in range(unroll):
            c = body(i * unroll + u, c)
        return c

    return lax.fori_loop(0, n // unroll, it, carry)


def _softmax_pv(s_scr, nblk, unroll, v_fn, m_scr, l_scr, acc_scr):
    _, rows, bw = s_scr.shape
    nlb = bw // LANES
    for ch in range(rows // ATT_ROWS):
        rs = slice(ch * ATT_ROWS, (ch + 1) * ATT_ROWS)

        def pmax(j, mel):
            for b in range(nlb):
                mel = jnp.maximum(mel, s_scr[j, rs, b * LANES:(b + 1) * LANES])
            return mel

        mel = _for_blocks(nblk, unroll, pmax, jnp.full((ATT_ROWS, LANES), NEG, F32))
        m_scr[rs, :] = jnp.broadcast_to(jnp.max(mel, -1, keepdims=True), (ATT_ROWS, LANES))
    l_scr[...] = jnp.zeros_like(l_scr)
    acc_scr[...] = jnp.zeros_like(acc_scr)

    def ppv(j, c):
        pcs = []
        for ch in range(rows // ATT_ROWS):
            rs = slice(ch * ATT_ROWS, (ch + 1) * ATT_ROWS)
            mb = m_scr[rs, :]
            lacc = l_scr[rs, :]
            ps = []
            for b in range(nlb):
                pb = jnp.exp(s_scr[j, rs, b * LANES:(b + 1) * LANES] - mb)
                lacc = lacc + pb
                ps.append(pb.astype(BF16))
            l_scr[rs, :] = lacc
            pcs.append(jnp.concatenate(ps, axis=1))
        acc_scr[...] += jnp.dot(jnp.concatenate(pcs, axis=0), v_fn(j), preferred_element_type=F32)
        return c

    _for_blocks(nblk, unroll, ppv)
    return acc_scr[...] / jnp.sum(l_scr[...], -1, keepdims=True)


def _softmax_scratch(rows, nblk, bw):
    return [pltpu.VMEM((nblk, rows, bw), F32), pltpu.VMEM((rows, LANES), F32),
            pltpu.VMEM((rows, LANES), F32), pltpu.VMEM((rows, HEAD_DIM), F32)]


def _fox_attn_body(q_ref, k_ref, v_ref, cumt_ref, o_ref, s_scr, m_scr, l_scr, acc_scr, *, tq):
    hd = pl.program_id(1)
    qi = pl.program_id(2)
    qall = (q_ref[0] * (HEAD_DIM ** -0.5)).astype(BF16)

    def scores(j):
        off = pl.multiple_of(j * tq, tq)
        kj = k_ref[0, pl.ds(off, tq), :].astype(BF16)
        ck = cumt_ref[0, pl.ds(hd % 8, 1), pl.ds(off, tq)]
        return lax.dot_general(qall, kj, NT, preferred_element_type=F32) - ck

    row = lax.broadcasted_iota(jnp.int32, (tq, tq), 0)
    col = lax.broadcasted_iota(jnp.int32, (tq, tq), 1)

    def v_blk(j):
        return v_ref[0, pl.ds(pl.multiple_of(j * tq, tq), tq), :].astype(BF16)

    def tile(nfull):
        for j in range(nfull):
            s_scr[j] = scores(j)
        s_scr[nfull] = jnp.where(col <= row, scores(nfull), NEG)
        o_ref[0] = _softmax_pv(s_scr, nfull + 1, 1, v_blk, m_scr, l_scr, acc_scr).astype(BF16)

    lax.switch(qi, [functools.partial(tile, n) for n in range(s_scr.shape[0])])


def fox_attention(z3, cumt):
    bsz, seq, _ = z3.shape
    tq = min(512, seq)
    full = lambda cb: pl.BlockSpec((1, seq, LANES), lambda b, h, i: (b, 0, cb + h))
    return pl.pallas_call(
        functools.partial(_fox_attn_body, tq=tq),
        grid=(bsz, N_HEADS, seq // tq),
        in_specs=[pl.BlockSpec((1, tq, LANES), lambda b, h, i: (b, i, CD_FQKV + h)),
                  full(CD_FQKV + 16), full(CD_FQKV + 32),
                  pl.BlockSpec((1, 8, seq), lambda b, h, i: (b, h // 8, 0))],
        out_specs=pl.BlockSpec((1, tq, LANES), lambda b, h, i: (b, i, h)),
        out_shape=jax.ShapeDtypeStruct((bsz, seq, N_HEADS * HEAD_DIM), BF16),
        scratch_shapes=_softmax_scratch(tq, seq // tq, tq),
        compiler_params=_cparams(("parallel", "parallel", "arbitrary")),
        name="fox_attention",
    )(z3, z3, z3, cumt)


def _nsa_cmp_body(zk_ref, zv_ref, pek_ref, pev_ref, wk1_ref, wk2_ref, wv1_ref, wv2_ref, kc_ref, vc_ref, *, nblk):
    def compress(z_ref, pe_ref, w1_ref, w2_ref):
        u1 = jnp.zeros((nblk, HEAD_DIM), F32)
        u2 = jnp.zeros((nblk, HEAD_DIM), F32)
        for l in range(S_CMP):
            zl = z_ref[0, :, l, :]
            u1 = u1 + jnp.dot(zl, w1_ref[l * HEAD_DIM:(l + 1) * HEAD_DIM, :], precision=HI,
                              preferred_element_type=F32)
            u2 = u2 + jnp.dot(zl, w1_ref[(S_CMP + l) * HEAD_DIM:(S_CMP + l + 1) * HEAD_DIM, :], precision=HI,
                              preferred_element_type=F32)
        pe = jnp.dot(jnp.broadcast_to(pe_ref[...], (8, L_CMP * HEAD_DIM)), w1_ref[...], precision=HI,
                     preferred_element_type=F32)[0:1]
        hmid = u1 + pltpu.roll(u2, nblk - 1, 0) + pe
        return jnp.dot(_silu(hmid), w2_ref[...], precision=HI, preferred_element_type=F32)

    kc_ref[0, 0] = compress(zk_ref, pek_ref, wk1_ref, wk2_ref)
    vc_ref[0, 0] = compress(zv_ref, pev_ref, wv1_ref, wv2_ref)


def nsa_compress(z3, pe_k, pe_v, wk1, wk2, wv1, wv2):
    bsz, seq, n = z3.shape
    nblk = seq // S_CMP
    z4 = z3.reshape(bsz, nblk, S_CMP, n)
    zspec = lambda cb: pl.BlockSpec((1, nblk, S_CMP, LANES), lambda b, g: (b, 0, 0, cb + g))
    wfull = lambda a: pl.BlockSpec(a.shape, lambda b, g: (0, 0))
    pek = pe_k.reshape(1, L_CMP * HEAD_DIM)
    pev = pe_v.reshape(1, L_CMP * HEAD_DIM)
    out = jax.ShapeDtypeStruct((bsz, G_NSA, nblk, HEAD_DIM), F32)
    ospec = pl.BlockSpec((1, 1, nblk, HEAD_DIM), lambda b, g: (b, g, 0, 0))
    return pl.pallas_call(
        functools.partial(_nsa_cmp_body, nblk=nblk),
        grid=(bsz, G_NSA),
        in_specs=[zspec(AB_NKV), zspec(AB_NKV + G_NSA), wfull(pek), wfull(pev),
                  wfull(wk1), wfull(wk2), wfull(wv1), wfull(wv2)],
        out_specs=[ospec, ospec],
        out_shape=[out, out],
        compiler_params=_cparams(("parallel", "parallel")),
        name="nsa_compress",
    )(z4, z4, pek, pev, wk1, wk2, wv1, wv2)


NSA_TQ = 128


def _nsa_expand(seq):
    m = np.arange(LANES)[:, None]
    t = np.arange(seq)[None, :]
    return jnp.asarray(((t // L_SLC) == m).astype(np.float32), dtype=BF16)


def _nsa_overlap(seq):
    cs = np.arange(seq // S_CMP)[:, None] * S_CMP
    ss = np.arange(LANES)[None, :] * L_SLC
    ov = (cs < ss + L_SLC) & (cs + L_CMP > ss) & (cs <= seq - L_CMP) & (ss < seq)
    return jnp.asarray(ov.astype(np.float32), dtype=BF16)


NSA_UNROLL = 2
NSA_WIN_BLOCKS = max(e // ATT_TK - max(e - (WINDOW + NSA_TQ - 2), 0) // ATT_TK + 1
                     for e in range(NSA_TQ - 1, 8 * WINDOW, NSA_TQ))


def _nsa_attn_body(q_ref, kc_ref, vc_ref, ks_ref, vs_ref, kw_ref, vw_ref, sm_ref, exp_ref, ovl_ref, o_ref,
                   s_scr, m_scr, l_scr, acc_scr, *, seq):
    g = pl.program_id(1)
    qi = pl.program_id(2)
    tq = NSA_TQ
    rows = HPG_NSA * tq
    nblk = seq // S_CMP
    n_slc = seq // L_SLC
    n_top = min(N_SEL, n_slc)
    q0 = qi * tq

    qall = q_ref[0]
    qs = jnp.concatenate([qall[:, p * HEAD_DIM:(p + 1) * HEAD_DIM] for p in range(HPG_NSA)], axis=0)
    qs = qs * (HEAD_DIM ** -0.5)
    qb = qs.astype(BF16)
    tpos = q0 + lax.broadcasted_iota(jnp.int32, (rows, 1), 0) % tq

    kc = kc_ref[0, 0]
    vc = vc_ref[0, 0]
    qh, ql = _split2(qs)
    kh, kl = _split2(kc)
    sc = lax.dot_general(jnp.concatenate([qh, qh, ql], axis=1), jnp.concatenate([kh, kl, kh], axis=1), NT,
                         preferred_element_type=F32)
    nidx = lax.broadcasted_iota(jnp.int32, (rows, nblk), 1)
    cmask = (nidx * S_CMP + (L_CMP - 1) <= tpos) & (nidx <= nblk - 2)
    scm = jnp.where(cmask, sc, NEG)
    mc = jnp.max(scm, -1, keepdims=True)
    ec = jnp.where(cmask, jnp.exp(scm - mc), 0.0)
    dc = jnp.sum(ec, -1, keepdims=True)
    p_cmp = ec / jnp.where(dc > 0, dc, 1.0)
    o_cmp = jnp.dot(p_cmp.astype(BF16), vc.astype(BF16), preferred_element_type=F32)

    psum = p_cmp[0:tq]
    for p in range(1, HPG_NSA):
        psum = psum + p_cmp[p * tq:(p + 1) * tq]
    ph, pl_ = _split2(psum)
    imp = jnp.dot(jnp.concatenate([ph, pl_], axis=1), jnp.concatenate([ovl_ref[...], ovl_ref[...]], axis=0),
                  preferred_element_type=F32)
    blk = lax.broadcasted_iota(jnp.int32, (tq, LANES), 1)
    cur = (q0 + lax.broadcasted_iota(jnp.int32, (tq, LANES), 0)) // L_SLC
    valid = blk <= cur
    forced = (blk == 0) | (blk == cur) | (blk == cur - 1)
    score = jnp.where(valid, jnp.where(forced, FORCE_SCORE, imp), NEG)
    rank = jnp.zeros((tq, LANES), jnp.int32)
    for mp in range(n_slc):
        colv = score[:, mp:mp + 1]
        rank = rank + ((colv > score) | ((colv == score) & (mp < blk))).astype(jnp.int32)
    selb = (valid & (rank < n_top)).astype(F32).astype(BF16)

    tk = ATT_TK
    qrow = q0 + lax.broadcasted_iota(jnp.int32, (tq, tk), 0)
    col = lax.broadcasted_iota(jnp.int32, (tq, tk), 1)

    def branch(k_ref, v_ref, first, nb, bias_fn):
        def scores(j, c):
            off = pl.multiple_of((first + j) * tk, tk)
            s = lax.dot_general(qb, k_ref[0, pl.ds(off, tk), :].astype(BF16), NT,
                                preferred_element_type=F32)
            bias = bias_fn(off)
            for p in range(HPG_NSA):
                s_scr[j, p * tq:(p + 1) * tq, :] = s[p * tq:(p + 1) * tq] + bias
            return c

        _for_blocks(nb, NSA_UNROLL, scores)

        def v_blk(j):
            return v_ref[0, pl.ds(pl.multiple_of((first + j) * tk, tk), tk), :].astype(BF16)

        return _softmax_pv(s_scr, nb, NSA_UNROLL, v_blk, m_scr, l_scr, acc_scr)

    def slc_bias(off):
        sel = jnp.dot(selb, exp_ref[:, pl.ds(off, tk)], preferred_element_type=F32)
        return jnp.where((sel > 0.5) & (off + col <= qrow), 0.0, NEG)

    last = (q0 + tq - 1) // tk
    o_slc = branch(ks_ref, vs_ref, 0, (last // NSA_UNROLL + 1) * NSA_UNROLL, slc_bias)

    def win_bias(off):
        d = qrow - (off + col)
        return jnp.where((d >= 0) & (d < WINDOW), 0.0, NEG)

    nwin = min(NSA_WIN_BLOCKS, seq // tk)
    o_win = branch(kw_ref, vw_ref, jnp.maximum(last - (nwin - 1), 0), nwin, win_bias)

    sg = jax.nn.sigmoid(sm_ref[0])
    for p in range(HPG_NSA):
        sl = slice(p * tq, (p + 1) * tq)
        base = 2 * N_HEADS + g * HPG_NSA + p
        o = (_lane_col(sg, blk, base) * o_cmp[sl] + _lane_col(sg, blk, base + N_HEADS) * o_slc[sl]
             + _lane_col(sg, blk, base + 2 * N_HEADS) * o_win[sl])
        o_ref[0, :, p * HEAD_DIM:(p + 1) * HEAD_DIM] = o.astype(BF16)


def nsa_attention(z3, kc, vc):
    bsz, seq, _ = z3.shape
    tq = NSA_TQ
    assert seq % tq == 0 and (seq // ATT_TK) % NSA_UNROLL == 0 and seq // L_SLC <= LANES
    nblk = seq // S_CMP
    full = lambda cb: pl.BlockSpec((1, seq, LANES), lambda b, g, i: (b, 0, cb + g))
    cspec = pl.BlockSpec((1, 1, nblk, HEAD_DIM), lambda b, g, i: (b, g, 0, 0))
    qw = HPG_NSA * HEAD_DIM
    return pl.pallas_call(
        functools.partial(_nsa_attn_body, seq=seq),
        grid=(bsz, G_NSA, seq // tq),
        in_specs=[pl.BlockSpec((1, tq, qw), lambda b, g, i: (b, i, AB_NQ * LANES // qw + g)),
                  cspec, cspec,
                  full(AB_NKV + 2 * G_NSA), full(AB_NKV + 3 * G_NSA),
                  full(AB_NKV + 4 * G_NSA), full(AB_NKV + 5 * G_NSA),
                  pl.BlockSpec((1, tq, LANES), lambda b, g, i: (b, i, AB_SMALL)),
                  pl.BlockSpec((LANES, seq), lambda b, g, i: (0, 0)),
                  pl.BlockSpec((nblk, LANES), lambda b, g, i: (0, 0))],
        out_specs=pl.BlockSpec((1, tq, qw), lambda b, g, i: (b, i, g)),
        out_shape=jax.ShapeDtypeStruct((bsz, seq, N_HEADS * HEAD_DIM), BF16),
        scratch_shapes=_softmax_scratch(HPG_NSA * tq, seq // ATT_TK, ATT_TK),
        compiler_params=_cparams(("parallel", "parallel", "arbitrary")),
        name="nsa_attention",
    )(z3, kc, vc, z3, z3, z3, z3, z3, _nsa_expand(seq), _nsa_overlap(seq))


def _repack_body(w_ref, o_ref, *, sections, pad_from):
    o_ref[pad_from:, :] = jnp.zeros((o_ref.shape[0] - pad_from, o_ref.shape[1]), BF16)
    for dst, src, width in sections:
        o_ref[dst:dst + width, :] = w_ref[src:src + width, :].astype(BF16)


def _repack_weight(w, sections, n_out, pad_from, tc=256):
    wt = jnp.swapaxes(w, 0, 1)
    n_in, k = wt.shape
    return pl.pallas_call(
        functools.partial(_repack_body, sections=sections, pad_from=pad_from),
        grid=(k // tc,),
        in_specs=[pl.BlockSpec((n_in, tc), lambda i: (0, i))],
        out_specs=pl.BlockSpec((n_out, tc), lambda i: (0, i)),
        out_shape=jax.ShapeDtypeStruct((n_out, k), BF16),
        compiler_params=_cparams(("parallel",)),
        name="repack_weight",
    )(wt)


def _ab_weight(w):
    small = AB_SMALL * LANES
    sections = ((0, 0, 6144), (6144, 6176, 2048), (8192, 8224, 2048), (10240, 10272, 3072),
                (small, 6144, 32), (small + 32, 13344, 48))
    return _repack_weight(w, sections, AB_N, small)


def _cd_weight(w):
    small = CD_SMALL * LANES
    return _repack_weight(w, ((0, 0, small), (small, small, 16)), CD_N, small)


def _row128(v):
    return jnp.pad(v.astype(F32), (0, LANES - v.shape[0])).reshape(1, LANES)


def kernel(x, p, ab_norm_pre, ab_norm_post, ab_w_in, gdn_conv_w, gdn_a_log, gdn_dt_bias, gdn_norm, nsa_pe_k, nsa_pe_v, nsa_cmp_k1, nsa_cmp_k2, nsa_cmp_v1, nsa_cmp_v2, ab_w_out, cd_norm_pre, cd_norm_post, cd_w_in, hgrn_lb_logits, hgrn_norm, fox_f_bias, cd_w_out, ffn_norm_pre, ffn_norm_post, ffn_w_up, ffn_conv_w, ffn_conv_b, ffn_w_down, ple_w_proj, ple_gate_norm, ple_w_gate, ple_norm_post):
    bsz, seq, dm = x.shape
    depth = p.shape[0]
    m = bsz * seq
    half = N_HEADS * HEAD_DIM
    sm_ = jax.nn.softmax(hgrn_lb_logits.astype(F32), axis=0)
    lb_table = jnp.cumsum(sm_, axis=0) - sm_[0]
    xf = x.reshape(m, dm)
    pf = p.reshape(depth, m, -1)
    w_up, w_down = ffn_w_up.astype(BF16), ffn_w_down.astype(BF16)
    w_gate, w_proj = ple_w_gate.astype(BF16), ple_w_proj.astype(BF16)
    for li in range(depth):
        j = li // 2
        if li % 2 == 0:
            z3 = norm_matmul(xf, ab_norm_pre[j], _ab_weight(ab_w_in[j])).reshape(bsz, seq, AB_N)
            prm = jnp.concatenate([_row128(gdn_a_log[j]), _row128(gdn_dt_bias[j]),
                                   jnp.zeros((6, LANES), F32)], axis=0)
            o_a = gated_deltanet(z3, gdn_conv_w[j], prm, gdn_norm[j])
            kc, vc = nsa_compress(z3, nsa_pe_k[j], nsa_pe_v[j], nsa_cmp_k1[j], nsa_cmp_k2[j],
                                  nsa_cmp_v1[j], nsa_cmp_v2[j])
            o_b = nsa_attention(z3, kc, vc)
            w_out, post = ab_w_out[j], ab_norm_post[j]
        else:
            z3 = norm_matmul(xf, cd_norm_pre[j], _cd_weight(cd_w_in[j])).reshape(bsz, seq, CD_N)
            o_a = hgrn2(z3, lb_table[li], hgrn_norm[j])
            o_b = fox_attention(z3, fox_gates(z3, _row128(fox_f_bias[j])))
            w_out, post = cd_w_out[j], cd_norm_post[j]
        xf = outproj(o_a.reshape(m, half), o_b.reshape(m, half), w_out.astype(BF16), xf, post)
        xf = conv_ffn(xf, li, ffn_norm_pre, w_up, ffn_conv_w, ffn_conv_b, w_down, ffn_norm_post, seq)
        xf = ple(xf, li, pf, ple_gate_norm, w_gate, w_proj, ple_norm_post)
    return xf.reshape(bsz, seq, dm)
```

```python
import functools

import numpy as np
import jax
import jax.numpy as jnp
from jax import lax
from jax.experimental import pallas as pl
from jax.experimental.pallas import tpu as pltpu

F32 = jnp.float32
BF16 = jnp.bfloat16
HI = lax.Precision.HIGHEST
NT = (((1,), (1,)), ((), ()))
TN = (((0,), (0,)), ((), ()))

HEAD_DIM = 128
N_HEADS = 16
G_NSA = 4
HPG_NSA = 4
L_CMP = 32
S_CMP = 16
L_SLC = 64
N_SEL = 8
WINDOW = 512
FORCE_SCORE = 1e4
GDN_CHUNK = 64
GDN_GROUP = 256
GDN_LOCKSTEP = 8
GDN_HEADS = 2
HGRN_CHUNK = 256
HGRN_LOCKSTEP = 2
EPS = 1e-6
NEG = -1e30
LANES = 128

AB_QKV, AB_GATE, AB_NQ, AB_NKV, AB_SMALL = 0, 48, 64, 80, 104
AB_N = 108 * LANES
CD_Q, CD_F, CD_I, CD_G, CD_FQKV, CD_SMALL = 0, 16, 32, 48, 64, 112
CD_N = 116 * LANES

VMEM_LIMIT = 56 * 1024 * 1024


def _cparams(sem):
    return pltpu.CompilerParams(dimension_semantics=sem, vmem_limit_bytes=VMEM_LIMIT)


def _rms(x, w):
    return x * lax.rsqrt(jnp.mean(x * x, axis=-1, keepdims=True) + EPS) * w


def _silu(x):
    return x * jax.nn.sigmoid(x)


def _lane_col(x, lane_idx, lane):
    return jnp.sum(jnp.where(lane_idx == lane, x, 0.0), axis=-1, keepdims=True)


def _split2(x):
    hi = x.astype(BF16)
    return hi, (x - hi.astype(F32)).astype(BF16)


def _dot_sel(sel, x):
    n = x.shape[1]
    hi, lo = _split2(x)
    y = jnp.dot(sel, jnp.concatenate([hi, lo], axis=1), preferred_element_type=F32)
    return y[:, :n] + y[:, n:]


def _norm_matmul_body(x_ref, nw_ref, w_ref, o_ref, h_ref):
    @pl.when(pl.program_id(1) == 0)
    def _():
        h_ref[...] = _rms(x_ref[...], nw_ref[...]).astype(BF16)

    o_ref[...] = lax.dot_general(h_ref[...], w_ref[...], NT, preferred_element_type=F32)


def norm_matmul(x, nw, wt, tm=512, tn=512):
    m, k = x.shape
    n = wt.shape[0]
    return pl.pallas_call(
        _norm_matmul_body,
        grid=(m // tm, n // tn),
        in_specs=[pl.BlockSpec((tm, k), lambda i, j: (i, 0)),
                  pl.BlockSpec((1, k), lambda i, j: (0, 0)),
                  pl.BlockSpec((tn, k), lambda i, j: (j, 0))],
        out_specs=pl.BlockSpec((tm, tn), lambda i, j: (i, j)),
        out_shape=jax.ShapeDtypeStruct((m, n), F32),
        scratch_shapes=[pltpu.VMEM((tm, k), BF16)],
        compiler_params=_cparams(("parallel", "arbitrary")),
        name="norm_matmul",
    )(x, nw.reshape(1, k), wt)


def _outproj_body(oa_ref, ob_ref, wa_ref, wb_ref, x_ref, nw_ref, o_ref, *, tn):
    j = pl.program_id(1)
    y = jnp.dot(oa_ref[...], wa_ref[...], preferred_element_type=F32)
    y = y + jnp.dot(ob_ref[...], wb_ref[...], preferred_element_type=F32)
    o_ref[:, pl.ds(pl.multiple_of(j * tn, tn), tn)] = y

    @pl.when(j == pl.num_programs(1) - 1)
    def _():
        o_ref[...] = x_ref[...] + _rms(o_ref[...], nw_ref[...])


def outproj(oa, ob, w, x, nw, tm=512, tn=512):
    m, ka = oa.shape
    n = w.shape[1]
    return pl.pallas_call(
        functools.partial(_outproj_body, tn=tn),
        grid=(m // tm, n // tn),
        in_specs=[pl.BlockSpec((tm, ka), lambda i, j: (i, 0)),
                  pl.BlockSpec((tm, ka), lambda i, j: (i, 0)),
                  pl.BlockSpec((ka, tn), lambda i, j: (0, j)),
                  pl.BlockSpec((ka, tn), lambda i, j: (1, j)),
                  pl.BlockSpec((tm, n), lambda i, j: (i, 0), pipeline_mode=pl.Buffered(1)),
                  pl.BlockSpec((1, n), lambda i, j: (0, 0))],
        out_specs=pl.BlockSpec((tm, n), lambda i, j: (i, 0)),
        out_shape=jax.ShapeDtypeStruct((m, n), F32),
        compiler_params=_cparams(("parallel", "arbitrary")),
        name="outproj",
    )(oa, ob, w, w, x, nw.reshape(1, n))


FFN_HALO = 16


def _ffn_body(x_ref, xh_ref, nw_ref, wg_ref, wu_ref, cwg_ref, cwu_ref, cbg_ref, cbu_ref,
              wd_ref, pw_ref, o_ref, h_s, *, tm, seq):
    i = pl.program_id(0)
    j = pl.program_id(1)

    @pl.when(j == 0)
    def _():
        keep = jnp.where((i * tm) % seq == 0, 0.0, 1.0)
        h_s[0:FFN_HALO, :] = (_rms(xh_ref[...], nw_ref[...]) * keep).astype(BF16)
        h_s[FFN_HALO:, :] = _rms(x_ref[...], nw_ref[...]).astype(BF16)
        o_ref[...] = jnp.zeros_like(o_ref)

    h = h_s[...]

    def branch(w_ref, cw_ref, cb_ref):
        u = jnp.dot(h, w_ref[...], preferred_element_type=F32)
        cw = cw_ref[...]
        y = u * cw[2:3] + pltpu.roll(u, 1, 0) * cw[1:2] + pltpu.roll(u, 2, 0) * cw[0:1]
        return y[FFN_HALO:] + cb_ref[...]

    g = branch(wg_ref, cwg_ref, cbg_ref)
    u = branch(wu_ref, cwu_ref, cbu_ref)
    act = (_silu(g) * u).astype(BF16)
    o_ref[...] += jnp.dot(act, wd_ref[...], preferred_element_type=F32)

    @pl.when(j == pl.num_programs(1) - 1)
    def _():
        o_ref[...] = x_ref[...] + _rms(o_ref[...], pw_ref[...])


def conv_ffn(x, li, nw, w_up, conv_w, conv_b, w_down, pw, seq, tm=512, tf=256):
    m, k = x.shape
    nl, f, _ = w_down.shape
    nf = f // tf
    hb = tm // FFN_HALO
    cb = conv_b.reshape(nl, 1, 2 * f)
    row = pl.BlockSpec((None, 1, k), lambda i, j: (li, 0, 0))
    return pl.pallas_call(
        functools.partial(_ffn_body, tm=tm, seq=seq),
        grid=(m // tm, nf),
        in_specs=[pl.BlockSpec((tm, k), lambda i, j: (i, 0), pipeline_mode=pl.Buffered(1)),
                  pl.BlockSpec((FFN_HALO, k), lambda i, j: (jnp.maximum(i * hb - 1, 0), 0)),
                  row,
                  pl.BlockSpec((None, k, tf), lambda i, j: (li, 0, j)),
                  pl.BlockSpec((None, k, tf), lambda i, j: (li, 0, nf + j)),
                  pl.BlockSpec((None, 3, tf), lambda i, j: (li, 0, j)),
                  pl.BlockSpec((None, 3, tf), lambda i, j: (li, 0, nf + j)),
                  pl.BlockSpec((None, 1, tf), lambda i, j: (li, 0, j)),
                  pl.BlockSpec((None, 1, tf), lambda i, j: (li, 0, nf + j)),
                  pl.BlockSpec((None, tf, k), lambda i, j: (li, j, 0)),
                  row],
        out_specs=pl.BlockSpec((tm, k), lambda i, j: (i, 0)),
        out_shape=jax.ShapeDtypeStruct((m, k), F32),
        scratch_shapes=[pltpu.VMEM((tm + FFN_HALO, k), BF16)],
        compiler_params=_cparams(("parallel", "arbitrary")),
        name="conv_ffn",
    )(x, x, nw.reshape(nl, 1, k), w_up, w_up, conv_w, conv_w, cb, cb, w_down, pw.reshape(nl, 1, k))


def _ple_body(x_ref, p_ref, gnw_ref, wg_ref, wp_ref, pnw_ref, o_ref, h_s, *, tn):
    j = pl.program_id(1)

    @pl.when(j == 0)
    def _():
        h_s[...] = _rms(x_ref[...], gnw_ref[...]).astype(BF16)

    gate = jax.nn.sigmoid(jnp.dot(h_s[...], wg_ref[...], preferred_element_type=F32))
    proj = jnp.dot(p_ref[...].astype(BF16), wp_ref[...], preferred_element_type=F32)
    o_ref[:, pl.ds(pl.multiple_of(j * tn, tn), tn)] = gate * proj

    @pl.when(j == pl.num_programs(1) - 1)
    def _():
        o_ref[...] = x_ref[...] + _rms(o_ref[...], pnw_ref[...])


def ple(x, li, p, gnw, wg, wp, pnw, tm=512, tn=512):
    m, k = x.shape
    nl, _, dp = p.shape
    row = pl.BlockSpec((None, 1, k), lambda i, j: (li, 0, 0))
    return pl.pallas_call(
        functools.partial(_ple_body, tn=tn),
        grid=(m // tm, k // tn),
        in_specs=[pl.BlockSpec((tm, k), lambda i, j: (i, 0), pipeline_mode=pl.Buffered(1)),
                  pl.BlockSpec((None, tm, dp), lambda i, j: (li, i, 0)),
                  row,
                  pl.BlockSpec((None, k, tn), lambda i, j: (li, 0, j)),
                  pl.BlockSpec((None, dp, tn), lambda i, j: (li, 0, j)),
                  row],
        out_specs=pl.BlockSpec((tm, k), lambda i, j: (i, 0)),
        out_shape=jax.ShapeDtypeStruct((m, k), F32),
        scratch_shapes=[pltpu.VMEM((tm, k), BF16)],
        compiler_params=_cparams(("parallel", "arbitrary")),
        name="ple",
    )(x, p, gnw.reshape(nl, 1, k), wg, wp, pnw.reshape(nl, 1, k))


def _gdn_ltri():
    r = np.arange(GDN_GROUP)[:, None]
    t = np.arange(GDN_GROUP)[None, :]
    return jnp.asarray((((r // GDN_CHUNK) == (t // GDN_CHUNK)) & (r >= t)).astype(np.float32), dtype=BF16)


def _gdn_head_setup(hd, lanes, zq_ref, zk_ref, zv_ref, sm_ref, cwq_ref, cwk_ref, cwv_ref, prm_ref, ltri_ref,
                    q_s, k_s, v_s, g_s, b_s, o0_s, n0_s, gl_s, qe_s, m_s, seq):
    c = GDN_CHUNK
    gs = GDN_GROUP
    row = lax.broadcasted_iota(jnp.int32, (seq, LANES), 0)
    lane = lax.broadcasted_iota(jnp.int32, (seq, LANES), 1)

    def conv_silu(z_ref, w_ref):
        z = z_ref[0, :, lanes]
        w = w_ref[:, lanes]
        y = z * w[3:4]
        for s in (1, 2, 3):
            y = y + jnp.where(row >= s, pltpu.roll(z, s, 0), 0.0) * w[3 - s:4 - s]
        return _silu(y)

    q = conv_silu(zq_ref, cwq_ref)
    k = conv_silu(zk_ref, cwk_ref)
    q_s[...] = q * lax.rsqrt(jnp.sum(q * q, -1, keepdims=True) + EPS) * (HEAD_DIM ** -0.5)
    k_s[...] = k * lax.rsqrt(jnp.sum(k * k, -1, keepdims=True) + EPS)
    v_s[...] = conv_silu(zv_ref, cwv_ref)

    sm = sm_ref[0]
    glog = -jnp.exp(prm_ref[0:1, :]) * jax.nn.softplus(sm + prm_ref[1:2, :])
    g_s[...] = jnp.broadcast_to(_lane_col(glog, lane, hd), (seq, LANES))
    b_s[...] = jnp.broadcast_to(_lane_col(jax.nn.sigmoid(sm), lane, N_HEADS + hd), (seq, LANES))

    nper = gs // c
    rs = lax.broadcasted_iota(jnp.int32, (c, gs), 0)
    cs = lax.broadcasted_iota(jnp.int32, (c, gs), 1)
    jj = cs % c
    cblk = cs // c
    incl = rs >= jj
    strict = rs > jj
    eye_side = (rs == jj).astype(F32)
    r2 = lax.broadcasted_iota(jnp.int32, (gs, gs), 0)
    c2 = lax.broadcasted_iota(jnp.int32, (gs, gs), 1)
    bdm = ((r2 // c) == (c2 // c)).astype(F32)
    bdm_b = bdm.astype(BF16)
    lane_c = lax.broadcasted_iota(jnp.int32, (c, LANES), 1)
    lane_g = lax.broadcasted_iota(jnp.int32, (gs, LANES), 1)
    ones3 = (lane_c < 3).astype(F32).astype(BF16)

    def side_col(xb):
        left = jnp.where(lane_c < c, xb[0:c], xb[c:2 * c])
        right = jnp.where(lane_c < c, xb[2 * c:3 * c], xb[3 * c:4 * c])
        return jnp.concatenate([left, right], axis=1)

    def to_side(full):
        out = full[0:c]
        for ci in range(1, nper):
            out = jnp.where(cblk == ci, full[ci * c:(ci + 1) * c], out)
        return out

    def bd(xs, mask):
        return jnp.concatenate([xs] * nper, axis=0) * mask

    def mm3(ph, pl_, xh, xl):
        bh = bd(xh, bdm_b)
        lhs = jnp.concatenate([ph, ph, pl_], axis=1)
        rhs = jnp.concatenate([bh, bd(xl, bdm_b), bh], axis=0)
        return jnp.dot(lhs, rhs, preferred_element_type=F32)

    def group_pre(gi):
        off = pl.multiple_of(gi * gs, gs)
        qg = q_s[pl.ds(off, gs), :]
        kg = k_s[pl.ds(off, gs), :]
        bg = b_s[pl.ds(off, gs), :]
        gam = _dot_sel(ltri_ref[...], g_s[pl.ds(off, gs), :])
        g_hi = gam.astype(BF16).astype(F32)
        g_mid = (gam - g_hi).astype(BF16).astype(F32)
        g_lo = gam - g_hi - g_mid
        gam3 = jnp.where(lane_g == 0, g_hi, jnp.where(lane_g == 1, g_mid, jnp.where(lane_g == 2, g_lo, 0.0)))
        gam_row = lax.dot_general(ones3, gam3.astype(BF16), NT, preferred_element_type=F32)
        dec = jnp.where(incl, jnp.exp(jnp.where(incl, side_col(gam) - gam_row, 0.0)), 0.0)
        qb = qg.astype(BF16)
        kb = kg.astype(BF16)
        kk = to_side(lax.dot_general(kb, kb, NT, preferred_element_type=F32))
        a = jnp.where(strict, side_col(bg) * dec * kk, 0.0)
        return dict(off=off, qg=qg, kg=kg, bg=bg, gam=gam, dec=dec, qb=qb, kb=kb, a=a)

    def group_post(gi, st, tinv):
        off, qg, kg, bg, gam, dec, qb, kb = (st[n] for n in ("off", "qg", "kg", "bg", "gam", "dec", "qb", "kb"))
        vg = v_s[pl.ds(off, gs), :]
        th, tl = _split2(tinv)
        eg = jnp.exp(gam)
        rh, rl = _split2(jnp.concatenate([vg * bg, kg * (bg * eg)], axis=1))
        bth = bd(th, bdm_b)
        sol = jnp.dot(jnp.concatenate([bth, bth, bd(tl, bdm_b)], axis=1),
                      jnp.concatenate([rh, rl, rh], axis=0), preferred_element_type=F32)
        solb = sol.astype(BF16)
        qk = bd(dec, bdm) * lax.dot_general(qb, kb, NT, preferred_element_type=F32)
        qkuw = jnp.dot(qk.astype(BF16), solb, preferred_element_type=F32)
        o0_s[pl.ds(off, gs), :] = qkuw[:, :HEAD_DIM]
        qe_s[pl.ds(off, gs), :] = (qg * eg - qkuw[:, HEAD_DIM:]).astype(BF16)
        for ci in range(nper):
            lo = ci * c
            gl = gam[lo + c - 1:lo + c, :]
            kd = (kg[lo:lo + c] * jnp.exp(gl - gam[lo:lo + c])).astype(BF16)
            kds = lax.dot_general(kd, solb[lo:lo + c], TN, preferred_element_type=F32)
            so = pl.multiple_of((gi * nper + ci) * HEAD_DIM, HEAD_DIM)
            n0_s[pl.ds(so, HEAD_DIM), :] = kds[:, :HEAD_DIM]
            m_s[pl.ds(so, HEAD_DIM), :] = (-kds[:, HEAD_DIM:]).astype(BF16)
            gl_s[pl.ds(pl.multiple_of((gi * nper + ci) * 8, 8), 8), :] = jnp.broadcast_to(jnp.exp(gl), (8, LANES))

    ngroups = seq // gs
    nlock = GDN_LOCKSTEP if ngroups % GDN_LOCKSTEP == 0 else 1

    def groups(it, carry):
        gis = [it + u * (ngroups // nlock) for u in range(nlock)]
        sts = [group_pre(gi) for gi in gis]
        tinvs = [eye_side - st["a"] for st in sts]
        xs = [_split2(st["a"]) for st in sts]
        for _ in range(5):
            xs = [_split2(mm3(xh, xl, xh, xl)) for xh, xl in xs]
            ts = [_split2(t) for t in tinvs]
            tinvs = [t + mm3(th, tl, xh, xl) for t, (th, tl), (xh, xl) in zip(tinvs, ts, xs)]
        for gi, st, t in zip(gis, sts, tinvs):
            group_post(gi, st, t)
        return carry

    lax.fori_loop(0, ngroups // nlock, groups, 0)


def _gdn_body(zq_ref, zk_ref, zv_ref, zg_ref, sm_ref, cwq_ref, cwk_ref, cwv_ref, prm_ref, nw_ref, ltri_ref,
              o_ref, *scratch, seq):
    c = GDN_CHUNK
    nh = GDN_HEADS
    o0_s, n0_s, gl_s, qe_s, m_s = scratch[5:]
    for hh in range(nh):
        _gdn_head_setup(pl.program_id(1) * nh + hh, slice(hh * LANES, (hh + 1) * LANES), zq_ref, zk_ref, zv_ref,
                        sm_ref, cwq_ref, cwk_ref, cwv_ref, prm_ref, ltri_ref, *(r.at[hh] for r in scratch), seq)

    def step(n, states):
        off = pl.multiple_of(n * c, c)
        so = pl.multiple_of(n * HEAD_DIM, HEAD_DIM)
        sbs = [s.astype(BF16) for s in states]
        for hh in range(nh):
            o0_s[hh, pl.ds(off, c), :] = o0_s[hh, pl.ds(off, c), :] + jnp.dot(
                qe_s[hh, pl.ds(off, c), :], sbs[hh], preferred_element_type=F32)
        ms = [jnp.dot(m_s[hh, pl.ds(so, HEAD_DIM), :], sbs[hh], preferred_element_type=F32) for hh in range(nh)]
        return tuple(gl_s[hh, pl.ds(pl.multiple_of(n * 8, 8), 1), :] * states[hh] + n0_s[hh, pl.ds(so, HEAD_DIM), :]
                     + ms[hh] for hh in range(nh))

    lax.fori_loop(0, seq // c, step, tuple(jnp.zeros((HEAD_DIM, HEAD_DIM), F32) for _ in range(nh)))
    for hh in range(nh):
        lanes = slice(hh * LANES, (hh + 1) * LANES)
        o_ref[0, :, lanes] = (_rms(o0_s[hh], nw_ref[...]) * _silu(zg_ref[0, :, lanes])).astype(BF16)


def gated_deltanet(z3, conv_w, prm, norm_w):
    bsz, seq, _ = z3.shape
    nh = GDN_HEADS
    zspec = lambda cb: pl.BlockSpec((1, seq, nh * LANES), lambda b, h: (b, 0, cb // nh + h))
    wspec = lambda cb: pl.BlockSpec((4, nh * LANES), lambda b, h: (0, cb // nh + h))
    nchunk = seq // GDN_CHUNK
    return pl.pallas_call(
        functools.partial(_gdn_body, seq=seq),
        grid=(bsz, N_HEADS // nh),
        in_specs=[zspec(AB_QKV), zspec(AB_QKV + 16), zspec(AB_QKV + 32), zspec(AB_GATE),
                  pl.BlockSpec((1, seq, LANES), lambda b, h: (b, 0, AB_SMALL)),
                  wspec(0), wspec(16), wspec(32),
                  pl.BlockSpec((8, LANES), lambda b, h: (0, 0)),
                  pl.BlockSpec((1, LANES), lambda b, h: (0, 0)),
                  pl.BlockSpec((GDN_GROUP, GDN_GROUP), lambda b, h: (0, 0))],
        out_specs=pl.BlockSpec((1, seq, nh * LANES), lambda b, h: (b, 0, h)),
        out_shape=jax.ShapeDtypeStruct((bsz, seq, N_HEADS * HEAD_DIM), BF16),
        scratch_shapes=[pltpu.VMEM((nh, seq, LANES), F32) for _ in range(6)]
        + [pltpu.VMEM((nh, nchunk * HEAD_DIM, HEAD_DIM), F32),
           pltpu.VMEM((nh, nchunk * 8, LANES), F32),
           pltpu.VMEM((nh, seq, LANES), BF16),
           pltpu.VMEM((nh, nchunk * HEAD_DIM, HEAD_DIM), BF16)],
        compiler_params=_cparams(("parallel", "parallel")),
        name="gated_deltanet",
    )(z3, z3, z3, z3, z3, conv_w, conv_w, conv_w, prm, norm_w.reshape(1, HEAD_DIM), _gdn_ltri())


HGRN_LEVELS = tuple(HGRN_CHUNK >> (i + 1) for i in range(8))


def _hgrn_selectors():
    c = HGRN_CHUNK
    r = np.arange(c)[:, None]
    t = np.arange(c)[None, :]
    mats = [r >= t]
    for s in HGRN_LEVELS:
        isq = ((r // s) % 2) == 1
        mats.append(((r // s) == (t // s)) & ((isq & (t <= r)) | (~isq & (t > r))))
    return jnp.asarray(np.concatenate(mats, axis=0).astype(np.float32), dtype=BF16)


def _hgrn_body(zq_ref, zf_ref, zi_ref, zg_ref, lb_ref, nw_ref, sel_ref, o_ref, oi_s, ebl_s, qd_s, kd_s, *, seq):
    c = HGRN_CHUNK
    r = lax.broadcasted_iota(jnp.int32, (c, c), 0)
    cc = lax.broadcasted_iota(jnp.int32, (c, c), 1)
    diag = r == cc
    x = r ^ cc
    lev = jnp.zeros((c, c), jnp.int32)
    for bit in range(1, 8):
        lev = lev + (x >= (1 << bit)).astype(jnp.int32)
    lev = jnp.where(r > cc, lev, -1)
    rr = lax.broadcasted_iota(jnp.int32, (c, LANES), 0)
    lb = lb_ref[...]
    log_lb = jnp.log(lb)
    log_1m_lb = jnp.log1p(-lb)

    nchunk = seq // c
    nlock = HGRN_LOCKSTEP if nchunk % HGRN_LOCKSTEP == 0 else 1

    def intra(it, carry):
        ns = [it + u * (nchunk // nlock) for u in range(nlock)]
        offs = [pl.multiple_of(n * c, c) for n in ns]
        fxs = [zf_ref[0, pl.ds(off, c), :] for off in offs]
        qs_ = [_silu(zq_ref[0, pl.ds(off, c), :]) for off in offs]
        lfs, ks_ = [], []
        for fx in fxs:
            t = jnp.exp(-jnp.abs(fx))
            gate = log_1m_lb + (jnp.minimum(fx, 0.0) - jnp.log(1.0 + t))
            lfs.append(jnp.maximum(log_lb, gate) + jnp.log(1.0 + jnp.exp(-jnp.abs(log_lb - gate))))
            ks_.append((1.0 - lb) * (jnp.where(fx >= 0.0, t, 1.0) / (1.0 + t)))
        sums = [_dot_sel(sel_ref[...], lf) for lf in lfs]
        atts = [jnp.where(diag, jnp.sum(q * k, -1, keepdims=True), 0.0) for q, k in zip(qs_, ks_)]
        for i, s in enumerate(HGRN_LEVELS):
            isq = ((rr // s) % 2) == 1
            parts = []
            for q, k, sm in zip(qs_, ks_, sums):
                e = jnp.exp(sm[(i + 1) * c:(i + 2) * c])
                parts.append(lax.dot_general(jnp.where(isq, q * e, 0.0).astype(BF16),
                                             jnp.where(isq, 0.0, k * e).astype(BF16), NT,
                                             preferred_element_type=F32))
            atts = [jnp.where(lev == (7 - i), part, att) for part, att in zip(parts, atts)]
        for n, off, q, k, sm, att in zip(ns, offs, qs_, ks_, sums, atts):
            b = sm[0:c]
            bl = b[c - 1:c, :]
            vb = zi_ref[0, pl.ds(off, c), :].astype(BF16)
            oi_s[pl.ds(off, c), :] = jnp.dot(att.astype(BF16), vb, preferred_element_type=F32)
            qd_s[pl.ds(off, c), :] = (q * jnp.exp(b)).astype(BF16)
            kd_s[pl.ds(off, c), :] = (k * jnp.exp(bl - b)).astype(BF16)
            ebl_s[pl.ds(pl.multiple_of(n * 8, 8), 8), :] = jnp.broadcast_to(jnp.exp(bl), (8, LANES))
        return carry

    lax.fori_loop(0, nchunk // nlock, intra, 0)

    def scan(n, st):
        off = pl.multiple_of(n * c, c)
        oi_s[pl.ds(off, c), :] = oi_s[pl.ds(off, c), :] + lax.dot_general(
            qd_s[pl.ds(off, c), :], st.astype(BF16), NT, preferred_element_type=F32)
        vb = zi_ref[0, pl.ds(off, c), :].astype(BF16)
        return (st * ebl_s[pl.ds(pl.multiple_of(n * 8, 8), 1), :]
                + lax.dot_general(vb, kd_s[pl.ds(off, c), :], TN, preferred_element_type=F32))

    lax.fori_loop(0, nchunk, scan, jnp.zeros((HEAD_DIM, HEAD_DIM), F32))
    o_ref[0] = (_rms(oi_s[...], nw_ref[...]) * _silu(zg_ref[0])).astype(BF16)


def hgrn2(z3, lb, norm_w):
    bsz, seq, _ = z3.shape
    zspec = lambda cb: pl.BlockSpec((1, seq, LANES), lambda b, h: (b, 0, cb + h))
    sel = _hgrn_selectors()
    return pl.pallas_call(
        functools.partial(_hgrn_body, seq=seq),
        grid=(bsz, N_HEADS),
        in_specs=[zspec(CD_Q), zspec(CD_F), zspec(CD_I), zspec(CD_G),
                  pl.BlockSpec((1, LANES), lambda b, h: (0, h)),
                  pl.BlockSpec((1, LANES), lambda b, h: (0, 0)),
                  pl.BlockSpec(sel.shape, lambda b, h: (0, 0))],
        out_specs=pl.BlockSpec((1, seq, LANES), lambda b, h: (b, 0, h)),
        out_shape=jax.ShapeDtypeStruct((bsz, seq, N_HEADS * HEAD_DIM), BF16),
        scratch_shapes=[pltpu.VMEM((seq, LANES), F32), pltpu.VMEM((seq // HGRN_CHUNK * 8, LANES), F32),
                        pltpu.VMEM((seq, LANES), BF16), pltpu.VMEM((seq, LANES), BF16)],
        compiler_params=_cparams(("parallel", "parallel")),
        name="hgrn2",
    )(z3, z3, z3, z3, lb.reshape(1, N_HEADS * HEAD_DIM), norm_w.reshape(1, HEAD_DIM), sel)


FOX_BLK = 256


def _fox_gate_body(sm_ref, bias_ref, cumt_ref, *, seq):
    c = FOX_BLK
    r = lax.broadcasted_iota(jnp.int32, (c, c), 0)
    cc = lax.broadcasted_iota(jnp.int32, (c, c), 1)
    ltri = (r >= cc).astype(F32)
    r1 = lax.broadcasted_iota(jnp.int32, (LANES, LANES), 0)
    c1 = lax.broadcasted_iota(jnp.int32, (LANES, LANES), 1)
    eye = (r1 == c1).astype(F32)

    def blk(n, carry):
        off = pl.multiple_of(n * c, c)
        ls = jax.nn.log_sigmoid(sm_ref[0, pl.ds(off, c), :] + bias_ref[...])
        cum = carry + jnp.dot(ltri, ls, precision=HI, preferred_element_type=F32)
        cumt_ref[0, :, pl.ds(off, c)] = lax.dot_general(eye, cum, NT, precision=HI, preferred_element_type=F32)
        return cum[c - 1:c, :]

    lax.fori_loop(0, seq // c, blk, jnp.zeros((1, LANES), F32))


def fox_gates(z3, bias_row):
    bsz, seq, _ = z3.shape
    return pl.pallas_call(
        functools.partial(_fox_gate_body, seq=seq),
        grid=(bsz,),
        in_specs=[pl.BlockSpec((1, seq, LANES), lambda b: (b, 0, CD_SMALL)),
                  pl.BlockSpec((1, LANES), lambda b: (0, 0))],
        out_specs=pl.BlockSpec((1, LANES, seq), lambda b: (b, 0, 0)),
        out_shape=jax.ShapeDtypeStruct((bsz, LANES, seq), F32),
        compiler_params=_cparams(("parallel",)),
        name="fox_gates",
    )(z3, bias_row)


ATT_ROWS = 128
ATT_TK = 256


def _for_blocks(n, unroll, body, carry=None):
    if isinstance(n, int):
        for j in range(n):
            carry = body(j, carry)
        return carry

    def it(i, c):
        for u in range(unroll):
            c = body(i * unroll + u, c)
        return c

    return lax.fori_loop(0, n // unroll, it, carry)


def _softmax_pv(s_scr, nblk, unroll, v_fn, m_scr, l_scr, acc_scr):
    _, rows, bw = s_scr.shape
    nlb = bw // LANES
    for ch in range(rows // ATT_ROWS):
        rs = slice(ch * ATT_ROWS, (ch + 1) * ATT_ROWS)

        def pmax(j, mel):
            for b in range(nlb):
                mel = jnp.maximum(mel, s_scr[j, rs, b * LANES:(b + 1) * LANES])
            return mel

        mel = _for_blocks(nblk, unroll, pmax, jnp.full((ATT_ROWS, LANES), NEG, F32))
        m_scr[rs, :] = jnp.broadcast_to(jnp.max(mel, -1, keepdims=True), (ATT_ROWS, LANES))
    l_scr[...] = jnp.zeros_like(l_scr)
    acc_scr[...] = jnp.zeros_like(acc_scr)

    def ppv(j, c):
        pcs = []
        for ch in range(rows // ATT_ROWS):
            rs = slice(ch * ATT_ROWS, (ch + 1) * ATT_ROWS)
            mb = m_scr[rs, :]
            lacc = l_scr[rs, :]
            ps = []
            for b in range(nlb):
                pb = jnp.exp(s_scr[j, rs, b * LANES:(b + 1) * LANES] - mb)
                lacc = lacc + pb
                ps.append(pb.astype(BF16))
            l_scr[rs, :] = lacc
            pcs.append(jnp.concatenate(ps, axis=1))
        acc_scr[...] += jnp.dot(jnp.concatenate(pcs, axis=0), v_fn(j), preferred_element_type=F32)
        return c

    _for_blocks(nblk, unroll, ppv)
    return acc_scr[...] / jnp.sum(l_scr[...], -1, keepdims=True)


def _softmax_scratch(rows, nblk, bw):
    return [pltpu.VMEM((nblk, rows, bw), F32), pltpu.VMEM((rows, LANES), F32),
            pltpu.VMEM((rows, LANES), F32), pltpu.VMEM((rows, HEAD_DIM), F32)]


def _fox_attn_body(q_ref, k_ref, v_ref, cumt_ref, o_ref, s_scr, m_scr, l_scr, acc_scr, *, tq):
    hd = pl.program_id(1)
    qi = pl.program_id(2)
    qall = (q_ref[0] * (HEAD_DIM ** -0.5)).astype(BF16)

    def scores(j):
        off = pl.multiple_of(j * tq, tq)
        kj = k_ref[0, pl.ds(off, tq), :].astype(BF16)
        ck = cumt_ref[0, pl.ds(hd % 8, 1), pl.ds(off, tq)]
        return lax.dot_general(qall, kj, NT, preferred_element_type=F32) - ck

    row = lax.broadcasted_iota(jnp.int32, (tq, tq), 0)
    col = lax.broadcasted_iota(jnp.int32, (tq, tq), 1)

    def v_blk(j):
        return v_ref[0, pl.ds(pl.multiple_of(j * tq, tq), tq), :].astype(BF16)

    def tile(nfull):
        for j in range(nfull):
            s_scr[j] = scores(j)
        s_scr[nfull] = jnp.where(col <= row, scores(nfull), NEG)
        o_ref[0] = _softmax_pv(s_scr, nfull + 1, 1, v_blk, m_scr, l_scr, acc_scr).astype(BF16)

    lax.switch(qi, [functools.partial(tile, n) for n in range(s_scr.shape[0])])


def fox_attention(z3, cumt):
    bsz, seq, _ = z3.shape
    tq = min(512, seq)
    full = lambda cb: pl.BlockSpec((1, seq, LANES), lambda b, h, i: (b, 0, cb + h))
    return pl.pallas_call(
        functools.partial(_fox_attn_body, tq=tq),
        grid=(bsz, N_HEADS, seq // tq),
        in_specs=[pl.BlockSpec((1, tq, LANES), lambda b, h, i: (b, i, CD_FQKV + h)),
                  full(CD_FQKV + 16), full(CD_FQKV + 32),
                  pl.BlockSpec((1, 8, seq), lambda b, h, i: (b, h // 8, 0))],
        out_specs=pl.BlockSpec((1, tq, LANES), lambda b, h, i: (b, i, h)),
        out_shape=jax.ShapeDtypeStruct((bsz, seq, N_HEADS * HEAD_DIM), BF16),
        scratch_shapes=_softmax_scratch(tq, seq // tq, tq),
        compiler_params=_cparams(("parallel", "parallel", "arbitrary")),
        name="fox_attention",
    )(z3, z3, z3, cumt)


def _nsa_cmp_body(zk_ref, zv_ref, pek_ref, pev_ref, wk1_ref, wk2_ref, wv1_ref, wv2_ref, kc_ref, vc_ref, *, nblk):
    def compress(z_ref, pe_ref, w1_ref, w2_ref):
        u1 = jnp.zeros((nblk, HEAD_DIM), F32)
        u2 = jnp.zeros((nblk, HEAD_DIM), F32)
        for l in range(S_CMP):
            zl = z_ref[0, :, l, :]
            u1 = u1 + jnp.dot(zl, w1_ref[l * HEAD_DIM:(l + 1) * HEAD_DIM, :], precision=HI,
                              preferred_element_type=F32)
            u2 = u2 + jnp.dot(zl, w1_ref[(S_CMP + l) * HEAD_DIM:(S_CMP + l + 1) * HEAD_DIM, :], precision=HI,
                              preferred_element_type=F32)
        pe = jnp.dot(jnp.broadcast_to(pe_ref[...], (8, L_CMP * HEAD_DIM)), w1_ref[...], precision=HI,
                     preferred_element_type=F32)[0:1]
        hmid = u1 + pltpu.roll(u2, nblk - 1, 0) + pe
        return jnp.dot(_silu(hmid), w2_ref[...], precision=HI, preferred_element_type=F32)

    kc_ref[0, 0] = compress(zk_ref, pek_ref, wk1_ref, wk2_ref)
    vc_ref[0, 0] = compress(zv_ref, pev_ref, wv1_ref, wv2_ref)


def nsa_compress(z3, pe_k, pe_v, wk1, wk2, wv1, wv2):
    bsz, seq, n = z3.shape
    nblk = seq // S_CMP
    z4 = z3.reshape(bsz, nblk, S_CMP, n)
    zspec = lambda cb: pl.BlockSpec((1, nblk, S_CMP, LANES), lambda b, g: (b, 0, 0, cb + g))
    wfull = lambda a: pl.BlockSpec(a.shape, lambda b, g: (0, 0))
    pek = pe_k.reshape(1, L_CMP * HEAD_DIM)
    pev = pe_v.reshape(1, L_CMP * HEAD_DIM)
    out = jax.ShapeDtypeStruct((bsz, G_NSA, nblk, HEAD_DIM), F32)
    ospec = pl.BlockSpec((1, 1, nblk, HEAD_DIM), lambda b, g: (b, g, 0, 0))
    return pl.pallas_call(
        functools.partial(_nsa_cmp_body, nblk=nblk),
        grid=(bsz, G_NSA),
        in_specs=[zspec(AB_NKV), zspec(AB_NKV + G_NSA), wfull(pek), wfull(pev),
                  wfull(wk1), wfull(wk2), wfull(wv1), wfull(wv2)],
        out_specs=[ospec, ospec],
        out_shape=[out, out],
        compiler_params=_cparams(("parallel", "parallel")),
        name="nsa_compress",
    )(z4, z4, pek, pev, wk1, wk2, wv1, wv2)


NSA_TQ = 128


def _nsa_expand(seq):
    m = np.arange(LANES)[:, None]
    t = np.arange(seq)[None, :]
    return jnp.asarray(((t // L_SLC) == m).astype(np.float32), dtype=BF16)


def _nsa_overlap(seq):
    cs = np.arange(seq // S_CMP)[:, None] * S_CMP
    ss = np.arange(LANES)[None, :] * L_SLC
    ov = (cs < ss + L_SLC) & (cs + L_CMP > ss) & (cs <= seq - L_CMP) & (ss < seq)
    return jnp.asarray(ov.astype(np.float32), dtype=BF16)


NSA_UNROLL = 2
NSA_WIN_BLOCKS = max(e // ATT_TK - max(e - (WINDOW + NSA_TQ - 2), 0) // ATT_TK + 1
                     for e in range(NSA_TQ - 1, 8 * WINDOW, NSA_TQ))


def _nsa_attn_body(q_ref, kc_ref, vc_ref, ks_ref, vs_ref, kw_ref, vw_ref, sm_ref, exp_ref, ovl_ref, o_ref,
                   s_scr, m_scr, l_scr, acc_scr, *, seq):
    g = pl.program_id(1)
    qi = pl.program_id(2)
    tq = NSA_TQ
    rows = HPG_NSA * tq
    nblk = seq // S_CMP
    n_slc = seq // L_SLC
    n_top = min(N_SEL, n_slc)
    q0 = qi * tq

    qall = q_ref[0]
    qs = jnp.concatenate([qall[:, p * HEAD_DIM:(p + 1) * HEAD_DIM] for p in range(HPG_NSA)], axis=0)
    qs = qs * (HEAD_DIM ** -0.5)
    qb = qs.astype(BF16)
    tpos = q0 + lax.broadcasted_iota(jnp.int32, (rows, 1), 0) % tq

    kc = kc_ref[0, 0]
    vc = vc_ref[0, 0]
    qh, ql = _split2(qs)
    kh, kl = _split2(kc)
    sc = lax.dot_general(jnp.concatenate([qh, qh, ql], axis=1), jnp.concatenate([kh, kl, kh], axis=1), NT,
                         preferred_element_type=F32)
    nidx = lax.broadcasted_iota(jnp.int32, (rows, nblk), 1)
    cmask = (nidx * S_CMP + (L_CMP - 1) <= tpos) & (nidx <= nblk - 2)
    scm = jnp.where(cmask, sc, NEG)
    mc = jnp.max(scm, -1, keepdims=True)
    ec = jnp.where(cmask, jnp.exp(scm - mc), 0.0)
    dc = jnp.sum(ec, -1, keepdims=True)
    p_cmp = ec / jnp.where(dc > 0, dc, 1.0)
    o_cmp = jnp.dot(p_cmp.astype(BF16), vc.astype(BF16), preferred_element_type=F32)

    psum = p_cmp[0:tq]
    for p in range(1, HPG_NSA):
        psum = psum + p_cmp[p * tq:(p + 1) * tq]
    ph, pl_ = _split2(psum)
    imp = jnp.dot(jnp.concatenate([ph, pl_], axis=1), jnp.concatenate([ovl_ref[...], ovl_ref[...]], axis=0),
                  preferred_element_type=F32)
    blk = lax.broadcasted_iota(jnp.int32, (tq, LANES), 1)
    cur = (q0 + lax.broadcasted_iota(jnp.int32, (tq, LANES), 0)) // L_SLC
    valid = blk <= cur
    forced = (blk == 0) | (blk == cur) | (blk == cur - 1)
    score = jnp.where(valid, jnp.where(forced, FORCE_SCORE, imp), NEG)
    rank = jnp.zeros((tq, LANES), jnp.int32)
    for mp in range(n_slc):
        colv = score[:, mp:mp + 1]
        rank = rank + ((colv > score) | ((colv == score) & (mp < blk))).astype(jnp.int32)
    selb = (valid & (rank < n_top)).astype(F32).astype(BF16)

    tk = ATT_TK
    qrow = q0 + lax.broadcasted_iota(jnp.int32, (tq, tk), 0)
    col = lax.broadcasted_iota(jnp.int32, (tq, tk), 1)

    def branch(k_ref, v_ref, first, nb, bias_fn):
        def scores(j, c):
            off = pl.multiple_of((first + j) * tk, tk)
            s = lax.dot_general(qb, k_ref[0, pl.ds(off, tk), :].astype(BF16), NT,
                                preferred_element_type=F32)
            bias = bias_fn(off)
            for p in range(HPG_NSA):
                s_scr[j, p * tq:(p + 1) * tq, :] = s[p * tq:(p + 1) * tq] + bias
            return c

        _for_blocks(nb, NSA_UNROLL, scores)

        def v_blk(j):
            return v_ref[0, pl.ds(pl.multiple_of((first + j) * tk, tk), tk), :].astype(BF16)

        return _softmax_pv(s_scr, nb, NSA_UNROLL, v_blk, m_scr, l_scr, acc_scr)

    def slc_bias(off):
        sel = jnp.dot(selb, exp_ref[:, pl.ds(off, tk)], preferred_element_type=F32)
        return jnp.where((sel > 0.5) & (off + col <= qrow), 0.0, NEG)

    last = (q0 + tq - 1) // tk
    o_slc = branch(ks_ref, vs_ref, 0, (last // NSA_UNROLL + 1) * NSA_UNROLL, slc_bias)

    def win_bias(off):
        d = qrow - (off + col)
        return jnp.where((d >= 0) & (d < WINDOW), 0.0, NEG)

    nwin = min(NSA_WIN_BLOCKS, seq // tk)
    o_win = branch(kw_ref, vw_ref, jnp.maximum(last - (nwin - 1), 0), nwin, win_bias)

    sg = jax.nn.sigmoid(sm_ref[0])
    for p in range(HPG_NSA):
        sl = slice(p * tq, (p + 1) * tq)
        base = 2 * N_HEADS + g * HPG_NSA + p
        o = (_lane_col(sg, blk, base) * o_cmp[sl] + _lane_col(sg, blk, base + N_HEADS) * o_slc[sl]
             + _lane_col(sg, blk, base + 2 * N_HEADS) * o_win[sl])
        o_ref[0, :, p * HEAD_DIM:(p + 1) * HEAD_DIM] = o.astype(BF16)


def nsa_attention(z3, kc, vc):
    bsz, seq, _ = z3.shape
    tq = NSA_TQ
    assert seq % tq == 0 and (seq // ATT_TK) % NSA_UNROLL == 0 and seq // L_SLC <= LANES
    nblk = seq // S_CMP
    full = lambda cb: pl.BlockSpec((1, seq, LANES), lambda b, g, i: (b, 0, cb + g))
    cspec = pl.BlockSpec((1, 1, nblk, HEAD_DIM), lambda b, g, i: (b, g, 0, 0))
    qw = HPG_NSA * HEAD_DIM
    return pl.pallas_call(
        functools.partial(_nsa_attn_body, seq=seq),
        grid=(bsz, G_NSA, seq // tq),
        in_specs=[pl.BlockSpec((1, tq, qw), lambda b, g, i: (b, i, AB_NQ * LANES // qw + g)),
                  cspec, cspec,
                  full(AB_NKV + 2 * G_NSA), full(AB_NKV + 3 * G_NSA),
                  full(AB_NKV + 4 * G_NSA), full(AB_NKV + 5 * G_NSA),
                  pl.BlockSpec((1, tq, LANES), lambda b, g, i: (b, i, AB_SMALL)),
                  pl.BlockSpec((LANES, seq), lambda b, g, i: (0, 0)),
                  pl.BlockSpec((nblk, LANES), lambda b, g, i: (0, 0))],
        out_specs=pl.BlockSpec((1, tq, qw), lambda b, g, i: (b, i, g)),
        out_shape=jax.ShapeDtypeStruct((bsz, seq, N_HEADS * HEAD_DIM), BF16),
        scratch_shapes=_softmax_scratch(HPG_NSA * tq, seq // ATT_TK, ATT_TK),
        compiler_params=_cparams(("parallel", "parallel", "arbitrary")),
        name="nsa_attention",
    )(z3, kc, vc, z3, z3, z3, z3, z3, _nsa_expand(seq), _nsa_overlap(seq))


def _repack_body(w_ref, o_ref, *, sections, pad_from):
    o_ref[pad_from:, :] = jnp.zeros((o_ref.shape[0] - pad_from, o_ref.shape[1]), BF16)
    for dst, src, width in sections:
        o_ref[dst:dst + width, :] = w_ref[src:src + width, :].astype(BF16)


def _repack_weight(w, sections, n_out, pad_from, tc=256):
    wt = jnp.swapaxes(w, 0, 1)
    n_in, k = wt.shape
    return pl.pallas_call(
        functools.partial(_repack_body, sections=sections, pad_from=pad_from),
        grid=(k // tc,),
        in_specs=[pl.BlockSpec((n_in, tc), lambda i: (0, i))],
        out_specs=pl.BlockSpec((n_out, tc), lambda i: (0, i)),
        out_shape=jax.ShapeDtypeStruct((n_out, k), BF16),
        compiler_params=_cparams(("parallel",)),
        name="repack_weight",
    )(wt)


def _ab_weight(w):
    small = AB_SMALL * LANES
    sections = ((0, 0, 6144), (6144, 6176, 2048), (8192, 8224, 2048), (10240, 10272, 3072),
                (small, 6144, 32), (small + 32, 13344, 48))
    return _repack_weight(w, sections, AB_N, small)


def _cd_weight(w):
    small = CD_SMALL * LANES
    return _repack_weight(w, ((0, 0, small), (small, small, 16)), CD_N, small)


def _row128(v):
    return jnp.pad(v.astype(F32), (0, LANES - v.shape[0])).reshape(1, LANES)


def kernel(x, p, ab_norm_pre, ab_norm_post, ab_w_in, gdn_conv_w, gdn_a_log, gdn_dt_bias, gdn_norm, nsa_pe_k, nsa_pe_v, nsa_cmp_k1, nsa_cmp_k2, nsa_cmp_v1, nsa_cmp_v2, ab_w_out, cd_norm_pre, cd_norm_post, cd_w_in, hgrn_lb_logits, hgrn_norm, fox_f_bias, cd_w_out, ffn_norm_pre, ffn_norm_post, ffn_w_up, ffn_conv_w, ffn_conv_b, ffn_w_down, ple_w_proj, ple_gate_norm, ple_w_gate, ple_norm_post):
    bsz, seq, dm = x.shape
    depth = p.shape[0]
    m = bsz * seq
    half = N_HEADS * HEAD_DIM
    sm_ = jax.nn.softmax(hgrn_lb_logits.astype(F32), axis=0)
    lb_table = jnp.cumsum(sm_, axis=0) - sm_[0]
    xf = x.reshape(m, dm)
    pf = p.reshape(depth, m, -1)
    w_up, w_down = ffn_w_up.astype(BF16), ffn_w_down.astype(BF16)
    w_gate, w_proj = ple_w_gate.astype(BF16), ple_w_proj.astype(BF16)
    for li in range(depth):
        j = li // 2
        if li % 2 == 0:
            z3 = norm_matmul(xf, ab_norm_pre[j], _ab_weight(ab_w_in[j])).reshape(bsz, seq, AB_N)
            prm = jnp.concatenate([_row128(gdn_a_log[j]), _row128(gdn_dt_bias[j]),
                                   jnp.zeros((6, LANES), F32)], axis=0)
            o_a = gated_deltanet(z3, gdn_conv_w[j], prm, gdn_norm[j])
            kc, vc = nsa_compress(z3, nsa_pe_k[j], nsa_pe_v[j], nsa_cmp_k1[j], nsa_cmp_k2[j],
                                  nsa_cmp_v1[j], nsa_cmp_v2[j])
            o_b = nsa_attention(z3, kc, vc)
            w_out, post = ab_w_out[j], ab_norm_post[j]
        else:
            z3 = norm_matmul(xf, cd_norm_pre[j], _cd_weight(cd_w_in[j])).reshape(bsz, seq, CD_N)
            o_a = hgrn2(z3, lb_table[li], hgrn_norm[j])
            o_b = fox_attention(z3, fox_gates(z3, _row128(fox_f_bias[j])))
            w_out, post = cd_w_out[j], cd_norm_post[j]
        xf = outproj(o_a.reshape(m, half), o_b.reshape(m, half), w_out.astype(BF16), xf, post)
        xf = conv_ffn(xf, li, ffn_norm_pre, w_up, ffn_conv_w, ffn_conv_b, w_down, ffn_norm_post, seq)
        xf = ple(xf, li, pf, ple_gate_norm, w_gate, w_proj, ple_norm_post)
    return xf.reshape(bsz, seq, dm)
```

```python
import functools

import numpy as np
import jax
import jax.numpy as jnp
from jax import lax
from jax.experimental import pallas as pl
from jax.experimental.pallas import tpu as pltpu

F32 = jnp.float32
BF16 = jnp.bfloat16
HI = lax.Precision.HIGHEST
NT = (((1,), (1,)), ((), ()))
TN = (((0,), (0,)), ((), ()))

HEAD_DIM = 128
N_HEADS = 16
G_NSA = 4
HPG_NSA = 4
L_CMP = 32
S_CMP = 16
L_SLC = 64
N_SEL = 8
WINDOW = 512
FORCE_SCORE = 1e4
GDN_CHUNK = 64
GDN_GROUP = 256
GDN_LOCKSTEP = 8
GDN_HEADS = 2
HGRN_CHUNK = 256
HGRN_LOCKSTEP = 4
EPS = 1e-6
NEG = -1e30
LANES = 128

AB_QKV, AB_GATE, AB_NQ, AB_NKV, AB_SMALL = 0, 48, 64, 80, 104
AB_N = 108 * LANES
CD_Q, CD_F, CD_I, CD_G, CD_FQKV, CD_SMALL = 0, 16, 32, 48, 64, 112
CD_N = 116 * LANES

VMEM_LIMIT = 56 * 1024 * 1024


def _cparams(sem):
    return pltpu.CompilerParams(dimension_semantics=sem, vmem_limit_bytes=VMEM_LIMIT)


def _rms(x, w):
    return x * lax.rsqrt(jnp.mean(x * x, axis=-1, keepdims=True) + EPS) * w


def _silu(x):
    return x * jax.nn.sigmoid(x)


def _lane_col(x, lane_idx, lane):
    return jnp.sum(jnp.where(lane_idx == lane, x, 0.0), axis=-1, keepdims=True)


def _split2(x):
    hi = x.astype(BF16)
    return hi, (x - hi.astype(F32)).astype(BF16)


def _dot_sel(sel, x):
    n = x.shape[1]
    hi, lo = _split2(x)
    y = jnp.dot(sel, jnp.concatenate([hi, lo], axis=1), preferred_element_type=F32)
    return y[:, :n] + y[:, n:]


def _norm_matmul_body(x_ref, nw_ref, w_ref, o_ref, h_ref):
    @pl.when(pl.program_id(1) == 0)
    def _():
        h_ref[...] = _rms(x_ref[...], nw_ref[...]).astype(BF16)

    o_ref[...] = lax.dot_general(h_ref[...], w_ref[...], NT, preferred_element_type=F32)


def norm_matmul(x, nw, wt, tm=512, tn=512):
    m, k = x.shape
    n = wt.shape[0]
    return pl.pallas_call(
        _norm_matmul_body,
        grid=(m // tm, n // tn),
        in_specs=[pl.BlockSpec((tm, k), lambda i, j: (i, 0)),
                  pl.BlockSpec((1, k), lambda i, j: (0, 0)),
                  pl.BlockSpec((tn, k), lambda i, j: (j, 0))],
        out_specs=pl.BlockSpec((tm, tn), lambda i, j: (i, j)),
        out_shape=jax.ShapeDtypeStruct((m, n), F32),
        scratch_shapes=[pltpu.VMEM((tm, k), BF16)],
        compiler_params=_cparams(("parallel", "arbitrary")),
        name="norm_matmul",
    )(x, nw.reshape(1, k), wt)


def _outproj_body(oa_ref, ob_ref, wa_ref, wb_ref, x_ref, nw_ref, o_ref, *, tn):
    j = pl.program_id(1)
    y = jnp.dot(oa_ref[...], wa_ref[...], preferred_element_type=F32)
    y = y + jnp.dot(ob_ref[...], wb_ref[...], preferred_element_type=F32)
    o_ref[:, pl.ds(pl.multiple_of(j * tn, tn), tn)] = y

    @pl.when(j == pl.num_programs(1) - 1)
    def _():
        o_ref[...] = x_ref[...] + _rms(o_ref[...], nw_ref[...])


def outproj(oa, ob, w, x, nw, tm=512, tn=512):
    m, ka = oa.shape
    n = w.shape[1]
    return pl.pallas_call(
        functools.partial(_outproj_body, tn=tn),
        grid=(m // tm, n // tn),
        in_specs=[pl.BlockSpec((tm, ka), lambda i, j: (i, 0)),
                  pl.BlockSpec((tm, ka), lambda i, j: (i, 0)),
                  pl.BlockSpec((ka, tn), lambda i, j: (0, j)),
                  pl.BlockSpec((ka, tn), lambda i, j: (1, j)),
                  pl.BlockSpec((tm, n), lambda i, j: (i, 0), pipeline_mode=pl.Buffered(1)),
                  pl.BlockSpec((1, n), lambda i, j: (0, 0))],
        out_specs=pl.BlockSpec((tm, n), lambda i, j: (i, 0)),
        out_shape=jax.ShapeDtypeStruct((m, n), F32),
        compiler_params=_cparams(("parallel", "arbitrary")),
        name="outproj",
    )(oa, ob, w, w, x, nw.reshape(1, n))


FFN_HALO = 16


def _ffn_body(x_ref, xh_ref, nw_ref, wg_ref, wu_ref, cwg_ref, cwu_ref, cbg_ref, cbu_ref,
              wd_ref, pw_ref, o_ref, h_s, *, tm, seq):
    i = pl.program_id(0)
    j = pl.program_id(1)

    @pl.when(j == 0)
    def _():
        keep = jnp.where((i * tm) % seq == 0, 0.0, 1.0)
        h_s[0:FFN_HALO, :] = (_rms(xh_ref[...], nw_ref[...]) * keep).astype(BF16)
        h_s[FFN_HALO:, :] = _rms(x_ref[...], nw_ref[...]).astype(BF16)
        o_ref[...] = jnp.zeros_like(o_ref)

    h = h_s[...]

    def branch(w_ref, cw_ref, cb_ref):
        u = jnp.dot(h, w_ref[...], preferred_element_type=F32)
        cw = cw_ref[...]
        y = u * cw[2:3] + pltpu.roll(u, 1, 0) * cw[1:2] + pltpu.roll(u, 2, 0) * cw[0:1]
        return y[FFN_HALO:] + cb_ref[...]

    g = branch(wg_ref, cwg_ref, cbg_ref)
    u = branch(wu_ref, cwu_ref, cbu_ref)
    act = (_silu(g) * u).astype(BF16)
    o_ref[...] += jnp.dot(act, wd_ref[...], preferred_element_type=F32)

    @pl.when(j == pl.num_programs(1) - 1)
    def _():
        o_ref[...] = x_ref[...] + _rms(o_ref[...], pw_ref[...])


def conv_ffn(x, li, nw, w_up, conv_w, conv_b, w_down, pw, seq, tm=512, tf=256):
    m, k = x.shape
    nl, f, _ = w_down.shape
    nf = f // tf
    hb = tm // FFN_HALO
    cb = conv_b.reshape(nl, 1, 2 * f)
    row = pl.BlockSpec((None, 1, k), lambda i, j: (li, 0, 0))
    return pl.pallas_call(
        functools.partial(_ffn_body, tm=tm, seq=seq),
        grid=(m // tm, nf),
        in_specs=[pl.BlockSpec((tm, k), lambda i, j: (i, 0), pipeline_mode=pl.Buffered(1)),
                  pl.BlockSpec((FFN_HALO, k), lambda i, j: (jnp.maximum(i * hb - 1, 0), 0)),
                  row,
                  pl.BlockSpec((None, k, tf), lambda i, j: (li, 0, j)),
                  pl.BlockSpec((None, k, tf), lambda i, j: (li, 0, nf + j)),
                  pl.BlockSpec((None, 3, tf), lambda i, j: (li, 0, j)),
                  pl.BlockSpec((None, 3, tf), lambda i, j: (li, 0, nf + j)),
                  pl.BlockSpec((None, 1, tf), lambda i, j: (li, 0, j)),
                  pl.BlockSpec((None, 1, tf), lambda i, j: (li, 0, nf + j)),
                  pl.BlockSpec((None, tf, k), lambda i, j: (li, j, 0)),
                  row],
        out_specs=pl.BlockSpec((tm, k), lambda i, j: (i, 0)),
        out_shape=jax.ShapeDtypeStruct((m, k), F32),
        scratch_shapes=[pltpu.VMEM((tm + FFN_HALO, k), BF16)],
        compiler_params=_cparams(("parallel", "arbitrary")),
        name="conv_ffn",
    )(x, x, nw.reshape(nl, 1, k), w_up, w_up, conv_w, conv_w, cb, cb, w_down, pw.reshape(nl, 1, k))


def _ple_body(x_ref, p_ref, gnw_ref, wg_ref, wp_ref, pnw_ref, o_ref, h_s, *, tn):
    j = pl.program_id(1)

    @pl.when(j == 0)
    def _():
        h_s[...] = _rms(x_ref[...], gnw_ref[...]).astype(BF16)

    gate = jax.nn.sigmoid(jnp.dot(h_s[...], wg_ref[...], preferred_element_type=F32))
    proj = jnp.dot(p_ref[...].astype(BF16), wp_ref[...], preferred_element_type=F32)
    o_ref[:, pl.ds(pl.multiple_of(j * tn, tn), tn)] = gate * proj

    @pl.when(j == pl.num_programs(1) - 1)
    def _():
        o_ref[...] = x_ref[...] + _rms(o_ref[...], pnw_ref[...])


def ple(x, li, p, gnw, wg, wp, pnw, tm=512, tn=512):
    m, k = x.shape
    nl, _, dp = p.shape
    row = pl.BlockSpec((None, 1, k), lambda i, j: (li, 0, 0))
    return pl.pallas_call(
        functools.partial(_ple_body, tn=tn),
        grid=(m // tm, k // tn),
        in_specs=[pl.BlockSpec((tm, k), lambda i, j: (i, 0), pipeline_mode=pl.Buffered(1)),
                  pl.BlockSpec((None, tm, dp), lambda i, j: (li, i, 0)),
                  row,
                  pl.BlockSpec((None, k, tn), lambda i, j: (li, 0, j)),
                  pl.BlockSpec((None, dp, tn), lambda i, j: (li, 0, j)),
                  row],
        out_specs=pl.BlockSpec((tm, k), lambda i, j: (i, 0)),
        out_shape=jax.ShapeDtypeStruct((m, k), F32),
        scratch_shapes=[pltpu.VMEM((tm, k), BF16)],
        compiler_params=_cparams(("parallel", "arbitrary")),
        name="ple",
    )(x, p, gnw.reshape(nl, 1, k), wg, wp, pnw.reshape(nl, 1, k))


def _gdn_ltri():
    r = np.arange(GDN_GROUP)[:, None]
    t = np.arange(GDN_GROUP)[None, :]
    return jnp.asarray((((r // GDN_CHUNK) == (t // GDN_CHUNK)) & (r >= t)).astype(np.float32), dtype=BF16)


def _gdn_head_setup(hd, lanes, zq_ref, zk_ref, zv_ref, sm_ref, cwq_ref, cwk_ref, cwv_ref, prm_ref, ltri_ref,
                    q_s, k_s, v_s, g_s, b_s, o0_s, n0_s, gl_s, qe_s, m_s, seq):
    c = GDN_CHUNK
    gs = GDN_GROUP
    row = lax.broadcasted_iota(jnp.int32, (seq, LANES), 0)
    lane = lax.broadcasted_iota(jnp.int32, (seq, LANES), 1)

    def conv_silu(z_ref, w_ref):
        z = z_ref[0, :, lanes]
        w = w_ref[:, lanes]
        y = z * w[3:4]
        for s in (1, 2, 3):
            y = y + jnp.where(row >= s, pltpu.roll(z, s, 0), 0.0) * w[3 - s:4 - s]
        return _silu(y)

    q = conv_silu(zq_ref, cwq_ref)
    k = conv_silu(zk_ref, cwk_ref)
    q_s[...] = q * lax.rsqrt(jnp.sum(q * q, -1, keepdims=True) + EPS) * (HEAD_DIM ** -0.5)
    k_s[...] = k * lax.rsqrt(jnp.sum(k * k, -1, keepdims=True) + EPS)
    v_s[...] = conv_silu(zv_ref, cwv_ref)

    sm = sm_ref[0]
    glog = -jnp.exp(prm_ref[0:1, :]) * jax.nn.softplus(sm + prm_ref[1:2, :])
    g_s[...] = jnp.broadcast_to(_lane_col(glog, lane, hd), (seq, LANES))
    b_s[...] = jnp.broadcast_to(_lane_col(jax.nn.sigmoid(sm), lane, N_HEADS + hd), (seq, LANES))

    nper = gs // c
    rs = lax.broadcasted_iota(jnp.int32, (c, gs), 0)
    cs = lax.broadcasted_iota(jnp.int32, (c, gs), 1)
    jj = cs % c
    cblk = cs // c
    incl = rs >= jj
    strict = rs > jj
    eye_side = (rs == jj).astype(F32)
    r2 = lax.broadcasted_iota(jnp.int32, (gs, gs), 0)
    c2 = lax.broadcasted_iota(jnp.int32, (gs, gs), 1)
    bdm = ((r2 // c) == (c2 // c)).astype(F32)
    bdm_b = bdm.astype(BF16)
    lane_c = lax.broadcasted_iota(jnp.int32, (c, LANES), 1)
    lane_g = lax.broadcasted_iota(jnp.int32, (gs, LANES), 1)
    ones3 = (lane_c < 3).astype(F32).astype(BF16)

    def side_col(xb):
        left = jnp.where(lane_c < c, xb[0:c], xb[c:2 * c])
        right = jnp.where(lane_c < c, xb[2 * c:3 * c], xb[3 * c:4 * c])
        return jnp.concatenate([left, right], axis=1)

    def to_side(full):
        out = full[0:c]
        for ci in range(1, nper):
            out = jnp.where(cblk == ci, full[ci * c:(ci + 1) * c], out)
        return out

    def bd(xs, mask):
        return jnp.concatenate([xs] * nper, axis=0) * mask

    def mm3(ph, pl_, xh, xl):
        bh = bd(xh, bdm_b)
        lhs = jnp.concatenate([ph, ph, pl_], axis=1)
        rhs = jnp.concatenate([bh, bd(xl, bdm_b), bh], axis=0)
        return jnp.dot(lhs, rhs, preferred_element_type=F32)

    def group_pre(gi):
        off = pl.multiple_of(gi * gs, gs)
        qg = q_s[pl.ds(off, gs), :]
        kg = k_s[pl.ds(off, gs), :]
        bg = b_s[pl.ds(off, gs), :]
        gam = _dot_sel(ltri_ref[...], g_s[pl.ds(off, gs), :])
        g_hi = gam.astype(BF16).astype(F32)
        g_mid = (gam - g_hi).astype(BF16).astype(F32)
        g_lo = gam - g_hi - g_mid
        gam3 = jnp.where(lane_g == 0, g_hi, jnp.where(lane_g == 1, g_mid, jnp.where(lane_g == 2, g_lo, 0.0)))
        gam_row = lax.dot_general(ones3, gam3.astype(BF16), NT, preferred_element_type=F32)
        dec = jnp.where(incl, jnp.exp(jnp.where(incl, side_col(gam) - gam_row, 0.0)), 0.0)
        qb = qg.astype(BF16)
        kb = kg.astype(BF16)
        kk = to_side(lax.dot_general(kb, kb, NT, preferred_element_type=F32))
        a = jnp.where(strict, side_col(bg) * dec * kk, 0.0)
        return dict(off=off, qg=qg, kg=kg, bg=bg, gam=gam, dec=dec, qb=qb, kb=kb, a=a)

    def group_post(gi, st, tinv):
        off, qg, kg, bg, gam, dec, qb, kb = (st[n] for n in ("off", "qg", "kg", "bg", "gam", "dec", "qb", "kb"))
        vg = v_s[pl.ds(off, gs), :]
        th, tl = _split2(tinv)
        eg = jnp.exp(gam)
        rh, rl = _split2(jnp.concatenate([vg * bg, kg * (bg * eg)], axis=1))
        bth = bd(th, bdm_b)
        sol = jnp.dot(jnp.concatenate([bth, bth, bd(tl, bdm_b)], axis=1),
                      jnp.concatenate([rh, rl, rh], axis=0), preferred_element_type=F32)
        solb = sol.astype(BF16)
        qk = bd(dec, bdm) * lax.dot_general(qb, kb, NT, preferred_element_type=F32)
        qkuw = jnp.dot(qk.astype(BF16), solb, preferred_element_type=F32)
        o0_s[pl.ds(off, gs), :] = qkuw[:, :HEAD_DIM]
        qe_s[pl.ds(off, gs), :] = (qg * eg - qkuw[:, HEAD_DIM:]).astype(BF16)
        for ci in range(nper):
            lo = ci * c
            gl = gam[lo + c - 1:lo + c, :]
            kd = (kg[lo:lo + c] * jnp.exp(gl - gam[lo:lo + c])).astype(BF16)
            kds = lax.dot_general(kd, solb[lo:lo + c], TN, preferred_element_type=F32)
            so = pl.multiple_of((gi * nper + ci) * HEAD_DIM, HEAD_DIM)
            n0_s[pl.ds(so, HEAD_DIM), :] = kds[:, :HEAD_DIM]
            m_s[pl.ds(so, HEAD_DIM), :] = (-kds[:, HEAD_DIM:]).astype(BF16)
            gl_s[pl.ds(pl.multiple_of((gi * nper + ci) * 8, 8), 8), :] = jnp.broadcast_to(jnp.exp(gl), (8, LANES))

    ngroups = seq // gs
    nlock = GDN_LOCKSTEP if ngroups % GDN_LOCKSTEP == 0 else 1

    def groups(it, carry):
        gis = [it + u * (ngroups // nlock) for u in range(nlock)]
        sts = [group_pre(gi) for gi in gis]
        tinvs = [eye_side - st["a"] for st in sts]
        xs = [_split2(st["a"]) for st in sts]
        for _ in range(5):
            xs = [_split2(mm3(xh, xl, xh, xl)) for xh, xl in xs]
            ts = [_split2(t) for t in tinvs]
            tinvs = [t + mm3(th, tl, xh, xl) for t, (th, tl), (xh, xl) in zip(tinvs, ts, xs)]
        for gi, st, t in zip(gis, sts, tinvs):
            group_post(gi, st, t)
        return carry

    lax.fori_loop(0, ngroups // nlock, groups, 0)


def _gdn_body(zq_ref, zk_ref, zv_ref, zg_ref, sm_ref, cwq_ref, cwk_ref, cwv_ref, prm_ref, nw_ref, ltri_ref,
              o_ref, *scratch, seq):
    c = GDN_CHUNK
    nh = GDN_HEADS
    o0_s, n0_s, gl_s, qe_s, m_s = scratch[5:]
    for hh in range(nh):
        _gdn_head_setup(pl.program_id(1) * nh + hh, slice(hh * LANES, (hh + 1) * LANES), zq_ref, zk_ref, zv_ref,
                        sm_ref, cwq_ref, cwk_ref, cwv_ref, prm_ref, ltri_ref, *(r.at[hh] for r in scratch), seq)

    def step(n, states):
        off = pl.multiple_of(n * c, c)
        so = pl.multiple_of(n * HEAD_DIM, HEAD_DIM)
        sbs = [s.astype(BF16) for s in states]
        for hh in range(nh):
            o0_s[hh, pl.ds(off, c), :] = o0_s[hh, pl.ds(off, c), :] + jnp.dot(
                qe_s[hh, pl.ds(off, c), :], sbs[hh], preferred_element_type=F32)
        ms = [jnp.dot(m_s[hh, pl.ds(so, HEAD_DIM), :], sbs[hh], preferred_element_type=F32) for hh in range(nh)]
        return tuple(gl_s[hh, pl.ds(pl.multiple_of(n * 8, 8), 1), :] * states[hh] + n0_s[hh, pl.ds(so, HEAD_DIM), :]
                     + ms[hh] for hh in range(nh))

    lax.fori_loop(0, seq // c, step, tuple(jnp.zeros((HEAD_DIM, HEAD_DIM), F32) for _ in range(nh)))
    for hh in range(nh):
        lanes = slice(hh * LANES, (hh + 1) * LANES)
        o_ref[0, :, lanes] = (_rms(o0_s[hh], nw_ref[...]) * _silu(zg_ref[0, :, lanes])).astype(BF16)


def gated_deltanet(z3, conv_w, prm, norm_w):
    bsz, seq, _ = z3.shape
    nh = GDN_HEADS
    zspec = lambda cb: pl.BlockSpec((1, seq, nh * LANES), lambda b, h: (b, 0, cb // nh + h))
    wspec = lambda cb: pl.BlockSpec((4, nh * LANES), lambda b, h: (0, cb // nh + h))
    nchunk = seq // GDN_CHUNK
    return pl.pallas_call(
        functools.partial(_gdn_body, seq=seq),
        grid=(bsz, N_HEADS // nh),
        in_specs=[zspec(AB_QKV), zspec(AB_QKV + 16), zspec(AB_QKV + 32), zspec(AB_GATE),
                  pl.BlockSpec((1, seq, LANES), lambda b, h: (b, 0, AB_SMALL)),
                  wspec(0), wspec(16), wspec(32),
                  pl.BlockSpec((8, LANES), lambda b, h: (0, 0)),
                  pl.BlockSpec((1, LANES), lambda b, h: (0, 0)),
                  pl.BlockSpec((GDN_GROUP, GDN_GROUP), lambda b, h: (0, 0))],
        out_specs=pl.BlockSpec((1, seq, nh * LANES), lambda b, h: (b, 0, h)),
        out_shape=jax.ShapeDtypeStruct((bsz, seq, N_HEADS * HEAD_DIM), BF16),
        scratch_shapes=[pltpu.VMEM((nh, seq, LANES), F32) for _ in range(6)]
        + [pltpu.VMEM((nh, nchunk * HEAD_DIM, HEAD_DIM), F32),
           pltpu.VMEM((nh, nchunk * 8, LANES), F32),
           pltpu.VMEM((nh, seq, LANES), BF16),
           pltpu.VMEM((nh, nchunk * HEAD_DIM, HEAD_DIM), BF16)],
        compiler_params=_cparams(("parallel", "parallel")),
        name="gated_deltanet",
    )(z3, z3, z3, z3, z3, conv_w, conv_w, conv_w, prm, norm_w.reshape(1, HEAD_DIM), _gdn_ltri())


HGRN_LEVELS = tuple(HGRN_CHUNK >> (i + 1) for i in range(8))


def _hgrn_selectors():
    c = HGRN_CHUNK
    r = np.arange(c)[:, None]
    t = np.arange(c)[None, :]
    mats = [r >= t]
    for s in HGRN_LEVELS:
        isq = ((r // s) % 2) == 1
        mats.append(((r // s) == (t // s)) & ((isq & (t <= r)) | (~isq & (t > r))))
    return jnp.asarray(np.concatenate(mats, axis=0).astype(np.float32), dtype=BF16)


def _hgrn_body(zq_ref, zf_ref, zi_ref, zg_ref, lb_ref, nw_ref, sel_ref, o_ref, oi_s, ebl_s, qd_s, kd_s, *, seq):
    c = HGRN_CHUNK
    r = lax.broadcasted_iota(jnp.int32, (c, c), 0)
    cc = lax.broadcasted_iota(jnp.int32, (c, c), 1)
    diag = r == cc
    x = r ^ cc
    lev = jnp.zeros((c, c), jnp.int32)
    for bit in range(1, 8):
        lev = lev + (x >= (1 << bit)).astype(jnp.int32)
    lev = jnp.where(r > cc, lev, -1)
    rr = lax.broadcasted_iota(jnp.int32, (c, LANES), 0)
    lb = lb_ref[...]
    log_lb = jnp.log(lb)
    log_1m_lb = jnp.log1p(-lb)

    nchunk = seq // c
    nlock = HGRN_LOCKSTEP if nchunk % HGRN_LOCKSTEP == 0 else 1

    def intra(it, carry):
        ns = [it + u * (nchunk // nlock) for u in range(nlock)]
        offs = [pl.multiple_of(n * c, c) for n in ns]
        fxs = [zf_ref[0, pl.ds(off, c), :] for off in offs]
        qs_ = [_silu(zq_ref[0, pl.ds(off, c), :]) for off in offs]
        lfs, ks_ = [], []
        for fx in fxs:
            t = jnp.exp(-jnp.abs(fx))
            gate = log_1m_lb + (jnp.minimum(fx, 0.0) - jnp.log(1.0 + t))
            lfs.append(jnp.maximum(log_lb, gate) + jnp.log(1.0 + jnp.exp(-jnp.abs(log_lb - gate))))
            ks_.append((1.0 - lb) * (jnp.where(fx >= 0.0, t, 1.0) / (1.0 + t)))
        sums = [_dot_sel(sel_ref[...], lf) for lf in lfs]
        atts = [jnp.where(diag, jnp.sum(q * k, -1, keepdims=True), 0.0) for q, k in zip(qs_, ks_)]
        for i, s in enumerate(HGRN_LEVELS):
            isq = ((rr // s) % 2) == 1
            parts = []
            for q, k, sm in zip(qs_, ks_, sums):
                e = jnp.exp(sm[(i + 1) * c:(i + 2) * c])
                parts.append(lax.dot_general(jnp.where(isq, q * e, 0.0).astype(BF16),
                                             jnp.where(isq, 0.0, k * e).astype(BF16), NT,
                                             preferred_element_type=F32))
            atts = [jnp.where(lev == (7 - i), part, att) for part, att in zip(parts, atts)]
        for n, off, q, k, sm, att in zip(ns, offs, qs_, ks_, sums, atts):
            b = sm[0:c]
            bl = b[c - 1:c, :]
            vb = zi_ref[0, pl.ds(off, c), :].astype(BF16)
            oi_s[pl.ds(off, c), :] = jnp.dot(att.astype(BF16), vb, preferred_element_type=F32)
            qd_s[pl.ds(off, c), :] = (q * jnp.exp(b)).astype(BF16)
            kd_s[pl.ds(off, c), :] = (k * jnp.exp(bl - b)).astype(BF16)
            ebl_s[pl.ds(pl.multiple_of(n * 8, 8), 8), :] = jnp.broadcast_to(jnp.exp(bl), (8, LANES))
        return carry

    lax.fori_loop(0, nchunk // nlock, intra, 0)

    def scan(n, st):
        off = pl.multiple_of(n * c, c)
        oi_s[pl.ds(off, c), :] = oi_s[pl.ds(off, c), :] + lax.dot_general(
            qd_s[pl.ds(off, c), :], st.astype(BF16), NT, preferred_element_type=F32)
        vb = zi_ref[0, pl.ds(off, c), :].astype(BF16)
        return (st * ebl_s[pl.ds(pl.multiple_of(n * 8, 8), 1), :]
                + lax.dot_general(vb, kd_s[pl.ds(off, c), :], TN, preferred_element_type=F32))

    lax.fori_loop(0, nchunk, scan, jnp.zeros((HEAD_DIM, HEAD_DIM), F32))
    o_ref[0] = (_rms(oi_s[...], nw_ref[...]) * _silu(zg_ref[0])).astype(BF16)


def hgrn2(z3, lb, norm_w):
    bsz, seq, _ = z3.shape
    zspec = lambda cb: pl.BlockSpec((1, seq, LANES), lambda b, h: (b, 0, cb + h))
    sel = _hgrn_selectors()
    return pl.pallas_call(
        functools.partial(_hgrn_body, seq=seq),
        grid=(bsz, N_HEADS),
        in_specs=[zspec(CD_Q), zspec(CD_F), zspec(CD_I), zspec(CD_G),
                  pl.BlockSpec((1, LANES), lambda b, h: (0, h)),
                  pl.BlockSpec((1, LANES), lambda b, h: (0, 0)),
                  pl.BlockSpec(sel.shape, lambda b, h: (0, 0))],
        out_specs=pl.BlockSpec((1, seq, LANES), lambda b, h: (b, 0, h)),
        out_shape=jax.ShapeDtypeStruct((bsz, seq, N_HEADS * HEAD_DIM), BF16),
        scratch_shapes=[pltpu.VMEM((seq, LANES), F32), pltpu.VMEM((seq // HGRN_CHUNK * 8, LANES), F32),
                        pltpu.VMEM((seq, LANES), BF16), pltpu.VMEM((seq, LANES), BF16)],
        compiler_params=_cparams(("parallel", "parallel")),
        name="hgrn2",
    )(z3, z3, z3, z3, lb.reshape(1, N_HEADS * HEAD_DIM), norm_w.reshape(1, HEAD_DIM), sel)


FOX_BLK = 256


def _fox_gate_body(sm_ref, bias_ref, cumt_ref, *, seq):
    c = FOX_BLK
    r = lax.broadcasted_iota(jnp.int32, (c, c), 0)
    cc = lax.broadcasted_iota(jnp.int32, (c, c), 1)
    ltri = (r >= cc).astype(F32)
    r1 = lax.broadcasted_iota(jnp.int32, (LANES, LANES), 0)
    c1 = lax.broadcasted_iota(jnp.int32, (LANES, LANES), 1)
    eye = (r1 == c1).astype(F32)

    def blk(n, carry):
        off = pl.multiple_of(n * c, c)
        ls = jax.nn.log_sigmoid(sm_ref[0, pl.ds(off, c), :] + bias_ref[...])
        cum = carry + jnp.dot(ltri, ls, precision=HI, preferred_element_type=F32)
        cumt_ref[0, :, pl.ds(off, c)] = lax.dot_general(eye, cum, NT, precision=HI, preferred_element_type=F32)
        return cum[c - 1:c, :]

    lax.fori_loop(0, seq // c, blk, jnp.zeros((1, LANES), F32))


def fox_gates(z3, bias_row):
    bsz, seq, _ = z3.shape
    return pl.pallas_call(
        functools.partial(_fox_gate_body, seq=seq),
        grid=(bsz,),
        in_specs=[pl.BlockSpec((1, seq, LANES), lambda b: (b, 0, CD_SMALL)),
                  pl.BlockSpec((1, LANES), lambda b: (0, 0))],
        out_specs=pl.BlockSpec((1, LANES, seq), lambda b: (b, 0, 0)),
        out_shape=jax.ShapeDtypeStruct((bsz, LANES, seq), F32),
        compiler_params=_cparams(("parallel",)),
        name="fox_gates",
    )(z3, bias_row)


ATT_ROWS = 128
ATT_TK = 256


def _for_blocks(n, unroll, body, carry=None):
    if isinstance(n, int):
        for j in range(n):
            carry = body(j, carry)
        return carry

    def it(i, c):
        for u in range(unroll):
            c = body(i * unroll + u, c)
        return c

    return lax.fori_loop(0, n // unroll, it, carry)


def _softmax_pv(s_scr, nblk, unroll, v_fn, m_scr, l_scr, acc_scr):
    _, rows, bw = s_scr.shape
    nlb = bw // LANES
    for ch in range(rows // ATT_ROWS):
        rs = slice(ch * ATT_ROWS, (ch + 1) * ATT_ROWS)

        def pmax(j, mel):
            for b in range(nlb):
                mel = jnp.maximum(mel, s_scr[j, rs, b * LANES:(b + 1) * LANES])
            return mel

        mel = _for_blocks(nblk, unroll, pmax, jnp.full((ATT_ROWS, LANES), NEG, F32))
        m_scr[rs, :] = jnp.broadcast_to(jnp.max(mel, -1, keepdims=True), (ATT_ROWS, LANES))
    l_scr[...] = jnp.zeros_like(l_scr)
    acc_scr[...] = jnp.zeros_like(acc_scr)

    def ppv(j, c):
        pcs = []
        for ch in range(rows // ATT_ROWS):
            rs = slice(ch * ATT_ROWS, (ch + 1) * ATT_ROWS)
            mb = m_scr[rs, :]
            lacc = l_scr[rs, :]
            ps = []
            for b in range(nlb):
                pb = jnp.exp(s_scr[j, rs, b * LANES:(b + 1) * LANES] - mb)
                lacc = lacc + pb
                ps.append(pb.astype(BF16))
            l_scr[rs, :] = lacc
            pcs.append(jnp.concatenate(ps, axis=1))
        acc_scr[...] += jnp.dot(jnp.concatenate(pcs, axis=0), v_fn(j), preferred_element_type=F32)
        return c

    _for_blocks(nblk, unroll, ppv)
    return acc_scr[...] / jnp.sum(l_scr[...], -1, keepdims=True)


def _softmax_scratch(rows, nblk, bw):
    return [pltpu.VMEM((nblk, rows, bw), F32), pltpu.VMEM((rows, LANES), F32),
            pltpu.VMEM((rows, LANES), F32), pltpu.VMEM((rows, HEAD_DIM), F32)]


def _fox_attn_body(q_ref, k_ref, v_ref, cumt_ref, o_ref, s_scr, m_scr, l_scr, acc_scr, *, tq):
    hd = pl.program_id(1)
    qi = pl.program_id(2)
    qall = (q_ref[0] * (HEAD_DIM ** -0.5)).astype(BF16)

    def scores(j):
        off = pl.multiple_of(j * tq, tq)
        kj = k_ref[0, pl.ds(off, tq), :].astype(BF16)
        ck = cumt_ref[0, pl.ds(hd % 8, 1), pl.ds(off, tq)]
        return lax.dot_general(qall, kj, NT, preferred_element_type=F32) - ck

    row = lax.broadcasted_iota(jnp.int32, (tq, tq), 0)
    col = lax.broadcasted_iota(jnp.int32, (tq, tq), 1)

    def v_blk(j):
        return v_ref[0, pl.ds(pl.multiple_of(j * tq, tq), tq), :].astype(BF16)

    def tile(nfull):
        for j in range(nfull):
            s_scr[j] = scores(j)
        s_scr[nfull] = jnp.where(col <= row, scores(nfull), NEG)
        o_ref[0] = _softmax_pv(s_scr, nfull + 1, 1, v_blk, m_scr, l_scr, acc_scr).astype(BF16)

    lax.switch(qi, [functools.partial(tile, n) for n in range(s_scr.shape[0])])


def fox_attention(z3, cumt):
    bsz, seq, _ = z3.shape
    tq = min(512, seq)
    full = lambda cb: pl.BlockSpec((1, seq, LANES), lambda b, h, i: (b, 0, cb + h))
    return pl.pallas_call(
        functools.partial(_fox_attn_body, tq=tq),
        grid=(bsz, N_HEADS, seq // tq),
        in_specs=[pl.BlockSpec((1, tq, LANES), lambda b, h, i: (b, i, CD_FQKV + h)),
                  full(CD_FQKV + 16), full(CD_FQKV + 32),
                  pl.BlockSpec((1, 8, seq), lambda b, h, i: (b, h // 8, 0))],
        out_specs=pl.BlockSpec((1, tq, LANES), lambda b, h, i: (b, i, h)),
        out_shape=jax.ShapeDtypeStruct((bsz, seq, N_HEADS * HEAD_DIM), BF16),
        scratch_shapes=_softmax_scratch(tq, seq // tq, tq),
        compiler_params=_cparams(("parallel", "parallel", "arbitrary")),
        name="fox_attention",
    )(z3, z3, z3, cumt)


def _nsa_cmp_body(zk_ref, zv_ref, pek_ref, pev_ref, wk1_ref, wk2_ref, wv1_ref, wv2_ref, kc_ref, vc_ref, *, nblk):
    def compress(z_ref, pe_ref, w1_ref, w2_ref):
        u1 = jnp.zeros((nblk, HEAD_DIM), F32)
        u2 = jnp.zeros((nblk, HEAD_DIM), F32)
        for l in range(S_CMP):
            zl = z_ref[0, :, l, :]
            u1 = u1 + jnp.dot(zl, w1_ref[l * HEAD_DIM:(l + 1) * HEAD_DIM, :], precision=HI,
                              preferred_element_type=F32)
            u2 = u2 + jnp.dot(zl, w1_ref[(S_CMP + l) * HEAD_DIM:(S_CMP + l + 1) * HEAD_DIM, :], precision=HI,
                              preferred_element_type=F32)
        pe = jnp.dot(jnp.broadcast_to(pe_ref[...], (8, L_CMP * HEAD_DIM)), w1_ref[...], precision=HI,
                     preferred_element_type=F32)[0:1]
        hmid = u1 + pltpu.roll(u2, nblk - 1, 0) + pe
        return jnp.dot(_silu(hmid), w2_ref[...], precision=HI, preferred_element_type=F32)

    kc_ref[0, 0] = compress(zk_ref, pek_ref, wk1_ref, wk2_ref)
    vc_ref[0, 0] = compress(zv_ref, pev_ref, wv1_ref, wv2_ref)


def nsa_compress(z3, pe_k, pe_v, wk1, wk2, wv1, wv2):
    bsz, seq, n = z3.shape
    nblk = seq // S_CMP
    z4 = z3.reshape(bsz, nblk, S_CMP, n)
    zspec = lambda cb: pl.BlockSpec((1, nblk, S_CMP, LANES), lambda b, g: (b, 0, 0, cb + g))
    wfull = lambda a: pl.BlockSpec(a.shape, lambda b, g: (0, 0))
    pek = pe_k.reshape(1, L_CMP * HEAD_DIM)
    pev = pe_v.reshape(1, L_CMP * HEAD_DIM)
    out = jax.ShapeDtypeStruct((bsz, G_NSA, nblk, HEAD_DIM), F32)
    ospec = pl.BlockSpec((1, 1, nblk, HEAD_DIM), lambda b, g: (b, g, 0, 0))
    return pl.pallas_call(
        functools.partial(_nsa_cmp_body, nblk=nblk),
        grid=(bsz, G_NSA),
        in_specs=[zspec(AB_NKV), zspec(AB_NKV + G_NSA), wfull(pek), wfull(pev),
                  wfull(wk1), wfull(wk2), wfull(wv1), wfull(wv2)],
        out_specs=[ospec, ospec],
        out_shape=[out, out],
        compiler_params=_cparams(("parallel", "parallel")),
        name="nsa_compress",
    )(z4, z4, pek, pev, wk1, wk2, wv1, wv2)


NSA_TQ = 128


def _nsa_expand(seq):
    m = np.arange(LANES)[:, None]
    t = np.arange(seq)[None, :]
    return jnp.asarray(((t // L_SLC) == m).astype(np.float32), dtype=BF16)


def _nsa_overlap(seq):
    cs = np.arange(seq // S_CMP)[:, None] * S_CMP
    ss = np.arange(LANES)[None, :] * L_SLC
    ov = (cs < ss + L_SLC) & (cs + L_CMP > ss) & (cs <= seq - L_CMP) & (ss < seq)
    return jnp.asarray(ov.astype(np.float32), dtype=BF16)


NSA_UNROLL = 2
NSA_WIN_BLOCKS = max(e // ATT_TK - max(e - (WINDOW + NSA_TQ - 2), 0) // ATT_TK + 1
                     for e in range(NSA_TQ - 1, 8 * WINDOW, NSA_TQ))


def _nsa_attn_body(q_ref, kc_ref, vc_ref, ks_ref, vs_ref, kw_ref, vw_ref, sm_ref, exp_ref, ovl_ref, o_ref,
                   s_scr, m_scr, l_scr, acc_scr, *, seq):
    g = pl.program_id(1)
    qi = pl.program_id(2)
    tq = NSA_TQ
    rows = HPG_NSA * tq
    nblk = seq // S_CMP
    n_slc = seq // L_SLC
    n_top = min(N_SEL, n_slc)
    q0 = qi * tq

    qall = q_ref[0]
    qs = jnp.concatenate([qall[:, p * HEAD_DIM:(p + 1) * HEAD_DIM] for p in range(HPG_NSA)], axis=0)
    qs = qs * (HEAD_DIM ** -0.5)
    qb = qs.astype(BF16)
    tpos = q0 + lax.broadcasted_iota(jnp.int32, (rows, 1), 0) % tq

    kc = kc_ref[0, 0]
    vc = vc_ref[0, 0]
    qh, ql = _split2(qs)
    kh, kl = _split2(kc)
    sc = lax.dot_general(jnp.concatenate([qh, qh, ql], axis=1), jnp.concatenate([kh, kl, kh], axis=1), NT,
                         preferred_element_type=F32)
    nidx = lax.broadcasted_iota(jnp.int32, (rows, nblk), 1)
    cmask = (nidx * S_CMP + (L_CMP - 1) <= tpos) & (nidx <= nblk - 2)
    scm = jnp.where(cmask, sc, NEG)
    mc = jnp.max(scm, -1, keepdims=True)
    ec = jnp.where(cmask, jnp.exp(scm - mc), 0.0)
    dc = jnp.sum(ec, -1, keepdims=True)
    p_cmp = ec / jnp.where(dc > 0, dc, 1.0)
    o_cmp = jnp.dot(p_cmp.astype(BF16), vc.astype(BF16), preferred_element_type=F32)

    psum = p_cmp[0:tq]
    for p in range(1, HPG_NSA):
        psum = psum + p_cmp[p * tq:(p + 1) * tq]
    ph, pl_ = _split2(psum)
    imp = jnp.dot(jnp.concatenate([ph, pl_], axis=1), jnp.concatenate([ovl_ref[...], ovl_ref[...]], axis=0),
                  preferred_element_type=F32)
    blk = lax.broadcasted_iota(jnp.int32, (tq, LANES), 1)
    cur = (q0 + lax.broadcasted_iota(jnp.int32, (tq, LANES), 0)) // L_SLC
    valid = blk <= cur
    forced = (blk == 0) | (blk == cur) | (blk == cur - 1)
    score = jnp.where(valid, jnp.where(forced, FORCE_SCORE, imp), NEG)
    rank = jnp.zeros((tq, LANES), jnp.int32)
    for mp in range(n_slc):
        colv = score[:, mp:mp + 1]
        rank = rank + ((colv > score) | ((colv == score) & (mp < blk))).astype(jnp.int32)
    selb = (valid & (rank < n_top)).astype(F32).astype(BF16)

    tk = ATT_TK
    qrow = q0 + lax.broadcasted_iota(jnp.int32, (tq, tk), 0)
    col = lax.broadcasted_iota(jnp.int32, (tq, tk), 1)

    def branch(k_ref, v_ref, first, nb, bias_fn):
        def scores(j, c):
            off = pl.multiple_of((first + j) * tk, tk)
            s = lax.dot_general(qb, k_ref[0, pl.ds(off, tk), :].astype(BF16), NT,
                                preferred_element_type=F32)
            bias = bias_fn(off)
            for p in range(HPG_NSA):
                s_scr[j, p * tq:(p + 1) * tq, :] = s[p * tq:(p + 1) * tq] + bias
            return c

        _for_blocks(nb, NSA_UNROLL, scores)

        def v_blk(j):
            return v_ref[0, pl.ds(pl.multiple_of((first + j) * tk, tk), tk), :].astype(BF16)

        return _softmax_pv(s_scr, nb, NSA_UNROLL, v_blk, m_scr, l_scr, acc_scr)

    def slc_bias(off):
        sel = jnp.dot(selb, exp_ref[:, pl.ds(off, tk)], preferred_element_type=F32)
        return jnp.where((sel > 0.5) & (off + col <= qrow), 0.0, NEG)

    last = (q0 + tq - 1) // tk
    o_slc = branch(ks_ref, vs_ref, 0, (last // NSA_UNROLL + 1) * NSA_UNROLL, slc_bias)

    def win_bias(off):
        d = qrow - (off + col)
        return jnp.where((d >= 0) & (d < WINDOW), 0.0, NEG)

    nwin = min(NSA_WIN_BLOCKS, seq // tk)
    o_win = branch(kw_ref, vw_ref, jnp.maximum(last - (nwin - 1), 0), nwin, win_bias)

    sg = jax.nn.sigmoid(sm_ref[0])
    for p in range(HPG_NSA):
        sl = slice(p * tq, (p + 1) * tq)
        base = 2 * N_HEADS + g * HPG_NSA + p
        o = (_lane_col(sg, blk, base) * o_cmp[sl] + _lane_col(sg, blk, base + N_HEADS) * o_slc[sl]
             + _lane_col(sg, blk, base + 2 * N_HEADS) * o_win[sl])
        o_ref[0, :, p * HEAD_DIM:(p + 1) * HEAD_DIM] = o.astype(BF16)


def nsa_attention(z3, kc, vc):
    bsz, seq, _ = z3.shape
    tq = NSA_TQ
    assert seq % tq == 0 and (seq // ATT_TK) % NSA_UNROLL == 0 and seq // L_SLC <= LANES
    nblk = seq // S_CMP
    full = lambda cb: pl.BlockSpec((1, seq, LANES), lambda b, g, i: (b, 0, cb + g))
    cspec = pl.BlockSpec((1, 1, nblk, HEAD_DIM), lambda b, g, i: (b, g, 0, 0))
    qw = HPG_NSA * HEAD_DIM
    return pl.pallas_call(
        functools.partial(_nsa_attn_body, seq=seq),
        grid=(bsz, G_NSA, seq // tq),
        in_specs=[pl.BlockSpec((1, tq, qw), lambda b, g, i: (b, i, AB_NQ * LANES // qw + g)),
                  cspec, cspec,
                  full(AB_NKV + 2 * G_NSA), full(AB_NKV + 3 * G_NSA),
                  full(AB_NKV + 4 * G_NSA), full(AB_NKV + 5 * G_NSA),
                  pl.BlockSpec((1, tq, LANES), lambda b, g, i: (b, i, AB_SMALL)),
                  pl.BlockSpec((LANES, seq), lambda b, g, i: (0, 0)),
                  pl.BlockSpec((nblk, LANES), lambda b, g, i: (0, 0))],
        out_specs=pl.BlockSpec((1, tq, qw), lambda b, g, i: (b, i, g)),
        out_shape=jax.ShapeDtypeStruct((bsz, seq, N_HEADS * HEAD_DIM), BF16),
        scratch_shapes=_softmax_scratch(HPG_NSA * tq, seq // ATT_TK, ATT_TK),
        compiler_params=_cparams(("parallel", "parallel", "arbitrary")),
        name="nsa_attention",
    )(z3, kc, vc, z3, z3, z3, z3, z3, _nsa_expand(seq), _nsa_overlap(seq))


def _repack_body(w_ref, o_ref, *, sections, pad_from):
    o_ref[pad_from:, :] = jnp.zeros((o_ref.shape[0] - pad_from, o_ref.shape[1]), BF16)
    for dst, src, width in sections:
        o_ref[dst:dst + width, :] = w_ref[src:src + width, :].astype(BF16)


def _repack_weight(w, sections, n_out, pad_from, tc=256):
    wt = jnp.swapaxes(w, 0, 1)
    n_in, k = wt.shape
    return pl.pallas_call(
        functools.partial(_repack_body, sections=sections, pad_from=pad_from),
        grid=(k // tc,),
        in_specs=[pl.BlockSpec((n_in, tc), lambda i: (0, i))],
        out_specs=pl.BlockSpec((n_out, tc), lambda i: (0, i)),
        out_shape=jax.ShapeDtypeStruct((n_out, k), BF16),
        compiler_params=_cparams(("parallel",)),
        name="repack_weight",
    )(wt)


def _ab_weight(w):
    small = AB_SMALL * LANES
    sections = ((0, 0, 6144), (6144, 6176, 2048), (8192, 8224, 2048), (10240, 10272, 3072),
                (small, 6144, 32), (small + 32, 13344, 48))
    return _repack_weight(w, sections, AB_N, small)


def _cd_weight(w):
    small = CD_SMALL * LANES
    return _repack_weight(w, ((0, 0, small), (small, small, 16)), CD_N, small)


def _row128(v):
    return jnp.pad(v.astype(F32), (0, LANES - v.shape[0])).reshape(1, LANES)


def kernel(x, p, ab_norm_pre, ab_norm_post, ab_w_in, gdn_conv_w, gdn_a_log, gdn_dt_bias, gdn_norm, nsa_pe_k, nsa_pe_v, nsa_cmp_k1, nsa_cmp_k2, nsa_cmp_v1, nsa_cmp_v2, ab_w_out, cd_norm_pre, cd_norm_post, cd_w_in, hgrn_lb_logits, hgrn_norm, fox_f_bias, cd_w_out, ffn_norm_pre, ffn_norm_post, ffn_w_up, ffn_conv_w, ffn_conv_b, ffn_w_down, ple_w_proj, ple_gate_norm, ple_w_gate, ple_norm_post):
    bsz, seq, dm = x.shape
    depth = p.shape[0]
    m = bsz * seq
    half = N_HEADS * HEAD_DIM
    sm_ = jax.nn.softmax(hgrn_lb_logits.astype(F32), axis=0)
    lb_table = jnp.cumsum(sm_, axis=0) - sm_[0]
    xf = x.reshape(m, dm)
    pf = p.reshape(depth, m, -1)
    w_up, w_down = ffn_w_up.astype(BF16), ffn_w_down.astype(BF16)
    w_gate, w_proj = ple_w_gate.astype(BF16), ple_w_proj.astype(BF16)
    for li in range(depth):
        j = li // 2
        if li % 2 == 0:
            z3 = norm_matmul(xf, ab_norm_pre[j], _ab_weight(ab_w_in[j])).reshape(bsz, seq, AB_N)
            prm = jnp.concatenate([_row128(gdn_a_log[j]), _row128(gdn_dt_bias[j]),
                                   jnp.zeros((6, LANES), F32)], axis=0)
            o_a = gated_deltanet(z3, gdn_conv_w[j], prm, gdn_norm[j])
            kc, vc = nsa_compress(z3, nsa_pe_k[j], nsa_pe_v[j], nsa_cmp_k1[j], nsa_cmp_k2[j],
                                  nsa_cmp_v1[j], nsa_cmp_v2[j])
            o_b = nsa_attention(z3, kc, vc)
            w_out, post = ab_w_out[j], ab_norm_post[j]
        else:
            z3 = norm_matmul(xf, cd_norm_pre[j], _cd_weight(cd_w_in[j])).reshape(bsz, seq, CD_N)
            o_a = hgrn2(z3, lb_table[li], hgrn_norm[j])
            o_b = fox_attention(z3, fox_gates(z3, _row128(fox_f_bias[j])))
            w_out, post = cd_w_out[j], cd_norm_post[j]
        xf = outproj(o_a.reshape(m, half), o_b.reshape(m, half), w_out.astype(BF16), xf, post)
        xf = conv_ffn(xf, li, ffn_norm_pre, w_up, ffn_conv_w, ffn_conv_b, w_down, ffn_norm_post, seq)
        xf = ple(xf, li, pf, ple_gate_norm, w_gate, w_proj, ple_norm_post)
    return xf.reshape(bsz, seq, dm)
```
